```python
import jax
import jax.numpy as jnp
from jax import lax
import numpy as np

D_MODEL = 1024
BATCH = 4
SEQ = 4096
DEPTH = 2
DEC_BATCH = 32
DEC_SEQ = 8
PAST_LEN = 8192
PAGE_SIZE = 128

HEAD_DIM = 64
ROT_DIM = HEAD_DIM // 4
ROPE_THETA = 500000.0
N_BRANCH = 4
BRANCH_CH = 256
CONV_CH = BRANCH_CH
CONV_W = 3
GMLP_CH = BRANCH_CH
GMLP_GROUPS = 4
GMLP_CHUNK = 128
NSA_HEADS = BRANCH_CH // HEAD_DIM
NSA_CMP_LEN = 32
NSA_CMP_STRIDE = 16
NSA_CMP_HIDDEN = 256
NSA_SLC_BLOCK = 64
NSA_TOPN = 16
NSA_WINDOW = 512
MOBA_HEADS = BRANCH_CH // HEAD_DIM
MOBA_BLOCK = 256
MOBA_TOPK = 3
D_FF = 2816
FFN_CONV_W = 3
Q_BLOCK = 128
MOBA_Q_BLOCK = 64
EPS = 1e-6
NEG = -1e30
BIG = 1e30
PROJ_SIZES = (CONV_CH, CONV_CH, CONV_CH, 2 * GMLP_CH, NSA_HEADS * HEAD_DIM, 3 * HEAD_DIM, 3 * HEAD_DIM, 3 * NSA_HEADS, MOBA_HEADS * HEAD_DIM, MOBA_HEADS * HEAD_DIM, MOBA_HEADS * HEAD_DIM, N_BRANCH * D_MODEL)
PROJ_COLS = sum(PROJ_SIZES)

kernel_name = 'hybrid_conv_gmlp_nsa_moba_decoder_step'


def split_cols(t, sizes):
    out, o = [], 0
    for s in sizes:
        out.append(t[..., o:o + s])
        o += s
    return out


def rmsnorm(x, g):
    xf = x.astype(jnp.float32)
    y = xf * lax.rsqrt(jnp.mean(xf * xf, axis=-1, keepdims=True) + EPS)
    return (y * g.astype(jnp.float32)).astype(x.dtype)


def layernorm(x, g, b):
    xf = x.astype(jnp.float32)
    xc = xf - jnp.mean(xf, axis=-1, keepdims=True)
    y = xc * lax.rsqrt(jnp.mean(xc * xc, axis=-1, keepdims=True) + EPS)
    return (y * g.astype(jnp.float32) + b.astype(jnp.float32)).astype(x.dtype)


def rope_partial(x, pos):
    half = ROT_DIM // 2
    inv = jnp.power(jnp.float32(ROPE_THETA), -jnp.arange(half, dtype=jnp.float32) / half)
    ang = pos.astype(jnp.float32)[:, None] * inv[None, :]
    cos = jnp.cos(ang)[:, None, :]
    sin = jnp.sin(ang)[:, None, :]
    xf = x.astype(jnp.float32)
    x1, x2 = xf[..., :half], xf[..., half:ROT_DIM]
    out = jnp.concatenate([x1 * cos - x2 * sin, x2 * cos + x1 * sin, xf[..., ROT_DIM:]], axis=-1)
    return out.astype(x.dtype)


def causal_dwconv(z, buf, w, b):
    t = z.shape[1]
    zz = jnp.concatenate([buf.astype(z.dtype), z], axis=1)
    y = zz[:, 0:t] * w[0] + b
    for i in range(1, w.shape[0]):
        y = y + zz[:, i:i + t] * w[i]
    return y, zz[:, t:]


def gmlp_mix(v, w_s, b_s):
    bn, t, c = v.shape
    nc = -(-t // GMLP_CHUNK)
    vp = jnp.pad(v, ((0, 0), (0, nc * GMLP_CHUNK - t), (0, 0)))
    vp = vp.reshape(bn, nc, GMLP_CHUNK, GMLP_GROUPS, c // GMLP_GROUPS)
    tril = jnp.tril(jnp.ones((GMLP_CHUNK, GMLP_CHUNK), dtype=bool))
    ws = jnp.where(tril[None], w_s, 0.0).astype(v.dtype)
    s = jnp.einsum('gij,bcjgd->bcigd', ws, vp) + b_s.T[None, None, :, :, None].astype(v.dtype)
    return s.reshape(bn, nc * GMLP_CHUNK, c)[:, :t]


def nsa_compress(k, pos_emb, w1, w2):
    bn, l, dh = k.shape
    nc = (l - NSA_CMP_LEN) // NSA_CMP_STRIDE + 1
    idx = np.arange(nc)[:, None] * NSA_CMP_STRIDE + np.arange(NSA_CMP_LEN)[None, :]
    blocks = k[:, idx] + pos_emb.astype(k.dtype)
    hdn = jax.nn.gelu(blocks.reshape(bn, nc, NSA_CMP_LEN * dh) @ w1)
    return hdn @ w2


def nsa_core(q, qpos, gates, kc, vc, k_slc, v_slc, k_win, v_win, win_pos):
    bn, tq = q.shape[0], q.shape[1]
    f32 = jnp.float32
    qf = q.astype(f32) * (HEAD_DIM ** -0.5)
    nc = kc.shape[1]
    c_start = jnp.arange(nc, dtype=jnp.int32) * NSA_CMP_STRIDE
    c_end = c_start + (NSA_CMP_LEN - 1)
    c_ok = c_end[None, :] <= qpos[:, None]
    s = jnp.einsum('bqhd,bnd->bhqn', qf, kc.astype(f32))
    p_cmp = jnp.where(c_ok, jax.nn.softmax(jnp.where(c_ok, s, NEG), axis=-1), 0.0)
    o_cmp = jnp.einsum('bhqn,bnd->bqhd', p_cmp, vc.astype(f32))
    ns = k_slc.shape[1] // NSA_SLC_BLOCK
    b_start = jnp.arange(ns, dtype=jnp.int32) * NSA_SLC_BLOCK
    cover = ((c_start[:, None] <= b_start[None, :] + NSA_SLC_BLOCK - 1) & (c_end[:, None] >= b_start[None, :])).astype(f32)
    imp = jnp.einsum('bhqn,nj->bqj', p_cmp, cover)
    cur = qpos // NSA_SLC_BLOCK
    blk = jnp.arange(ns, dtype=jnp.int32)
    forced = (blk[None, :] == 0) | (blk[None, :] == cur[:, None]) | (blk[None, :] == cur[:, None] - 1)
    imp = jnp.where(forced[None], BIG, imp)
    imp = jnp.where((blk[None, :] > cur[:, None])[None], NEG, imp)
    _, idx = lax.top_k(imp, min(NSA_TOPN, ns))
    kb = k_slc.reshape(bn, ns, NSA_SLC_BLOCK, HEAD_DIM)
    vb = v_slc.reshape(bn, ns, NSA_SLC_BLOCK, HEAD_DIM)
    bi = jnp.arange(bn)[:, None, None]
    ks = kb[bi, idx].reshape(bn, tq, -1, HEAD_DIM)
    vs = vb[bi, idx].reshape(bn, tq, -1, HEAD_DIM)
    kpos = (idx[..., None] * NSA_SLC_BLOCK + jnp.arange(NSA_SLC_BLOCK, dtype=jnp.int32)).reshape(bn, tq, -1)
    s_ok = kpos <= qpos[None, :, None]
    s = jnp.einsum('bqhd,bqkd->bqhk', qf, ks.astype(f32))
    p = jax.nn.softmax(jnp.where(s_ok[:, :, None, :], s, NEG), axis=-1)
    o_slc = jnp.einsum('bqhk,bqkd->bqhd', p, vs.astype(f32))
    w_ok = (win_pos[None, :] <= qpos[:, None]) & (win_pos[None, :] > qpos[:, None] - NSA_WINDOW) & (win_pos[None, :] >= 0)
    s = jnp.einsum('bqhd,bkd->bqhk', qf, k_win.astype(f32))
    p = jax.nn.softmax(jnp.where(w_ok[None, :, None, :], s, NEG), axis=-1)
    o_win = jnp.einsum('bqhk,bkd->bqhd', p, v_win.astype(f32))
    g = gates.astype(f32)
    o = g[..., 0:1] * o_cmp + g[..., 1:2] * o_slc + g[..., 2:3] * o_win
    return o.astype(q.dtype)


def moba_core(q, qpos, kb, vb, kmean):
    bn, tq, nh = q.shape[0], q.shape[1], q.shape[2]
    nb = kb.shape[2]
    f32 = jnp.float32
    qf = q.astype(f32)
    cur = qpos // MOBA_BLOCK
    past = jnp.arange(nb, dtype=jnp.int32)[None, :] < cur[:, None]
    gs = jnp.einsum('bqhd,bhnd->bhqn', qf, kmean)
    _, idx = lax.top_k(jnp.where(past[None, None], gs, NEG), min(MOBA_TOPK, nb))
    own = jnp.broadcast_to(cur[None, None, :, None], (bn, nh, tq, 1)).astype(idx.dtype)
    blk = jnp.concatenate([idx, own], axis=-1)
    blk_ok = jnp.concatenate([idx < cur[None, None, :, None], jnp.ones(own.shape, dtype=bool)], axis=-1)
    bi = jnp.arange(bn)[:, None, None, None]
    hi = jnp.arange(nh)[None, :, None, None]
    ks = kb[bi, hi, blk]
    vs = vb[bi, hi, blk]
    kpos = blk[..., None] * MOBA_BLOCK + jnp.arange(MOBA_BLOCK, dtype=jnp.int32)
    ok = blk_ok[..., None] & (kpos <= qpos[None, None, :, None, None])
    s = jnp.einsum('bqhd,bhqnkd->bhqnk', qf * (HEAD_DIM ** -0.5), ks.astype(f32))
    s = jnp.where(ok, s, NEG)
    p = jax.nn.softmax(s.reshape(bn, nh, tq, -1), axis=-1).reshape(ok.shape)
    o = jnp.einsum('bhqnk,bhqnkd->bqhd', p, vs.astype(f32))
    return o.astype(q.dtype)


def gather_pages(cache_l, page_table):
    g = cache_l[page_table]
    return g.reshape((g.shape[0], g.shape[1] * g.shape[2]) + g.shape[3:])


def hybrid_layer(x, pos, lp, conv_buf, ffn_buf, nsa_past, win_past, moba_past):
    bn, t, _ = x.shape
    dt = x.dtype
    h = rmsnorm(x, lp['norm1'])
    proj = h @ lp['w_in']
    a_h, a_b, a_c, b_uv, c_q, c_k, c_v, c_g, d_q, d_k, d_v, br_g = split_cols(proj, PROJ_SIZES)
    ya, conv_new = causal_dwconv(a_c * a_h, conv_buf, lp['conv_w'], lp['conv_b'])
    o_a = a_b * ya
    uv = jax.nn.gelu(b_uv)
    u, v = uv[..., :GMLP_CH], uv[..., GMLP_CH:]
    vn = layernorm(v, lp['gmlp_ln_g'], lp['gmlp_ln_b'])
    o_b = u * gmlp_mix(vn, lp['gmlp_ws'], lp['gmlp_bs'])
    qn = rope_partial(rmsnorm(c_q.reshape(bn, t, NSA_HEADS, HEAD_DIM), lp['nsa_q_gain']), pos)
    kn = rope_partial(rmsnorm(c_k.reshape(bn, t, 3, HEAD_DIM), lp['nsa_k_gain']), pos)
    vv = c_v.reshape(bn, t, 3, HEAD_DIM)
    nsa_rows = jnp.stack([kn[:, :, 0], vv[:, :, 0], kn[:, :, 1], vv[:, :, 1]], axis=2)
    win_rows = jnp.stack([kn[:, :, 2], vv[:, :, 2]], axis=2)
    nsa_all = nsa_rows if nsa_past is None else jnp.concatenate([nsa_past.astype(dt), nsa_rows], axis=1)
    l_all = nsa_all.shape[1]
    kc = nsa_compress(nsa_all[:, :, 0], lp['cmp_pos'][0], lp['cmp_w1'][0], lp['cmp_w2'][0])
    vc = nsa_compress(nsa_all[:, :, 1], lp['cmp_pos'][1], lp['cmp_w1'][1], lp['cmp_w2'][1])
    ns = -(-l_all // NSA_SLC_BLOCK)
    pad_s = ((0, 0), (0, ns * NSA_SLC_BLOCK - l_all), (0, 0))
    k_slc = jnp.pad(nsa_all[:, :, 2], pad_s)
    v_slc = jnp.pad(nsa_all[:, :, 3], pad_s)
    gates_c = jax.nn.sigmoid(c_g).reshape(bn, t, NSA_HEADS, 3)
    if win_past is None:
        kw_pad = jnp.pad(win_rows, ((0, 0), (NSA_WINDOW, 0), (0, 0), (0, 0)))

        def nsa_blk(i):
            s0 = i * Q_BLOCK
            qp = s0 + jnp.arange(Q_BLOCK, dtype=jnp.int32)
            wr = lax.dynamic_slice_in_dim(kw_pad, s0, NSA_WINDOW + Q_BLOCK, axis=1)
            wp = s0 - NSA_WINDOW + jnp.arange(NSA_WINDOW + Q_BLOCK, dtype=jnp.int32)
            return nsa_core(lax.dynamic_slice_in_dim(qn, s0, Q_BLOCK, axis=1), qp,
                            lax.dynamic_slice_in_dim(gates_c, s0, Q_BLOCK, axis=1),
                            kc, vc, k_slc, v_slc, wr[:, :, 0], wr[:, :, 1], wp)

        o_c = lax.map(nsa_blk, jnp.arange(t // Q_BLOCK, dtype=jnp.int32))
        o_c = jnp.moveaxis(o_c, 0, 1).reshape(bn, t, NSA_HEADS * HEAD_DIM)
        win_all = win_rows
    else:
        win_all = jnp.concatenate([win_past.astype(dt), win_rows], axis=1)
        wp = pos[0] - win_past.shape[1] + jnp.arange(win_all.shape[1], dtype=jnp.int32)
        o_c = nsa_core(qn, pos, gates_c, kc, vc, k_slc, v_slc, win_all[:, :, 0], win_all[:, :, 1], wp)
        o_c = o_c.reshape(bn, t, NSA_HEADS * HEAD_DIM)
    win_new = win_all[:, -min(NSA_WINDOW, win_all.shape[1]):]
    qm = rope_partial(rmsnorm(d_q.reshape(bn, t, MOBA_HEADS, HEAD_DIM), lp['moba_q_gain']), pos)
    km = rope_partial(rmsnorm(d_k.reshape(bn, t, MOBA_HEADS, HEAD_DIM), lp['moba_k_gain']), pos)
    moba_rows = jnp.stack([km, d_v.reshape(bn, t, MOBA_HEADS, HEAD_DIM)], axis=2)
    moba_all = moba_rows if moba_past is None else jnp.concatenate([moba_past.astype(dt), moba_rows], axis=1)
    l_m = moba_all.shape[1]
    nb = -(-l_m // MOBA_BLOCK)
    mp = jnp.pad(moba_all, ((0, 0), (0, nb * MOBA_BLOCK - l_m), (0, 0), (0, 0), (0, 0)))
    mp = mp.reshape(bn, nb, MOBA_BLOCK, 2, MOBA_HEADS, HEAD_DIM)
    kb = jnp.transpose(mp[:, :, :, 0], (0, 3, 1, 2, 4))
    vb = jnp.transpose(mp[:, :, :, 1], (0, 3, 1, 2, 4))
    kmean = jnp.mean(kb.astype(jnp.float32), axis=3)
    if moba_past is None:
        def moba_blk(i):
            s0 = i * MOBA_Q_BLOCK
            qp = s0 + jnp.arange(MOBA_Q_BLOCK, dtype=jnp.int32)
            return moba_core(lax.dynamic_slice_in_dim(qm, s0, MOBA_Q_BLOCK, axis=1), qp, kb, vb, kmean)

        o_d = lax.map(moba_blk, jnp.arange(t // MOBA_Q_BLOCK, dtype=jnp.int32))
        o_d = jnp.moveaxis(o_d, 0, 1).reshape(bn, t, MOBA_HEADS * HEAD_DIM)
    else:
        o_d = moba_core(qm, pos, kb, vb, kmean).reshape(bn, t, MOBA_HEADS * HEAD_DIM)
    outs = jnp.stack([o_a, o_b, o_c, o_d], axis=2)
    br = jnp.einsum('btnc,ncd->btnd', outs, lp['w_branch'])
    g = jax.nn.sigmoid(br_g.reshape(bn, t, N_BRANCH, D_MODEL))
    x = x + jnp.sum(g * br, axis=2) @ lp['w_o']
    h2 = rmsnorm(x, lp['norm2'])
    ac, ffn_new = causal_dwconv(h2 @ lp['w_ffn_gate'], ffn_buf, lp['ffn_conv_w'], lp['ffn_conv_b'])
    x = x + (jax.nn.silu(ac) * (h2 @ lp['w_ffn_up'])) @ lp['w_ffn_down']
    return x, (nsa_rows, moba_rows, win_new, conv_new, ffn_new, vn)


def setup_inputs(seed: int = 0) -> dict:
    key = jax.random.key(seed)
    ks = jax.random.split(key, 40)
    f32 = jnp.float32

    def nrm(k, shape, scale):
        return jax.random.normal(k, shape, f32) * scale

    n_pages = PAST_LEN // PAGE_SIZE
    n_used = DEC_BATCH * n_pages
    n_pool = n_used + (n_used + 3) // 4
    wbuf = min(NSA_WINDOW, PAST_LEN)
    page_table = jax.random.permutation(ks[0], n_pool)[:n_used].reshape(DEC_BATCH, n_pages).astype(jnp.int32)
    cmp_in = NSA_CMP_LEN * HEAD_DIM
    return {
        'x_prompt': nrm(ks[1], (BATCH, SEQ, D_MODEL), 1.0),
        'x_sample': nrm(ks[2], (DEC_BATCH, DEC_SEQ, D_MODEL), 1.0),
        'cache_nsa_kv': nrm(ks[3], (DEPTH, n_pool, PAGE_SIZE, 4, HEAD_DIM), 1.0),
        'cache_moba_kv': nrm(ks[4], (DEPTH, n_pool, PAGE_SIZE, 2, MOBA_HEADS, HEAD_DIM), 1.0),
        'state_nsa_win': nrm(ks[5], (DEPTH, DEC_BATCH, wbuf, 2, HEAD_DIM), 1.0),
        'state_conv': nrm(ks[6], (DEPTH, DEC_BATCH, CONV_W - 1, CONV_CH), 1.0),
        'state_ffn_conv': nrm(ks[7], (DEPTH, DEC_BATCH, FFN_CONV_W - 1, D_FF), 1.0),
        'page_table': page_table,
        'norm1': 1.0 + nrm(ks[8], (DEPTH, D_MODEL), 0.02),
        'w_in': nrm(ks[9], (DEPTH, D_MODEL, PROJ_COLS), D_MODEL ** -0.5),
        'conv_w': nrm(ks[10], (DEPTH, CONV_W, CONV_CH), CONV_W ** -0.5),
        'conv_b': nrm(ks[11], (DEPTH, CONV_CH), 0.01),
        'gmlp_ln_g': 1.0 + nrm(ks[12], (DEPTH, GMLP_CH), 0.02),
        'gmlp_ln_b': nrm(ks[13], (DEPTH, GMLP_CH), 0.01),
        'gmlp_ws': nrm(ks[14], (DEPTH, GMLP_GROUPS, GMLP_CHUNK, GMLP_CHUNK), GMLP_CHUNK ** -0.5),
        'gmlp_bs': 1.0 + nrm(ks[15], (DEPTH, GMLP_GROUPS, GMLP_CHUNK), 0.01),
        'nsa_q_gain': 1.0 + nrm(ks[16], (DEPTH, HEAD_DIM), 0.02),
        'nsa_k_gain': 1.0 + nrm(ks[17], (DEPTH, 3, HEAD_DIM), 0.02),
        'nsa_cmp_pos': nrm(ks[18], (DEPTH, 2, NSA_CMP_LEN, HEAD_DIM), 0.02),
        'nsa_cmp_w1': nrm(ks[19], (DEPTH, 2, cmp_in, NSA_CMP_HIDDEN), cmp_in ** -0.5),
        'nsa_cmp_w2': nrm(ks[20], (DEPTH, 2, NSA_CMP_HIDDEN, HEAD_DIM), NSA_CMP_HIDDEN ** -0.5),
        'moba_q_gain': 1.0 + nrm(ks[21], (DEPTH, HEAD_DIM), 0.02),
        'moba_k_gain': 1.0 + nrm(ks[22], (DEPTH, HEAD_DIM), 0.02),
        'w_branch': nrm(ks[23], (DEPTH, N_BRANCH, BRANCH_CH, D_MODEL), BRANCH_CH ** -0.5),
        'w_o': nrm(ks[24], (DEPTH, D_MODEL, D_MODEL), D_MODEL ** -0.5),
        'norm2': 1.0 + nrm(ks[25], (DEPTH, D_MODEL), 0.02),
        'w_ffn_gate': nrm(ks[26], (DEPTH, D_MODEL, D_FF), D_MODEL ** -0.5),
        'w_ffn_up': nrm(ks[27], (DEPTH, D_MODEL, D_FF), D_MODEL ** -0.5),
        'ffn_conv_w': nrm(ks[28], (DEPTH, FFN_CONV_W, D_FF), FFN_CONV_W ** -0.5),
        'ffn_conv_b': nrm(ks[29], (DEPTH, D_FF), 0.01),
        'w_ffn_down': nrm(ks[30], (DEPTH, D_FF, D_MODEL), D_FF ** -0.5),
    }


def reference(x_prompt, x_sample, cache_nsa_kv, cache_moba_kv, state_nsa_win, state_conv, state_ffn_conv, page_table,
              norm1, w_in, conv_w, conv_b, gmlp_ln_g, gmlp_ln_b, gmlp_ws, gmlp_bs, nsa_q_gain, nsa_k_gain,
              nsa_cmp_pos, nsa_cmp_w1, nsa_cmp_w2, moba_q_gain, moba_k_gain, w_branch, w_o, norm2,
              w_ffn_gate, w_ffn_up, ffn_conv_w, ffn_conv_b, w_ffn_down):
    bp, tp = x_prompt.shape[0], x_prompt.shape[1]
    pos_p = jnp.arange(tp, dtype=jnp.int32)
    pos_s = PAST_LEN + jnp.arange(x_sample.shape[1], dtype=jnp.int32)
    yp, ys = x_prompt, x_sample
    sp_list, ss_list = [], []
    for l in range(DEPTH):
        lp = dict(norm1=norm1[l], w_in=w_in[l], conv_w=conv_w[l], conv_b=conv_b[l],
                  gmlp_ln_g=gmlp_ln_g[l], gmlp_ln_b=gmlp_ln_b[l], gmlp_ws=gmlp_ws[l], gmlp_bs=gmlp_bs[l],
                  nsa_q_gain=nsa_q_gain[l], nsa_k_gain=nsa_k_gain[l], cmp_pos=nsa_cmp_pos[l],
                  cmp_w1=nsa_cmp_w1[l], cmp_w2=nsa_cmp_w2[l], moba_q_gain=moba_q_gain[l],
                  moba_k_gain=moba_k_gain[l], w_branch=w_branch[l], w_o=w_o[l], norm2=norm2[l],
                  w_ffn_gate=w_ffn_gate[l], w_ffn_up=w_ffn_up[l], ffn_conv_w=ffn_conv_w[l],
                  ffn_conv_b=ffn_conv_b[l], w_ffn_down=w_ffn_down[l])
        zc = jnp.zeros((bp, CONV_W - 1, CONV_CH), yp.dtype)
        zf = jnp.zeros((bp, FFN_CONV_W - 1, D_FF), yp.dtype)
        yp, sp = hybrid_layer(yp, pos_p, lp, zc, zf, None, None, None)
        nsa_past = gather_pages(cache_nsa_kv[l], page_table)
        moba_past = gather_pages(cache_moba_kv[l], page_table)
        ys, ss = hybrid_layer(ys, pos_s, lp, state_conv[l], state_ffn_conv[l], nsa_past, state_nsa_win[l], moba_past)
        sp_list.append(sp)
        ss_list.append(ss)
    nsa_kv_p = jnp.stack([s[0] for s in sp_list])
    nsa_kv_s = jnp.stack([s[0] for s in ss_list])
    moba_kv_p = jnp.stack([s[1] for s in sp_list])
    moba_kv_s = jnp.stack([s[1] for s in ss_list])
    nsa_win_p = jnp.stack([s[2] for s in sp_list])
    nsa_win_s = jnp.stack([s[2] for s in ss_list])
    conv_p = jnp.stack([s[3] for s in sp_list])
    conv_s = jnp.stack([s[3] for s in ss_list])
    ffn_conv_p = jnp.stack([s[4] for s in sp_list])
    ffn_conv_s = jnp.stack([s[4] for s in ss_list])
    gmlp_v_s = jnp.stack([s[5] for s in ss_list])
    return (yp, ys, nsa_kv_p, nsa_kv_s, moba_kv_p, moba_kv_s, nsa_win_p, nsa_win_s, conv_p, conv_s, ffn_conv_p, ffn_conv_s, gmlp_v_s)
```

```python
import functools

import jax
import jax.numpy as jnp
from jax import lax
from jax.experimental import pallas as pl
from jax.experimental.pallas import tpu as pltpu

F32 = jnp.float32
BF16 = jnp.bfloat16

D_MODEL = 1024
HEAD_DIM = 64
ROT_DIM = HEAD_DIM // 4
ROPE_THETA = 500000.0
PAGE_SIZE = 128
BRANCH_CH = 256
N_BRANCH = 4
GMLP_GROUPS = 4
GMLP_CHUNK = 128
NSA_HEADS = 4
NSA_CMP_LEN = 32
NSA_CMP_STRIDE = 16
NSA_CMP_HIDDEN = 256
NSA_SLC_BLOCK = 64
NSA_TOPN = 16
NSA_WINDOW = 512
MOBA_HEADS = 4
MOBA_BLOCK = 256
MOBA_TOPK = 3
D_FF = 2816
EPS = 1e-6
NEG = -1e30
BIG = 1e30

LANES = 128
SUBLANES = 8
VMEM_LIMIT = 56 * 1024 * 1024

C_AH, C_AB, C_AC, C_U, C_V, C_Q, C_NK, C_MK, C_G, C_END = 0, 256, 512, 768, 1024, 1280, 1792, 2176, 2688, 2816

_NT = (((1,), (1,)), ((), ()))


def _dot(a, b):
    return jnp.dot(a, b, preferred_element_type=F32)


def _dot_nt(a, b):
    return lax.dot_general(a, b, _NT, preferred_element_type=F32)


def _split(a):
    hi = a.astype(BF16)
    lo = (a - hi.astype(F32)).astype(BF16)
    return hi, lo


def _dot_hilo(a, b_exact):
    hi, lo = _split(a)
    return _dot(hi, b_exact) + _dot(lo, b_exact)


def _dot_nt_3pass(a, b):
    ah, al = _split(a)
    bh, bl = _split(b)
    return _dot_nt(ah, bh) + _dot_nt(ah, bl) + _dot_nt(al, bh)


def _iota(shape, dim):
    return lax.broadcasted_iota(jnp.int32, shape, dim)


def _head_sumsq(x):
    w = x.shape[1]
    x2 = (x * x).astype(BF16)
    outs = []
    for c0 in range(0, w, 256):
        cw = min(256, w - c0)
        bd = (_iota((cw, cw), 0) // HEAD_DIM == _iota((cw, cw), 1) // HEAD_DIM).astype(BF16)
        outs.append(_dot(x2[:, c0:c0 + cw], bd))
    return outs[0] if len(outs) == 1 else jnp.concatenate(outs, axis=1)


def _norm_rope(x, gain, cos, sin_a, sin_b):
    y = x * lax.rsqrt(_head_sumsq(x) * (1.0 / HEAD_DIM) + EPS) * gain
    half = ROT_DIM // 2
    slabs = []
    for s in range(x.shape[1] // LANES):
        ys = y[:, s * LANES:(s + 1) * LANES]
        slabs.append(ys * cos + pltpu.roll(ys, LANES - half, 1) * sin_a + pltpu.roll(ys, half, 1) * sin_b)
    return jnp.concatenate(slabs, axis=1)


def _shift_rows(z, prev1, prev2, seg):
    rows = _iota(z.shape, 0)
    z1 = pltpu.roll(z, 1, 0)
    z2 = pltpu.roll(z, 2, 0)
    if seg is None:
        z1 = jnp.where(rows == 0, prev1, z1)
        z2 = jnp.where(rows == 0, prev2, jnp.where(rows == 1, prev1, z2))
    else:
        t = rows % seg
        z1 = jnp.where(t == 0, prev1, z1)
        z2 = jnp.where(t <= 1, prev2, z2)
    return z1, z2


def _proj_kernel(*refs, tiles_per_seq, chunk, seg):
    it = iter(refs)
    x_ref, n1_ref, w1_ref, cw_ref, cb_ref, lng_ref, lnb_ref, mix_ref, mixb_ref = (next(it) for _ in range(9))
    gq_ref, gnk_ref, gmk_ref, cos_ref, sa_ref, sb_ref = (next(it) for _ in range(6))
    if seg is not None:
        h1_ref, h2_ref = next(it), next(it)
    oab_ref, qq_ref, nsa_ref, win_ref, moba_ref, gate_ref, ztail_ref = (next(it) for _ in range(7))
    if seg is not None:
        vn_ref = next(it)
    else:
        zprev_ref = next(it)

    x = x_ref[...]
    tm = x.shape[0]
    xn = x * lax.rsqrt(jnp.mean(x * x, axis=-1, keepdims=True) + EPS) * n1_ref[...]
    proj = _dot(xn.astype(BF16), w1_ref[...])

    z = proj[:, C_AC:C_AC + 256] * proj[:, C_AH:C_AH + 256]
    if seg is None:
        @pl.when(pl.program_id(0) % tiles_per_seq == 0)
        def _():
            zprev_ref[...] = jnp.zeros_like(zprev_ref)

        z1, z2 = _shift_rows(z, zprev_ref[SUBLANES - 1:SUBLANES, :], zprev_ref[SUBLANES - 2:SUBLANES - 1, :], None)
        zprev_ref[...] = z[tm - SUBLANES:, :]
        ztail_ref[0] = z[tm - SUBLANES:, :]
    else:
        z1, z2 = _shift_rows(z, h1_ref[...], h2_ref[...], seg)
        ztail_ref[...] = z.reshape(ztail_ref.shape)
    ya = z2 * cw_ref[0:1, :] + cb_ref[...] + z1 * cw_ref[1:2, :] + z * cw_ref[2:3, :]
    oab_ref[:, 0:256] = (proj[:, C_AB:C_AB + 256] * ya).astype(BF16)

    u = jax.nn.gelu(proj[:, C_U:C_U + 256])
    v = jax.nn.gelu(proj[:, C_V:C_V + 256])
    vc = v - jnp.mean(v, axis=-1, keepdims=True)
    vn = vc * lax.rsqrt(jnp.mean(vc * vc, axis=-1, keepdims=True) + EPS) * lng_ref[...] + lnb_ref[...]
    if seg is not None:
        vn_ref[...] = vn
    vnb = vn.astype(BF16)
    grp = _iota((chunk, 256), 1) // HEAD_DIM
    for c in range(tm // chunk):
        r = _dot(mix_ref[...], vnb[c * chunk:(c + 1) * chunk, :])
        s = r[0:chunk]
        for g in range(1, GMLP_GROUPS):
            s = jnp.where(grp == g, r[g * chunk:(g + 1) * chunk], s)
        s = s + mixb_ref[...]
        oab_ref[c * chunk:(c + 1) * chunk, 256:512] = (u[c * chunk:(c + 1) * chunk, :] * s).astype(BF16)

    cos, sin_a, sin_b = cos_ref[...], sa_ref[...], sb_ref[...]
    qq_ref[...] = _norm_rope(proj[:, C_Q:C_NK], gq_ref[...], cos, sin_a, sin_b)
    nk = proj[:, C_NK:C_MK]
    nkr = _norm_rope(nk, gnk_ref[...], cos, sin_a, sin_b)
    is_k = (_iota(nk.shape, 1) // HEAD_DIM) % 2 == 0
    nkv = jnp.where(is_k, nkr, nk)
    nsa_ref[...] = nkv[:, 0:256]
    win_ref[...] = nkv[:, 256:384]
    moba_ref[:, 0:256] = _norm_rope(proj[:, C_MK:C_MK + 256], gmk_ref[...], cos, sin_a, sin_b)
    moba_ref[:, 256:512] = proj[:, C_MK + 256:C_G]
    gate_ref[...] = jax.nn.sigmoid(proj[:, C_G:C_END])


def _proj_call(x2d, lw, tables, halos, *, tm, tiles_per_seq, n_seq, chunk, seg):
    rows = x2d.shape[0]
    nt = rows // tm
    row = lambda w: pl.BlockSpec((tm, w), lambda i: (i, 0))
    full = lambda a: pl.BlockSpec(a.shape, lambda i: (0,) * a.ndim)
    n_tab = tables[0].shape[0] // tm
    tab = pl.BlockSpec((tm, LANES), lambda i: (i % n_tab, 0))
    ins = [x2d, lw['norm1'], lw['w1'], lw['conv_w'], lw['conv_b'], lw['ln_g'], lw['ln_b'], lw['mix'], lw['mixb'],
           lw['gq'], lw['gnk'], lw['gmk'], *tables]
    in_specs = [row(D_MODEL)] + [full(a) for a in ins[1:12]] + [tab] * 3
    out_shape = [jax.ShapeDtypeStruct((rows, 512), BF16), jax.ShapeDtypeStruct((rows, 512), F32),
                 jax.ShapeDtypeStruct((rows, 256), F32), jax.ShapeDtypeStruct((rows, 128), F32),
                 jax.ShapeDtypeStruct((rows, 512), F32), jax.ShapeDtypeStruct((rows, 128), F32),
                 jax.ShapeDtypeStruct((n_seq, SUBLANES, 256), F32)]
    out_specs = [row(512), row(512), row(256), row(128), row(512), row(128)]
    scratch = []
    if seg is None:
        out_specs.append(pl.BlockSpec((1, SUBLANES, 256), lambda i: (i // tiles_per_seq, 0, 0)))
        scratch.append(pltpu.VMEM((SUBLANES, 256), F32))
    else:
        ins += list(halos)
        in_specs += [row(256), row(256)]
        out_specs.append(pl.BlockSpec((n_seq, SUBLANES, 256), lambda i: (0, 0, 0)))
        out_shape.append(jax.ShapeDtypeStruct((rows, 256), F32))
        out_specs.append(row(256))
    return pl.pallas_call(
        functools.partial(_proj_kernel, tiles_per_seq=tiles_per_seq, chunk=chunk, seg=seg),
        grid=(nt,), in_specs=in_specs, out_specs=out_specs, out_shape=out_shape, scratch_shapes=scratch,
        compiler_params=pltpu.CompilerParams(dimension_semantics=("arbitrary",), vmem_limit_bytes=VMEM_LIMIT),
        name="proj_prompt" if seg is None else "proj_sample",
    )(*ins)


def _compress_compute(src_ref, wa_ref, wb_ref, pt_ref, pb_ref, w2_ref, out_ref, groups):
    acc_a = jnp.zeros((groups, 2 * NSA_CMP_HIDDEN), F32)
    acc_b = jnp.zeros((groups, 2 * NSA_CMP_HIDDEN), F32)
    for p in range(NSA_CMP_STRIDE // 2):
        xp = jnp.concatenate([src_ref[pl.ds(2 * p, groups, stride=NSA_CMP_STRIDE), :],
                              src_ref[pl.ds(2 * p + 1, groups, stride=NSA_CMP_STRIDE), :]], axis=1)
        acc_a = acc_a + _dot((xp + pt_ref[p:p + 1, :]).astype(BF16), wa_ref[p])
        acc_b = acc_b + _dot((xp + pb_ref[p:p + 1, :]).astype(BF16), wb_ref[p])
    hdn = jax.nn.gelu(acc_a + pltpu.roll(acc_b, groups - 1, 0))
    out_ref[0] = _dot(hdn.astype(BF16), w2_ref[...])


def _compress_prompt_kernel(src_ref, wa_ref, wb_ref, pt_ref, pb_ref, w2_ref, out_ref, *, groups):
    _compress_compute(src_ref.at[0], wa_ref, wb_ref, pt_ref, pb_ref, w2_ref, out_ref, groups)


def _compress_paged_kernel(ptab_ref, *refs, groups, pps):
    del ptab_ref
    pages = refs[:pps]
    wa_ref, wb_ref, pt_ref, pb_ref, w2_ref, out_ref, stage_ref = refs[pps:]
    j = pl.program_id(1)
    for k in range(pps):
        stage_ref[pl.ds(pl.multiple_of((j * pps + k) * PAGE_SIZE, PAGE_SIZE), PAGE_SIZE), :] = pages[k][0]

    @pl.when(j == pl.num_programs(1) - 1)
    def _():
        _compress_compute(stage_ref, wa_ref, wb_ref, pt_ref, pb_ref, w2_ref, out_ref, groups)


def _compress_prompt_call(nsa_rows3, lw):
    b, t, _ = nsa_rows3.shape
    groups = t // NSA_CMP_STRIDE
    full = lambda a: pl.BlockSpec(a.shape, lambda i: (0,) * a.ndim)
    ws = [lw['cmp_wa'], lw['cmp_wb'], lw['cmp_pt'], lw['cmp_pb'], lw['cmp_w2']]
    return pl.pallas_call(
        functools.partial(_compress_prompt_kernel, groups=groups),
        grid=(b,),
        in_specs=[pl.BlockSpec((1, t, LANES), lambda i: (i, 0, 0))] + [full(a) for a in ws],
        out_specs=pl.BlockSpec((1, groups, LANES), lambda i: (i, 0, 0)),
        out_shape=jax.ShapeDtypeStruct((b, groups, LANES), F32),
        compiler_params=pltpu.CompilerParams(dimension_semantics=("arbitrary",), vmem_limit_bytes=VMEM_LIMIT),
        name="compress_prompt",
    )(nsa_rows3, *ws)


def _pages_per_step(n_pages):
    pps = 16
    while n_pages % pps:
        pps //= 2
    return pps


def _compress_paged_call(cache3, page_table, lw):
    b, n_pages = page_table.shape
    past = n_pages * PAGE_SIZE
    groups = past // NSA_CMP_STRIDE
    pps = _pages_per_step(n_pages)
    full = lambda a: pl.BlockSpec(a.shape, lambda i, j, pt: (0,) * a.ndim)
    ws = [lw['cmp_wa'], lw['cmp_wb'], lw['cmp_pt'], lw['cmp_pb'], lw['cmp_w2']]
    page_specs = [pl.BlockSpec((1, PAGE_SIZE, LANES), functools.partial(
        lambda i, j, pt, k: (pt[i, j * pps + k], 0, 0), k=k)) for k in range(pps)]
    grid_spec = pltpu.PrefetchScalarGridSpec(
        num_scalar_prefetch=1, grid=(b, n_pages // pps),
        in_specs=page_specs + [full(a) for a in ws],
        out_specs=pl.BlockSpec((1, groups, LANES), lambda i, j, pt: (i, 0, 0)),
        scratch_shapes=[pltpu.VMEM((past, LANES), F32)])
    return pl.pallas_call(
        functools.partial(_compress_paged_kernel, groups=groups, pps=pps),
        grid_spec=grid_spec, out_shape=jax.ShapeDtypeStruct((b, groups, LANES), F32),
        compiler_params=pltpu.CompilerParams(dimension_semantics=("arbitrary", "arbitrary"),
                                             vmem_limit_bytes=VMEM_LIMIT),
        name="compress_sample",
    )(page_table, *([cache3] * pps), *ws)


def _topk_mask(score, n_cols, k):
    idx = _iota(score.shape, 1)
    rank = jnp.zeros(score.shape, jnp.int32)
    for j in range(n_cols):
        col = score[:, j:j + 1]
        beats = jnp.where(col > score, 1, jnp.where(col == score, jnp.where(idx > j, 1, 0), 0))
        rank = rank + beats
    return rank < k


def _softmax_rows(s, ok):
    s = jnp.where(ok, s, NEG)
    m = jnp.max(s, axis=-1, keepdims=True)
    p = jnp.where(ok, jnp.exp(s - m), 0.0)
    l = jnp.sum(p, axis=-1, keepdims=True)
    return p / jnp.where(l > 0.0, l, 1.0)


def _cmp_branch(qs, kcvc, qpos, n_cmp, n_q, n_blk_pad):
    g = kcvc.shape[0]
    kb = kcvc.astype(BF16)
    s = _dot_nt(qs, kb)
    n_idx = _iota(s.shape, 1)
    ok = (n_idx * NSA_CMP_STRIDE + (NSA_CMP_LEN - 1) <= qpos) & (n_idx < n_cmp)
    p = _softmax_rows(s, ok)
    o = _dot(p.astype(BF16), kb)
    psum = p[0:n_q]
    for h in range(1, NSA_HEADS):
        psum = psum + p[h * n_q:(h + 1) * n_q]
    cn = _iota((g, n_blk_pad), 0)
    cj = _iota((g, n_blk_pad), 1)
    cover = ((cn * NSA_CMP_STRIDE <= cj * NSA_SLC_BLOCK + (NSA_SLC_BLOCK - 1))
             & (cn * NSA_CMP_STRIDE + (NSA_CMP_LEN - 1) >= cj * NSA_SLC_BLOCK) & (cn < n_cmp)).astype(BF16)
    return o, _dot_hilo(psum, cover)


def _select_blocks(imp, qpos_q, n_cols):
    blk = _iota(imp.shape, 1)
    cur = qpos_q // NSA_SLC_BLOCK
    forced = (blk == 0) | (blk == cur) | (blk == cur - 1)
    imp = jnp.where(forced, BIG, imp)
    imp = jnp.where(blk > cur, NEG, imp)
    return jnp.where(_topk_mask(imp, n_cols, NSA_TOPN), 1.0, 0.0).astype(BF16)


def _slc_loop(qs, sel, kv_ref, qpos, qpos_q, n_tiles, tk, n_rep):
    rows = qs.shape[0]
    nb = sel.shape[1]

    def body(kt, carry):
        m, l, acc = carry
        k0 = pl.multiple_of(kt * tk, tk)
        kv = kv_ref[pl.ds(k0, tk), :]
        s = _dot_nt(qs, kv)
        expand = (_iota((nb, tk), 0) == kt * (tk // NSA_SLC_BLOCK) + _iota((nb, tk), 1) // NSA_SLC_BLOCK).astype(BF16)
        selx = _dot(sel, expand)
        kpos = k0 + _iota(selx.shape, 1)
        okq = (selx > 0.5) & (kpos <= qpos_q)
        ok = jnp.concatenate([okq] * n_rep, axis=0) if n_rep > 1 else okq
        s = jnp.where(ok, s, NEG)
        m_new = jnp.maximum(m, jnp.max(s, axis=-1, keepdims=True))
        alpha = jnp.exp(m - m_new)
        p = jnp.where(ok, jnp.exp(s - m_new), 0.0)
        l = alpha * l + jnp.sum(p, axis=-1, keepdims=True)
        acc = alpha * acc + _dot(p.astype(BF16), kv)
        return m_new, l, acc

    init = (jnp.full((rows, 1), NEG, F32), jnp.zeros((rows, 1), F32), jnp.zeros((rows, LANES), F32))
    _, l, acc = lax.fori_loop(0, n_tiles, body, init)
    del qpos
    return acc / l


def _win_branch(qs, wk, qpos, kpos0):
    s = _dot_nt(qs, wk)
    kpos = kpos0 + _iota(s.shape, 1)
    ok = (kpos <= qpos) & (kpos > qpos - NSA_WINDOW) & (kpos >= 0)
    return _dot(_softmax_rows(s, ok).astype(BF16), wk)


def _nsa_prompt_kernel(q_ref, g_ref, kcvc_ref, nsa_ref, win_ref, o_ref, kvs_ref, wins_ref, *, t, qb, tk):
    iq = pl.program_id(1)

    @pl.when(iq == 0)
    def _():
        kvs_ref[...] = nsa_ref[0].astype(BF16)
        wins_ref[...] = win_ref[0].astype(BF16)

    s0 = iq * qb
    q = q_ref[...] * (HEAD_DIM ** -0.5)
    lane = _iota((qb, LANES), 1)
    parts = []
    for h in range(NSA_HEADS):
        slab = q[:, (h // 2) * LANES:(h // 2 + 1) * LANES]
        if h % 2:
            slab = pltpu.roll(slab, HEAD_DIM, 1)
        parts.append(jnp.where(lane < HEAD_DIM, slab, 0.0))
    qs = jnp.concatenate(parts, axis=0).astype(BF16)
    qpos_q = s0 + _iota((qb, 1), 0)
    qpos = jnp.concatenate([qpos_q] * NSA_HEADS, axis=0)

    n_cmp = (t - NSA_CMP_LEN) // NSA_CMP_STRIDE + 1
    n_blk = t // NSA_SLC_BLOCK
    o_cmp, imp = _cmp_branch(qs, kcvc_ref[0], qpos, n_cmp, qb, n_blk)
    sel = _select_blocks(imp, qpos_q, n_blk)
    n_tiles = (s0 + qb + tk - 1) // tk
    o_slc = _slc_loop(qs, sel, kvs_ref, qpos, qpos_q, n_tiles, tk, NSA_HEADS)
    wlen = NSA_WINDOW + qb
    start = pl.multiple_of(jnp.maximum(s0 - NSA_WINDOW, 0), qb)
    o_win = _win_branch(qs, wins_ref[pl.ds(start, wlen), :], qpos, start)

    g = g_ref[...]
    slabs = []
    for h in range(NSA_HEADS):
        r = slice(h * qb, (h + 1) * qb)
        oh = (g[:, 3 * h:3 * h + 1] * o_cmp[r] + g[:, 3 * h + 1:3 * h + 2] * o_slc[r]
              + g[:, 3 * h + 2:3 * h + 3] * o_win[r])
        slabs.append(oh)
    outs = []
    for j in range(NSA_HEADS // 2):
        outs.append(jnp.where(lane < HEAD_DIM, pltpu.roll(slabs[2 * j], HEAD_DIM, 1), slabs[2 * j + 1]))
    o_ref[...] = jnp.concatenate(outs, axis=1)


def _nsa_prompt_call(qq, gates, kcvc, nsa_rows3, win_rows3, *, qb=128, tk=512):
    b, t, _ = nsa_rows3.shape
    nq = t // qb
    assert t % tk == 0 and t >= NSA_WINDOW + qb
    return pl.pallas_call(
        functools.partial(_nsa_prompt_kernel, t=t, qb=qb, tk=tk),
        grid=(b, nq),
        in_specs=[pl.BlockSpec((qb, 256), lambda i, j: (i * nq + j, 0)),
                  pl.BlockSpec((qb, LANES), lambda i, j: (i * nq + j, 0)),
                  pl.BlockSpec((1, kcvc.shape[1], LANES), lambda i, j: (i, 0, 0)),
                  pl.BlockSpec((1, t, LANES), lambda i, j: (i, 0, 1)),
                  pl.BlockSpec((1, t, LANES), lambda i, j: (i, 0, 0))],
        out_specs=pl.BlockSpec((qb, 256), lambda i, j: (i * nq + j, 0)),
        out_shape=jax.ShapeDtypeStruct((b * t, 256), F32),
        scratch_shapes=[pltpu.VMEM((t, LANES), BF16), pltpu.VMEM((t, LANES), BF16)],
        compiler_params=pltpu.CompilerParams(dimension_semantics=("arbitrary", "arbitrary"),
                                             vmem_limit_bytes=VMEM_LIMIT),
        name="nsa_prompt",
    )(qq, gates, kcvc, nsa_rows3, win_rows3)


def _nsa_sample_kernel(ptab_ref, *refs, past, tq, pps, tk):
    del ptab_ref
    pages = refs[:pps]
    q_ref, g_ref, kcvc_ref, new_ref, wst_ref, wnew_ref, o_ref, kvs_ref, wins_ref = refs[pps:]
    j = pl.program_id(1)
    for k in range(pps):
        kvs_ref[pl.ds(pl.multiple_of((j * pps + k) * PAGE_SIZE, PAGE_SIZE), PAGE_SIZE), :] = pages[k][0].astype(BF16)

    @pl.when(j == pl.num_programs(1) - 1)
    def _():
        pad = lambda a, n: jnp.concatenate([a, jnp.zeros((n - a.shape[0], LANES), F32)], axis=0).astype(BF16)
        kvs_ref[past:, :] = pad(new_ref[0], kvs_ref.shape[0] - past)
        wins_ref[0:NSA_WINDOW, :] = wst_ref[0].astype(BF16)
        wins_ref[NSA_WINDOW:, :] = pad(wnew_ref[0], wins_ref.shape[0] - NSA_WINDOW)

        rows = NSA_HEADS * tq
        qs = (q_ref[0] * (HEAD_DIM ** -0.5)).astype(BF16)
        qpos_q = past + _iota((tq, 1), 0)
        qpos = jnp.concatenate([qpos_q] * NSA_HEADS, axis=0)
        l_all = past + tq
        n_cmp = (l_all - NSA_CMP_LEN) // NSA_CMP_STRIDE + 1
        n_blk = -(-l_all // NSA_SLC_BLOCK)
        nb_pad = -(-n_blk // LANES) * LANES
        o_cmp, imp = _cmp_branch(qs, kcvc_ref[0], qpos, n_cmp, tq, nb_pad)
        sel = _select_blocks(imp, qpos_q, n_blk)
        o_slc = _slc_loop(qs, sel, kvs_ref, qpos, qpos_q, kvs_ref.shape[0] // tk, tk, NSA_HEADS)
        o_win = _win_branch(qs, wins_ref[...], qpos, past - NSA_WINDOW)
        g = g_ref[0]
        o_ref[0] = g[:, 0:1] * o_cmp + g[:, 1:2] * o_slc + g[:, 2:3] * o_win
        del rows


def _nsa_sample_call(cache3, page_table, q32, g32, kcvc, new3, wstate3, wnew3, *, tk=512):
    b, n_pages = page_table.shape
    past = n_pages * PAGE_SIZE
    tq = new3.shape[1]
    pps = _pages_per_step(n_pages)
    assert past % tk == 0 and tq <= tk and past >= NSA_WINDOW
    assert (past + tq - NSA_CMP_LEN) // NSA_CMP_STRIDE + 1 <= past // NSA_CMP_STRIDE - 1
    rows = NSA_HEADS * tq
    page_specs = [pl.BlockSpec((1, PAGE_SIZE, LANES), functools.partial(
        lambda i, j, pt, k: (pt[i, j * pps + k], 0, 1), k=k)) for k in range(pps)]
    per_b = lambda shp: pl.BlockSpec((1,) + shp, lambda i, j, pt: (i, 0, 0))
    grid_spec = pltpu.PrefetchScalarGridSpec(
        num_scalar_prefetch=1, grid=(b, n_pages // pps),
        in_specs=page_specs + [per_b((rows, LANES)), per_b((rows, LANES)), per_b((kcvc.shape[1], LANES)),
                               pl.BlockSpec((1, tq, LANES), lambda i, j, pt: (i, 0, 1)),
                               per_b((NSA_WINDOW, LANES)), per_b((tq, LANES))],
        out_specs=per_b((rows, LANES)),
        scratch_shapes=[pltpu.VMEM((past + tk, LANES), BF16), pltpu.VMEM((NSA_WINDOW + LANES, LANES), BF16)])
    return pl.pallas_call(
        functools.partial(_nsa_sample_kernel, past=past, tq=tq, pps=pps, tk=tk),
        grid_spec=grid_spec, out_shape=jax.ShapeDtypeStruct((b, rows, LANES), F32),
        compiler_params=pltpu.CompilerParams(dimension_semantics=("arbitrary", "arbitrary"),
                                             vmem_limit_bytes=VMEM_LIMIT),
        name="nsa_sample",
    )(page_table, *([cache3] * pps), q32, g32, kcvc, new3, wstate3, wnew3)


def _moba_head_loop(qh, sel, kv_ref, n_past, own_kv, own_ok):
    rows = qh.shape[0]
    nidx = _iota(sel.shape, 1)

    def step(carry, kb, vb, ok):
        m, l, acc = carry
        s = jnp.where(ok, _dot_nt(qh, kb), NEG)
        m_new = jnp.maximum(m, jnp.max(s, axis=-1, keepdims=True))
        alpha = jnp.exp(m - m_new)
        p = jnp.where(ok, jnp.exp(s - m_new), 0.0)
        return m_new, alpha * l + jnp.sum(p, axis=-1, keepdims=True), alpha * acc + _dot(p.astype(BF16), vb)

    def body(n, carry):
        r0 = pl.multiple_of(n * MOBA_BLOCK, MOBA_BLOCK)
        picked = jnp.sum(jnp.where(nidx == n, sel, 0.0), axis=-1, keepdims=True) > 0.5
        ok = jnp.broadcast_to(picked, (rows, MOBA_BLOCK))
        return step(carry, kv_ref[pl.ds(r0, MOBA_BLOCK), 0:256], kv_ref[pl.ds(r0, MOBA_BLOCK), 256:512], ok)

    init = (jnp.full((rows, 1), NEG, F32), jnp.zeros((rows, 1), F32), jnp.zeros((rows, 256), F32))
    carry = lax.fori_loop(0, n_past, body, init)
    _, l, acc = step(carry, own_kv[:, 0:256], own_kv[:, 256:512], own_ok)
    return acc / l


def _moba_select(q_f32, kmean, n_past, n_cols):
    gs = _dot_nt_3pass(q_f32, kmean)
    nidx = _iota(gs.shape, 1)
    gs = jnp.where(nidx < n_past, gs, NEG)
    return jnp.where(_topk_mask(gs, n_cols, MOBA_TOPK) & (nidx < n_past), 1.0, 0.0)


def _moba_prompt_kernel(q_ref, kv_ref, o_ref, kvb_ref, kmean_ref, *, nb):
    i = pl.program_id(1)
    r0 = pl.multiple_of(i * MOBA_BLOCK, MOBA_BLOCK)
    blk = kv_ref[0, pl.ds(r0, MOBA_BLOCK), :]
    kvb_ref[pl.ds(r0, MOBA_BLOCK), :] = blk.astype(BF16)

    @pl.when(i == 0)
    def _():
        kmean_ref[...] = jnp.zeros_like(kmean_ref)

    kmean_ref[pl.ds(i, 1), :] = jnp.sum(blk[:, 0:256], axis=0, keepdims=True) * (1.0 / MOBA_BLOCK)

    q = q_ref[...]
    grp = _iota(q.shape, 1) // HEAD_DIM
    own_kv = kvb_ref[pl.ds(r0, MOBA_BLOCK), :]
    causal = _iota((MOBA_BLOCK, MOBA_BLOCK), 1) <= _iota((MOBA_BLOCK, MOBA_BLOCK), 0)
    kmean = kmean_ref[...]
    out = jnp.zeros(q.shape, F32)
    for h in range(MOBA_HEADS):
        qh = jnp.where(grp == h, q, 0.0)
        sel = _moba_select(qh, kmean, i, nb)
        oh = _moba_head_loop((qh * (HEAD_DIM ** -0.5)).astype(BF16), sel, kvb_ref, i, own_kv, causal)
        out = jnp.where(grp == h, oh, out)
    o_ref[...] = out


def _moba_prompt_call(qq, moba_rows3):
    b, t, _ = moba_rows3.shape
    nb = t // MOBA_BLOCK
    return pl.pallas_call(
        functools.partial(_moba_prompt_kernel, nb=nb),
        grid=(b, nb),
        in_specs=[pl.BlockSpec((MOBA_BLOCK, 256), lambda i, j: (i * nb + j, 1)),
                  pl.BlockSpec((1, t, 512), lambda i, j: (i, 0, 0))],
        out_specs=pl.BlockSpec((MOBA_BLOCK, 256), lambda i, j: (i * nb + j, 0)),
        out_shape=jax.ShapeDtypeStruct((b * t, 256), F32),
        scratch_shapes=[pltpu.VMEM((t, 512), BF16), pltpu.VMEM((-(-nb // SUBLANES) * SUBLANES, 256), F32)],
        compiler_params=pltpu.CompilerParams(dimension_semantics=("arbitrary", "arbitrary"),
                                             vmem_limit_bytes=VMEM_LIMIT),
        name="moba_prompt",
    )(qq, moba_rows3)


def _moba_sample_kernel(ptab_ref, *refs, past, tq, pps):
    del ptab_ref
    pages = refs[:pps]
    q_ref, new_ref, o_ref, kvb_ref, kmean_ref = refs[pps:]
    j = pl.program_id(1)
    ppb = MOBA_BLOCK // PAGE_SIZE
    for k in range(0, pps, ppb):
        ksum = jnp.zeros((1, 256), F32)
        for kk in range(ppb):
            pg = pages[k + kk][0]
            r0 = pl.multiple_of((j * pps + k + kk) * PAGE_SIZE, PAGE_SIZE)
            kvb_ref[pl.ds(r0, PAGE_SIZE), :] = pg.astype(BF16)
            ksum = ksum + jnp.sum(pg[:, 0:256], axis=0, keepdims=True)
        kmean_ref[pl.ds(j * (pps // ppb) + k // ppb, 1), :] = ksum * (1.0 / MOBA_BLOCK)

    @pl.when(j == pl.num_programs(1) - 1)
    def _():
        n_past = past // MOBA_BLOCK
        q = q_ref[0]
        rows = q.shape[0]
        sel = _moba_select(q, kmean_ref[...], n_past, n_past)
        own = jnp.concatenate([new_ref[0], jnp.zeros((LANES - tq, 512), F32)], axis=0).astype(BF16)
        ri = _iota((rows, LANES), 0) % tq
        own_ok = _iota((rows, LANES), 1) <= ri
        oh = _moba_head_loop((q * (HEAD_DIM ** -0.5)).astype(BF16), sel, kvb_ref, n_past, own, own_ok)
        grp = _iota((tq, 256), 1) // HEAD_DIM
        out = oh[0:tq]
        for h in range(1, MOBA_HEADS):
            out = jnp.where(grp == h, oh[h * tq:(h + 1) * tq], out)
        o_ref[0] = out


def _moba_sample_call(cache3, page_table, qbd, new3):
    b, n_pages = page_table.shape
    past = n_pages * PAGE_SIZE
    tq = new3.shape[1]
    pps = _pages_per_step(n_pages)
    assert past % MOBA_BLOCK == 0 and pps % (MOBA_BLOCK // PAGE_SIZE) == 0 and tq <= LANES
    rows = MOBA_HEADS * tq
    nb = past // MOBA_BLOCK
    page_specs = [pl.BlockSpec((1, PAGE_SIZE, 512), functools.partial(
        lambda i, j, pt, k: (pt[i, j * pps + k], 0, 0), k=k)) for k in range(pps)]
    grid_spec = pltpu.PrefetchScalarGridSpec(
        num_scalar_prefetch=1, grid=(b, n_pages // pps),
        in_specs=page_specs + [pl.BlockSpec((1, rows, 256), lambda i, j, pt: (i, 0, 0)),
                               pl.BlockSpec((1, tq, 512), lambda i, j, pt: (i, 0, 0))],
        out_specs=pl.BlockSpec((1, tq, 256), lambda i, j, pt: (i, 0, 0)),
        scratch_shapes=[pltpu.VMEM((past, 512), BF16), pltpu.VMEM((-(-nb // SUBLANES) * SUBLANES, 256), F32)])
    return pl.pallas_call(
        functools.partial(_moba_sample_kernel, past=past, tq=tq, pps=pps),
        grid_spec=grid_spec, out_shape=jax.ShapeDtypeStruct((b, tq, 256), F32),
        compiler_params=pltpu.CompilerParams(dimension_semantics=("arbitrary", "arbitrary"),
                                             vmem_limit_bytes=VMEM_LIMIT),
        name="moba_sample",
    )(page_table, *([cache3] * pps), qbd, new3)


def _merge_kernel(x_ref, oab_ref, oc_ref, od_ref, n1_ref, wg_ref, wbr_ref, wo_ref, y_ref):
    x = x_ref[...]
    xn = (x * lax.rsqrt(jnp.mean(x * x, axis=-1, keepdims=True) + EPS) * n1_ref[...]).astype(BF16)
    branches = [oab_ref[:, 0:256], oab_ref[:, 256:512], oc_ref[...].astype(BF16), od_ref[...].astype(BF16)]
    merged = jnp.zeros(x.shape, F32)
    for n in range(N_BRANCH):
        g = jax.nn.sigmoid(_dot(xn, wg_ref[:, n * D_MODEL:(n + 1) * D_MODEL]))
        merged = merged + g * _dot(branches[n], wbr_ref[n])
    y_ref[...] = x + _dot(merged.astype(BF16), wo_ref[...])


def _merge_call(x2d, oab, oc, od, lw, *, tm):
    rows = x2d.shape[0]
    row = lambda w: pl.BlockSpec((tm, w), lambda i: (i, 0))
    full = lambda a: pl.BlockSpec(a.shape, lambda i: (0,) * a.ndim)
    ws = [lw['norm1'], lw['wg'], lw['w_branch'], lw['w_o']]
    return pl.pallas_call(
        _merge_kernel, grid=(rows // tm,),
        in_specs=[row(D_MODEL), row(512), row(256), row(256)] + [full(a) for a in ws],
        out_specs=row(D_MODEL), out_shape=jax.ShapeDtypeStruct((rows, D_MODEL), F32),
        compiler_params=pltpu.CompilerParams(dimension_semantics=("arbitrary",), vmem_limit_bytes=VMEM_LIMIT),
        name="merge",
    )(x2d, oab, oc, od, *ws)


def _ffn_kernel(*refs, tiles_per_seq, seg):
    it = iter(refs)
    x_ref, n2_ref, wgate_ref, wup_ref, cw_ref, cb_ref, wdown_ref = (next(it) for _ in range(7))
    if seg is not None:
        h1_ref, h2_ref = next(it), next(it)
    y_ref, tail_ref = next(it), next(it)
    if seg is None:
        prev_ref = next(it)

    x = x_ref[...]
    tm = x.shape[0]
    h2 = (x * lax.rsqrt(jnp.mean(x * x, axis=-1, keepdims=True) + EPS) * n2_ref[...]).astype(BF16)
    gate = _dot(h2, wgate_ref[...])
    if seg is None:
        @pl.when(pl.program_id(0) % tiles_per_seq == 0)
        def _():
            prev_ref[...] = jnp.zeros_like(prev_ref)

        g1, g2 = _shift_rows(gate, prev_ref[SUBLANES - 1:SUBLANES, :], prev_ref[SUBLANES - 2:SUBLANES - 1, :], None)
        prev_ref[...] = gate[tm - SUBLANES:, :]
        tail_ref[0] = gate[tm - SUBLANES:, :]
    else:
        g1, g2 = _shift_rows(gate, h1_ref[...], h2_ref[...], seg)
        tail_ref[...] = gate.reshape(tail_ref.shape)
    ac = g2 * cw_ref[0:1, :] + cb_ref[...] + g1 * cw_ref[1:2, :] + gate * cw_ref[2:3, :]
    act = jax.nn.silu(ac) * _dot(h2, wup_ref[...])
    y_ref[...] = x + _dot(act.astype(BF16), wdown_ref[...])


def _ffn_call(x2d, lw, halos, *, tm, tiles_per_seq, n_seq, seg):
    rows = x2d.shape[0]
    row = lambda w: pl.BlockSpec((tm, w), lambda i: (i, 0))
    full = lambda a: pl.BlockSpec(a.shape, lambda i: (0,) * a.ndim)
    ws = [lw['norm2'], lw['w_gate'], lw['w_up'], lw['ffn_conv_w'], lw['ffn_conv_b'], lw['w_down']]
    ins = [x2d] + ws
    in_specs = [row(D_MODEL)] + [full(a) for a in ws]
    scratch = []
    if seg is None:
        tail_spec = pl.BlockSpec((1, SUBLANES, D_FF), lambda i: (i // tiles_per_seq, 0, 0))
        scratch.append(pltpu.VMEM((SUBLANES, D_FF), F32))
    else:
        ins += list(halos)
        in_specs += [row(D_FF), row(D_FF)]
        tail_spec = pl.BlockSpec((n_seq, SUBLANES, D_FF), lambda i: (0, 0, 0))
    return pl.pallas_call(
        functools.partial(_ffn_kernel, tiles_per_seq=tiles_per_seq, seg=seg),
        grid=(rows // tm,), in_specs=in_specs, out_specs=[row(D_MODEL), tail_spec],
        out_shape=[jax.ShapeDtypeStruct((rows, D_MODEL), F32), jax.ShapeDtypeStruct((n_seq, SUBLANES, D_FF), F32)],
        scratch_shapes=scratch,
        compiler_params=pltpu.CompilerParams(dimension_semantics=("arbitrary",), vmem_limit_bytes=VMEM_LIMIT),
        name="ffn_prompt" if seg is None else "ffn_sample",
    )(*ins)


def _rope_tables(pos):
    half = ROT_DIM // 2
    inv = jnp.power(jnp.float32(ROPE_THETA), -jnp.arange(half, dtype=F32) / half)
    ang = pos.astype(F32)[:, None] * inv[None, :]
    cos, sin = jnp.cos(ang), jnp.sin(ang)
    n = pos.shape[0]
    ones = jnp.ones((n, HEAD_DIM - ROT_DIM), F32)
    zeros = jnp.zeros((n, HEAD_DIM - ROT_DIM), F32)
    z8 = jnp.zeros((n, half), F32)
    c64 = jnp.concatenate([cos, cos, ones], axis=1)
    a64 = jnp.concatenate([-sin, z8, zeros], axis=1)
    b64 = jnp.concatenate([z8, sin, zeros], axis=1)
    return tuple(jnp.tile(a, (1, LANES // HEAD_DIM)) for a in (c64, a64, b64))


def _layer_weights(l, p, sample_tq, dec_batch):
    w_in = p['w_in'][l]
    o_g = 3 * 256 + 512 + 256 + 192 + 192
    o_dq = o_g + 3 * NSA_HEADS
    o_br = o_dq + 3 * 256
    ck = w_in[:, 1536:1728].reshape(D_MODEL, 3, HEAD_DIM)
    cv = w_in[:, 1728:1920].reshape(D_MODEL, 3, HEAD_DIM)
    nk = jnp.stack([ck[:, 0], cv[:, 0], ck[:, 1], cv[:, 1], ck[:, 2], cv[:, 2]], axis=1).reshape(D_MODEL, 384)
    w1 = jnp.concatenate([
        w_in[:, 0:768], w_in[:, 768:1280], w_in[:, 1280:1536], w_in[:, o_dq:o_dq + 256], nk,
        w_in[:, o_dq + 256:o_dq + 768], w_in[:, o_g:o_g + 12], jnp.zeros((D_MODEL, LANES - 12), F32)], axis=1)
    lw = {'w1': w1.astype(BF16), 'wg': w_in[:, o_br:].astype(BF16)}
    row = lambda a: a.reshape(1, -1)
    lw['norm1'] = row(p['norm1'][l])
    lw['norm2'] = row(p['norm2'][l])
    lw['conv_w'] = p['conv_w'][l]
    lw['conv_b'] = row(p['conv_b'][l])
    lw['ln_g'] = row(p['gmlp_ln_g'][l])
    lw['ln_b'] = row(p['gmlp_ln_b'][l])
    tril = jnp.tril(jnp.ones((GMLP_CHUNK, GMLP_CHUNK), bool))
    ws = jnp.where(tril[None], p['gmlp_ws'][l], 0.0)
    bs = p['gmlp_bs'][l]
    lw['mix_p'] = ws.reshape(GMLP_GROUPS * GMLP_CHUNK, GMLP_CHUNK).astype(BF16)
    lw['mixb_p'] = jnp.repeat(bs.T, HEAD_DIM, axis=1)
    eye = jnp.eye(dec_batch, dtype=F32)
    lw['mix_s'] = jnp.concatenate([jnp.kron(eye, ws[g, :sample_tq, :sample_tq]) for g in range(GMLP_GROUPS)],
                                  axis=0).astype(BF16)
    lw['mixb_s'] = jnp.tile(jnp.repeat(bs.T[:sample_tq], HEAD_DIM, axis=1), (dec_batch, 1))
    qg, mqg = p['nsa_q_gain'][l], p['moba_q_gain'][l]
    lw['gq'] = row(jnp.concatenate([jnp.tile(qg, NSA_HEADS), jnp.tile(mqg, MOBA_HEADS)]))
    kg = p['nsa_k_gain'][l]
    one = jnp.ones((HEAD_DIM,), F32)
    lw['gnk'] = row(jnp.concatenate([kg[0], one, kg[1], one, kg[2], one]))
    lw['gmk'] = row(jnp.tile(p['moba_k_gain'][l], MOBA_HEADS))
    w1c = p['nsa_cmp_w1'][l].reshape(2, NSA_CMP_LEN, HEAD_DIM, NSA_CMP_HIDDEN)
    pos = p['nsa_cmp_pos'][l]
    zero = jnp.zeros((HEAD_DIM, NSA_CMP_HIDDEN), F32)

    def pair_w(r_off):
        mats = []
        for pp in range(NSA_CMP_STRIDE // 2):
            blocks = []
            for r in (2 * pp, 2 * pp + 1):
                blocks.append(jnp.concatenate([w1c[0, r + r_off], zero], axis=1))
                blocks.append(jnp.concatenate([zero, w1c[1, r + r_off]], axis=1))
            mats.append(jnp.concatenate(blocks, axis=0))
        return jnp.stack(mats).astype(BF16)

    def pair_pos(r_off):
        return jnp.stack([jnp.concatenate([pos[0, 2 * pp + r_off], pos[1, 2 * pp + r_off],
                                           pos[0, 2 * pp + 1 + r_off], pos[1, 2 * pp + 1 + r_off]])
                          for pp in range(NSA_CMP_STRIDE // 2)])

    lw['cmp_wa'], lw['cmp_wb'] = pair_w(0), pair_w(NSA_CMP_STRIDE)
    lw['cmp_pt'], lw['cmp_pb'] = pair_pos(0), pair_pos(NSA_CMP_STRIDE)
    w2 = p['nsa_cmp_w2'][l]
    z2 = jnp.zeros((NSA_CMP_HIDDEN, HEAD_DIM), F32)
    lw['cmp_w2'] = jnp.concatenate([jnp.concatenate([w2[0], z2], axis=1),
                                    jnp.concatenate([z2, w2[1]], axis=1)], axis=0).astype(BF16)
    lw['w_branch'] = p['w_branch'][l].astype(BF16)
    lw['w_o'] = p['w_o'][l].astype(BF16)
    lw['w_gate'] = p['w_ffn_gate'][l].astype(BF16)
    lw['w_up'] = p['w_ffn_up'][l].astype(BF16)
    lw['w_down'] = p['w_ffn_down'][l].astype(BF16)
    lw['ffn_conv_w'] = p['ffn_conv_w'][l]
    lw['ffn_conv_b'] = row(p['ffn_conv_b'][l])
    return lw


def _halos(buf, tq):
    b, _, c = buf.shape
    z = jnp.zeros((b, tq - 1, c), F32)
    h1 = jnp.concatenate([buf[:, 1:2], z], axis=1)
    h2 = jnp.concatenate([buf[:, 0:2], z[:, 1:]], axis=1)
    return h1.reshape(b * tq, c), h2.reshape(b * tq, c)


def _row_tile(rows, cap):
    tm = cap
    while rows % tm:
        tm //= 2
    return tm


def _prompt_layer(x2d, lw, tables, b, t):
    tm = _row_tile(t, 512)
    tps = t // tm
    pw = dict(lw, mix=lw['mix_p'], mixb=lw['mixb_p'])
    oab, qq, nsa_rows, win_rows, moba_rows, gates, ztail = _proj_call(
        x2d, pw, tables, None, tm=tm, tiles_per_seq=tps, n_seq=b, chunk=GMLP_CHUNK, seg=None)
    nsa3 = nsa_rows.reshape(b, t, 256)
    win3 = win_rows.reshape(b, t, 128)
    moba3 = moba_rows.reshape(b, t, 512)
    kcvc = _compress_prompt_call(nsa3, lw)
    o_c = _nsa_prompt_call(qq, gates, kcvc, nsa3, win3)
    o_d = _moba_prompt_call(qq, moba3)
    x_mid = _merge_call(x2d, oab, o_c, o_d, lw, tm=tm)
    y, ftail = _ffn_call(x_mid, lw, None, tm=tm, tiles_per_seq=tps, n_seq=b, seg=None)
    state = (nsa3.reshape(b, t, 4, HEAD_DIM), moba3.reshape(b, t, 2, MOBA_HEADS, HEAD_DIM),
             win3[:, t - min(NSA_WINDOW, t):].reshape(b, -1, 2, HEAD_DIM),
             ztail[:, SUBLANES - 2:], ftail[:, SUBLANES - 2:])
    return y, state


def _sample_layer(x2d, lw, tables, b, tq, nsa_cache3, moba_cache3, page_table, win_state, conv_state, ffn_state):
    rows = b * tq
    pw = dict(lw, mix=lw['mix_s'], mixb=lw['mixb_s'])
    oab, qq, nsa_rows, win_rows, moba_rows, gates, ztail, vn = _proj_call(
        x2d, pw, tables, _halos(conv_state, tq), tm=rows, tiles_per_seq=1, n_seq=b, chunk=rows, seg=tq)
    nsa3 = nsa_rows.reshape(b, tq, 256)
    win3 = win_rows.reshape(b, tq, 128)
    moba3 = moba_rows.reshape(b, tq, 512)
    kcvc = _compress_paged_call(nsa_cache3, page_table, lw)
    q4 = qq[:, 0:256].reshape(b, tq, NSA_HEADS, HEAD_DIM).transpose(0, 2, 1, 3).reshape(b, NSA_HEADS * tq, HEAD_DIM)
    q32 = jnp.concatenate([q4, jnp.zeros_like(q4)], axis=-1)
    g4 = gates[:, 0:3 * NSA_HEADS].reshape(b, tq, NSA_HEADS, 3).transpose(0, 2, 1, 3).reshape(b, NSA_HEADS * tq, 3)
    g32 = jnp.concatenate([g4, jnp.zeros((b, NSA_HEADS * tq, LANES - 3), F32)], axis=-1)
    wst3 = win_state.reshape(b, NSA_WINDOW, 128)
    o32 = _nsa_sample_call(nsa_cache3, page_table, q32, g32, kcvc, nsa3, wst3, win3)
    o_c = o32[:, :, HEAD_DIM:].reshape(b, NSA_HEADS, tq, HEAD_DIM).transpose(0, 2, 1, 3).reshape(rows, 256)
    qm = qq[:, 256:512].reshape(b, tq, MOBA_HEADS, HEAD_DIM).transpose(0, 2, 1, 3)
    eye = jnp.eye(MOBA_HEADS, dtype=F32)
    qbd = (qm[:, :, :, None, :] * eye[None, :, None, :, None]).reshape(b, MOBA_HEADS * tq, 256)
    o_d = _moba_sample_call(moba_cache3, page_table, qbd, moba3).reshape(rows, 256)
    x_mid = _merge_call(x2d, oab, o_c, o_d, lw, tm=rows)
    y, ftail = _ffn_call(x_mid, lw, _halos(ffn_state, tq), tm=rows, tiles_per_seq=1, n_seq=b, seg=tq)
    win_all = jnp.concatenate([wst3, win3], axis=1)
    state = (nsa3.reshape(b, tq, 4, HEAD_DIM), moba3.reshape(b, tq, 2, MOBA_HEADS, HEAD_DIM),
             win_all[:, win_all.shape[1] - NSA_WINDOW:].reshape(b, NSA_WINDOW, 2, HEAD_DIM),
             ztail[:, tq - 2:tq], ftail[:, tq - 2:tq], vn.reshape(b, tq, 256))
    return y, state


def kernel(x_prompt, x_sample, cache_nsa_kv, cache_moba_kv, state_nsa_win, state_conv, state_ffn_conv, page_table, norm1, w_in, conv_w, conv_b, gmlp_ln_g, gmlp_ln_b, gmlp_ws, gmlp_bs, nsa_q_gain, nsa_k_gain, nsa_cmp_pos, nsa_cmp_w1, nsa_cmp_w2, moba_q_gain, moba_k_gain, w_branch, w_o, norm2, w_ffn_gate, w_ffn_up, ffn_conv_w, ffn_conv_b, w_ffn_down):
    params = dict(norm1=norm1, w_in=w_in, conv_w=conv_w, conv_b=conv_b, gmlp_ln_g=gmlp_ln_g, gmlp_ln_b=gmlp_ln_b,
                  gmlp_ws=gmlp_ws, gmlp_bs=gmlp_bs, nsa_q_gain=nsa_q_gain, nsa_k_gain=nsa_k_gain,
                  nsa_cmp_pos=nsa_cmp_pos, nsa_cmp_w1=nsa_cmp_w1, nsa_cmp_w2=nsa_cmp_w2, moba_q_gain=moba_q_gain,
                  moba_k_gain=moba_k_gain, w_branch=w_branch, w_o=w_o, norm2=norm2, w_ffn_gate=w_ffn_gate,
                  w_ffn_up=w_ffn_up, ffn_conv_w=ffn_conv_w, ffn_conv_b=ffn_conv_b, w_ffn_down=w_ffn_down)
    bp, tp, _ = x_prompt.shape
    bs, ts, _ = x_sample.shape
    depth = w_in.shape[0]
    n_pool = cache_nsa_kv.shape[1]
    past = page_table.shape[1] * PAGE_SIZE
    assert ts == SUBLANES and state_nsa_win.shape[2] == NSA_WINDOW
    tab_p = _rope_tables(jnp.arange(tp, dtype=jnp.int32))
    tab_s = tuple(jnp.tile(a, (bs, 1)) for a in _rope_tables(past + jnp.arange(ts, dtype=jnp.int32)))
    yp = x_prompt.reshape(bp * tp, D_MODEL)
    ys = x_sample.reshape(bs * ts, D_MODEL)
    sp, ss = [], []
    for l in range(depth):
        lw = _layer_weights(l, params, ts, bs)
        yp, st_p = _prompt_layer(yp, lw, tab_p, bp, tp)
        ys, st_s = _sample_layer(ys, lw, tab_s, bs, ts, cache_nsa_kv[l].reshape(n_pool, PAGE_SIZE, 256),
                                 cache_moba_kv[l].reshape(n_pool, PAGE_SIZE, 512), page_table,
                                 state_nsa_win[l], state_conv[l], state_ffn_conv[l])
        sp.append(st_p)
        ss.append(st_s)
    stack = lambda lst, k: jnp.stack([s[k] for s in lst])
    return (yp.reshape(bp, tp, D_MODEL), ys.reshape(bs, ts, D_MODEL),
            stack(sp, 0), stack(ss, 0), stack(sp, 1), stack(ss, 1), stack(sp, 2), stack(ss, 2),
            stack(sp, 3), stack(ss, 3), stack(sp, 4), stack(ss, 4), stack(ss, 5))
```

```python
import functools

import jax
import jax.numpy as jnp
from jax import lax
from jax.experimental import pallas as pl
from jax.experimental.pallas import tpu as pltpu

F32 = jnp.float32
BF16 = jnp.bfloat16

D_MODEL = 1024
HEAD_DIM = 64
ROT_DIM = HEAD_DIM // 4
ROPE_THETA = 500000.0
PAGE_SIZE = 128
BRANCH_CH = 256
N_BRANCH = 4
GMLP_GROUPS = 4
GMLP_CHUNK = 128
NSA_HEADS = 4
NSA_CMP_LEN = 32
NSA_CMP_STRIDE = 16
NSA_CMP_HIDDEN = 256
NSA_SLC_BLOCK = 64
NSA_TOPN = 16
NSA_WINDOW = 512
MOBA_HEADS = 4
MOBA_BLOCK = 256
MOBA_TOPK = 3
D_FF = 2816
EPS = 1e-6
NEG = -1e30
BIG = 1e30

LANES = 128
SUBLANES = 8
VMEM_LIMIT = 56 * 1024 * 1024

C_AH, C_AB, C_AC, C_U, C_V, C_G, C_END = 0, 256, 512, 768, 1024, 1280, 1408
R_Q, R_NSA, R_WIN, R_MK, R_MV, R_G, R_END = 0, 512, 768, 896, 1152, 1408, 1424
NORMED_GROUPS = (0, 1, 2, 3, 4, 5, 6, 7, 8, 10, 12, 14, 15, 16, 17)

_NT = (((1,), (1,)), ((), ()))


def _dot(a, b):
    return jnp.dot(a, b, preferred_element_type=F32)


def _dot_nt(a, b):
    return lax.dot_general(a, b, _NT, preferred_element_type=F32)


def _split(a):
    hi = a.astype(BF16)
    lo = (a - hi.astype(F32)).astype(BF16)
    return hi, lo


def _dot_3pass(a, b):
    ah, al = _split(a)
    bh, bl = _split(b)
    return _dot(ah, bh) + _dot(ah, bl) + _dot(al, bh)


def _iota(shape, dim):
    return lax.broadcasted_iota(jnp.int32, shape, dim)


def _lane_tile(a, n):
    return a if n == 1 else jnp.concatenate([a] * n, axis=1)


def _norm_rope_t(xh, gain, cos, sin):
    ss = jnp.sum(xh * xh, axis=0, keepdims=True)
    y = xh * lax.rsqrt(ss * (1.0 / HEAD_DIM) + EPS) * gain
    half = ROT_DIM // 2
    y0, y1 = y[0:half], y[half:ROT_DIM]
    return jnp.concatenate([y0 * cos - y1 * sin, y1 * cos + y0 * sin, y[ROT_DIM:]], axis=0)


def _shift_rows(z, prev1, prev2, seg):
    rows = _iota(z.shape, 0)
    z1 = pltpu.roll(z, 1, 0)
    z2 = pltpu.roll(z, 2, 0)
    if seg is None:
        z1 = jnp.where(rows == 0, prev1, z1)
        z2 = jnp.where(rows == 0, prev2, jnp.where(rows == 1, prev1, z2))
    else:
        t = rows % seg
        z1 = jnp.where(t == 0, prev1, z1)
        z2 = jnp.where(t <= 1, prev2, z2)
    return z1, z2


def _proj_kernel(*refs, tiles_per_seq, chunk, seg):
    it = iter(refs)
    x_ref, n1_ref, wtm_ref, wtr_ref, cw_ref, cb_ref, lng_ref, lnb_ref, mix_ref, mixb_ref = (next(it) for _ in range(10))
    gain_ref, cos_ref, sin_ref = (next(it) for _ in range(3))
    if seg is not None:
        h1_ref, h2_ref = next(it), next(it)
    oab_ref, qt_ref, nsat_ref, wint_ref, mobat_ref, gt_ref, gtm_ref, ztail_ref = (next(it) for _ in range(8))
    if seg is not None:
        vn_ref = next(it)
    else:
        cmp_ref, slc_ref, win_ref, mk_ref, vt_ref, kmean_ref, zprev_ref = (next(it) for _ in range(7))

    x = x_ref[...]
    tm = x.shape[0]
    xn = (x * lax.rsqrt(jnp.mean(x * x, axis=-1, keepdims=True) + EPS) * n1_ref[...]).astype(BF16)
    proj = _dot_nt(xn, wtm_ref[...])
    projt = _dot_nt(wtr_ref[...], xn)

    z = proj[:, C_AC:C_AC + 256] * proj[:, C_AH:C_AH + 256]
    if seg is None:
        @pl.when(pl.program_id(0) % tiles_per_seq == 0)
        def _():
            zprev_ref[...] = jnp.zeros_like(zprev_ref)

        z1, z2 = _shift_rows(z, zprev_ref[SUBLANES - 1:SUBLANES, :], zprev_ref[SUBLANES - 2:SUBLANES - 1, :], None)
        zprev_ref[...] = z[tm - SUBLANES:, :]
        ztail_ref[0] = z[tm - SUBLANES:, :]
    else:
        z1, z2 = _shift_rows(z, h1_ref[...], h2_ref[...], seg)
        ztail_ref[...] = z.reshape(ztail_ref.shape)
    ya = z2 * cw_ref[0:1, :] + cb_ref[...] + z1 * cw_ref[1:2, :] + z * cw_ref[2:3, :]
    oab_ref[:, 0:256] = (proj[:, C_AB:C_AB + 256] * ya).astype(BF16)

    u = jax.nn.gelu(proj[:, C_U:C_U + 256])
    v = jax.nn.gelu(proj[:, C_V:C_V + 256])
    vc = v - jnp.mean(v, axis=-1, keepdims=True)
    vn = vc * lax.rsqrt(jnp.mean(vc * vc, axis=-1, keepdims=True) + EPS) * lng_ref[...] + lnb_ref[...]
    if seg is not None:
        vn_ref[...] = vn
    vnb = vn.astype(BF16)
    grp = _iota((chunk, 256), 1) // HEAD_DIM
    for c in range(tm // chunk):
        r = _dot(mix_ref[...], vnb[c * chunk:(c + 1) * chunk, :])
        s = r[0:chunk]
        for g in range(1, GMLP_GROUPS):
            s = jnp.where(grp == g, r[g * chunk:(g + 1) * chunk], s)
        s = s + mixb_ref[...]
        oab_ref[c * chunk:(c + 1) * chunk, 256:512] = (u[c * chunk:(c + 1) * chunk, :] * s).astype(BF16)

    gtm_ref[...] = jax.nn.sigmoid(proj[:, C_G:C_END])
    gt_ref[...] = jax.nn.sigmoid(projt[R_G:R_END, :])

    cos, sin = cos_ref[...], sin_ref[...]
    rep = tm // LANES
    groups = []
    for g in range(R_MV // HEAD_DIM):
        xh = projt[g * HEAD_DIM:(g + 1) * HEAD_DIM, :]
        if g in NORMED_GROUPS:
            xh = _norm_rope_t(xh, _lane_tile(gain_ref[g * HEAD_DIM:(g + 1) * HEAD_DIM, :], rep), cos, sin)
        groups.append(xh)
    qkv = jnp.concatenate(groups, axis=0)
    qt_ref[...] = qkv[R_Q:R_NSA]
    nsat_ref[0] = qkv[R_NSA:R_WIN]
    wint_ref[0] = qkv[R_WIN:R_MK]
    mobat_ref[0, 0:256, :] = qkv[R_MK:R_MV]
    vmt = projt[R_MV:R_G, :]
    mobat_ref[0, 256:512, :] = vmt

    if seg is None:
        nsa_tm = qkv[R_NSA:R_WIN].T
        cmp_ref[...] = nsa_tm[:, 0:LANES]
        slc_ref[...] = nsa_tm[:, LANES:2 * LANES].astype(BF16)
        win_ref[...] = qkv[R_WIN:R_MK].T.astype(BF16)
        mk_tm = qkv[R_MK:R_MV].T
        mk_ref[...] = mk_tm.astype(BF16)
        vt_ref[0, 0:256, :] = vmt.astype(BF16)
        vt_ref[0, 256:320, :] = qkv[R_NSA + 192:R_NSA + 256].astype(BF16)
        vt_ref[0, 320:384, :] = qkv[R_WIN + 64:R_WIN + 128].astype(BF16)
        kmean_ref[...] = jnp.zeros_like(kmean_ref)
        for r in range(tm // MOBA_BLOCK):
            kmean_ref[0, r:r + 1, :] = jnp.sum(mk_tm[r * MOBA_BLOCK:(r + 1) * MOBA_BLOCK, :], axis=0,
                                               keepdims=True) * (1.0 / MOBA_BLOCK)


def _proj_call(x2d, lw, tables, halos, *, tm, tiles_per_seq, n_seq, chunk, seg):
    rows = x2d.shape[0]
    nt = rows // tm
    tps = tiles_per_seq
    n_out_seq = n_seq if seg is None else 1
    t_out = rows // n_out_seq
    row = lambda w: pl.BlockSpec((tm, w), lambda i: (i, 0))
    col = lambda r: pl.BlockSpec((r, tm), lambda i: (0, i))
    seq3 = lambda r: pl.BlockSpec((1, r, tm), lambda i: (i // tps, 0, i % tps))
    full = lambda a: pl.BlockSpec(a.shape, lambda i: (0,) * a.ndim)
    n_tab = tables[0].shape[1] // tm
    tab = pl.BlockSpec((ROT_DIM // 2, tm), lambda i: (0, i % n_tab))
    ws = [lw['norm1'], lw['w_tm'], lw['w_tr'], lw['conv_w'], lw['conv_b'], lw['ln_g'], lw['ln_b'], lw['mix'],
          lw['mixb'], lw['gain_t']]
    ins = [x2d] + ws + list(tables)
    in_specs = [row(D_MODEL)] + [full(a) for a in ws] + [tab] * 2
    sds = jax.ShapeDtypeStruct
    out_shape = [sds((rows, 512), BF16), sds((512, rows), F32), sds((n_out_seq, 256, t_out), F32),
                 sds((n_out_seq, 128, t_out), F32), sds((n_out_seq, 512, t_out), F32),
                 sds((2 * SUBLANES, rows), F32), sds((rows, LANES), F32), sds((n_seq, SUBLANES, 256), F32)]
    out_specs = [row(512), col(512), seq3(256), seq3(128), seq3(512), col(2 * SUBLANES), row(LANES)]
    scratch = []
    if seg is None:
        out_specs.append(pl.BlockSpec((1, SUBLANES, 256), lambda i: (i // tps, 0, 0)))
        out_shape += [sds((rows, LANES), F32), sds((rows, LANES), BF16), sds((rows, LANES), BF16),
                      sds((rows, 256), BF16), sds((n_seq, 384, t_out), BF16), sds((nt, SUBLANES, 256), F32)]
        out_specs += [row(LANES), row(LANES), row(LANES), row(256), seq3(384),
                      pl.BlockSpec((1, SUBLANES, 256), lambda i: (i, 0, 0))]
        scratch.append(pltpu.VMEM((SUBLANES, 256), F32))
    else:
        ins += list(halos)
        in_specs += [row(256), row(256)]
        out_specs.append(pl.BlockSpec((n_seq, SUBLANES, 256), lambda i: (0, 0, 0)))
        out_shape.append(sds((rows, 256), F32))
        out_specs.append(row(256))
    return pl.pallas_call(
        functools.partial(_proj_kernel, tiles_per_seq=tps, chunk=chunk, seg=seg),
        grid=(nt,), in_specs=in_specs, out_specs=out_specs, out_shape=out_shape, scratch_shapes=scratch,
        compiler_params=pltpu.CompilerParams(dimension_semantics=("arbitrary",), vmem_limit_bytes=VMEM_LIMIT),
        name="proj_prompt" if seg is None else "proj_sample",
    )(*ins)


def _compress_compute(src_ref, wa_ref, wb_ref, pt_ref, pb_ref, w2_ref, out_ref, outt_ref, groups):
    acc_a = jnp.zeros((groups, 2 * NSA_CMP_HIDDEN), F32)
    acc_b = jnp.zeros((groups, 2 * NSA_CMP_HIDDEN), F32)
    for p in range(NSA_CMP_STRIDE // 2):
        xp = jnp.concatenate([src_ref[pl.ds(2 * p, groups, stride=NSA_CMP_STRIDE), :],
                              src_ref[pl.ds(2 * p + 1, groups, stride=NSA_CMP_STRIDE), :]], axis=1)
        acc_a = acc_a + _dot((xp + pt_ref[p:p + 1, :]).astype(BF16), wa_ref[p])
        acc_b = acc_b + _dot((xp + pb_ref[p:p + 1, :]).astype(BF16), wb_ref[p])
    hdn = jax.nn.gelu(acc_a + pltpu.roll(acc_b, groups - 1, 0))
    out = _dot(hdn.astype(BF16), w2_ref[...])
    if out_ref is not None:
        out_ref[0] = out
    outt_ref[0] = out.T


def _compress_prompt_kernel(src_ref, wa_ref, wb_ref, pt_ref, pb_ref, w2_ref, out_ref, outt_ref, *, groups):
    _compress_compute(src_ref.at[0], wa_ref, wb_ref, pt_ref, pb_ref, w2_ref, out_ref, outt_ref, groups)


def _compress_paged_kernel(ptab_ref, *refs, groups, pps):
    del ptab_ref
    pages = refs[:pps]
    wa_ref, wb_ref, pt_ref, pb_ref, w2_ref, outt_ref, stage_ref = refs[pps:]
    j = pl.program_id(1)
    for k in range(pps):
        pg = pages[k][0, 0].reshape(2 * HEAD_DIM, PAGE_SIZE)
        stage_ref[pl.ds(pl.multiple_of((j * pps + k) * PAGE_SIZE, PAGE_SIZE), PAGE_SIZE), :] = pg.T

    @pl.when(j == pl.num_programs(1) - 1)
    def _():
        _compress_compute(stage_ref, wa_ref, wb_ref, pt_ref, pb_ref, w2_ref, None, outt_ref, groups)


def _compress_prompt_call(cmp_tm3, lw):
    b, t, _ = cmp_tm3.shape
    groups = t // NSA_CMP_STRIDE
    full = lambda a: pl.BlockSpec(a.shape, lambda i: (0,) * a.ndim)
    ws = [lw['cmp_wa'], lw['cmp_wb'], lw['cmp_pt'], lw['cmp_pb'], lw['cmp_w2']]
    return pl.pallas_call(
        functools.partial(_compress_prompt_kernel, groups=groups),
        grid=(b,),
        in_specs=[pl.BlockSpec((1, t, LANES), lambda i: (i, 0, 0))] + [full(a) for a in ws],
        out_specs=[pl.BlockSpec((1, groups, LANES), lambda i: (i, 0, 0)),
                   pl.BlockSpec((1, LANES, groups), lambda i: (i, 0, 0))],
        out_shape=[jax.ShapeDtypeStruct((b, groups, LANES), F32), jax.ShapeDtypeStruct((b, LANES, groups), F32)],
        compiler_params=pltpu.CompilerParams(dimension_semantics=("arbitrary",), vmem_limit_bytes=VMEM_LIMIT),
        name="compress_prompt",
    )(cmp_tm3, *ws)


def _pages_per_step(n_pages):
    pps = 16
    while n_pages % pps:
        pps //= 2
    return pps


def _compress_paged_call(cache5, layer, page_table, lw):
    b, n_pages = page_table.shape
    past = n_pages * PAGE_SIZE
    groups = past // NSA_CMP_STRIDE
    pps = _pages_per_step(n_pages)
    full = lambda a: pl.BlockSpec(a.shape, lambda i, j, pt: (0,) * a.ndim)
    ws = [lw['cmp_wa'], lw['cmp_wb'], lw['cmp_pt'], lw['cmp_pb'], lw['cmp_w2']]
    page_specs = [pl.BlockSpec((1, 1, 2, HEAD_DIM, PAGE_SIZE), functools.partial(
        lambda i, j, pt, k: (layer, pt[i, j * pps + k], 0, 0, 0), k=k)) for k in range(pps)]
    grid_spec = pltpu.PrefetchScalarGridSpec(
        num_scalar_prefetch=1, grid=(b, n_pages // pps),
        in_specs=page_specs + [full(a) for a in ws],
        out_specs=pl.BlockSpec((1, LANES, groups), lambda i, j, pt: (i, 0, 0)),
        scratch_shapes=[pltpu.VMEM((past, LANES), F32)])
    return pl.pallas_call(
        functools.partial(_compress_paged_kernel, groups=groups, pps=pps),
        grid_spec=grid_spec, out_shape=jax.ShapeDtypeStruct((b, LANES, groups), F32),
        compiler_params=pltpu.CompilerParams(dimension_semantics=("arbitrary", "arbitrary"),
                                             vmem_limit_bytes=VMEM_LIMIT),
        name="compress_sample",
    )(page_table, *([cache5] * pps), *ws)


def _topk_mask(score, n_cols, k):
    idx = _iota(score.shape, 1)
    rank = jnp.zeros(score.shape, jnp.int32)
    for j in range(n_cols):
        col = score[:, j:j + 1]
        rank = rank + jnp.where(col > score, 1, jnp.where(col == score, jnp.where(idx > j, 1, 0), 0))
    return rank < k


def _topk_mask_t(score, n_rows, k):
    idx = _iota(score.shape, 0)
    rank = jnp.zeros(score.shape, jnp.int32)
    for j in range(n_rows):
        row = score[j:j + 1, :]
        rank = rank + jnp.where(row > score, 1, jnp.where(row == score, jnp.where(idx > j, 1, 0), 0))
    return rank < k


def _softmax_axis(s, ok, axis):
    s = jnp.where(ok, s, NEG)
    m = jnp.max(s, axis=axis, keepdims=True)
    p = jnp.where(ok, jnp.exp(s - m), 0.0)
    l = jnp.sum(p, axis=axis, keepdims=True)
    return p / jnp.where(l > 0.0, l, 1.0)


def _forced_importance(imp, blk, cur):
    forced = (blk == 0) | (blk == cur) | (blk == cur - 1)
    imp = jnp.where(forced, BIG, imp)
    return jnp.where(blk > cur, NEG, imp)


def _nsa_prompt_kernel(qt_ref, gt_ref, kcvc_ref, kcvct_ref, slc_ref, win_ref, vt_ref, o_ref, sel_ref, *, t, qb, tk):
    iq = pl.program_id(1)
    s0 = iq * qb
    nq = NSA_HEADS * qb
    qt = qt_ref[...] * (HEAD_DIM ** -0.5)
    qst = jnp.concatenate([qt[h * HEAD_DIM:(h + 1) * HEAD_DIM, :] for h in range(NSA_HEADS)], axis=1)
    qst = jnp.concatenate([qst, jnp.zeros_like(qst)], axis=0).astype(BF16)
    qpos_q = s0 + _iota((1, qb), 1)
    qpos = _lane_tile(qpos_q, NSA_HEADS)

    n_cmp = (t - NSA_CMP_LEN) // NSA_CMP_STRIDE + 1
    n_blk = t // NSA_SLC_BLOCK
    kcvc = kcvc_ref[0].astype(BF16)
    g = kcvc.shape[0]
    s = _dot(kcvc, qst)
    n_idx = _iota(s.shape, 0)
    ok = (n_idx * NSA_CMP_STRIDE + (NSA_CMP_LEN - 1) <= qpos) & (n_idx < n_cmp)
    p = _softmax_axis(s, ok, 0)
    o_cmp = _dot(kcvct_ref[0, HEAD_DIM:2 * HEAD_DIM, :].astype(BF16), p.astype(BF16))
    psum = p[:, 0:qb]
    for h in range(1, NSA_HEADS):
        psum = psum + p[:, h * qb:(h + 1) * qb]
    cj = _iota((n_blk, g), 0)
    cn = _iota((n_blk, g), 1)
    cover = ((cn * NSA_CMP_STRIDE <= cj * NSA_SLC_BLOCK + (NSA_SLC_BLOCK - 1))
             & (cn * NSA_CMP_STRIDE + (NSA_CMP_LEN - 1) >= cj * NSA_SLC_BLOCK) & (cn < n_cmp)).astype(BF16)
    p_hi, p_lo = _split(psum)
    imp = _dot(cover, p_hi) + _dot(cover, p_lo)
    imp = _forced_importance(imp, _iota(imp.shape, 0), qpos_q // NSA_SLC_BLOCK)
    sel = jnp.where(_topk_mask_t(imp, n_blk, NSA_TOPN), 1.0, 0.0)
    sel_ref[...] = _lane_tile(sel, NSA_HEADS)

    bpt = tk // NSA_SLC_BLOCK

    def body(kt, carry):
        m, l, acc = carry
        k0 = pl.multiple_of(kt * tk, tk)
        sc = _dot(slc_ref[0, pl.ds(k0, tk), :], qst)
        slab = sel_ref[pl.ds(pl.multiple_of(kt * bpt, bpt), bpt), :]
        picked = jnp.concatenate([jnp.broadcast_to(slab[j:j + 1, :], (NSA_SLC_BLOCK, nq)) for j in range(bpt)], axis=0)
        okk = (picked > 0.5) & (k0 + _iota(sc.shape, 0) <= qpos)
        sc = jnp.where(okk, sc, NEG)
        m_new = jnp.maximum(m, jnp.max(sc, axis=0, keepdims=True))
        alpha = jnp.exp(m - m_new)
        pp = jnp.where(okk, jnp.exp(sc - m_new), 0.0)
        l = alpha * l + jnp.sum(pp, axis=0, keepdims=True)
        acc = alpha * acc + _dot(vt_ref[0, 0:HEAD_DIM, pl.ds(k0, tk)], pp.astype(BF16))
        return m_new, l, acc

    init = (jnp.full((1, nq), NEG, F32), jnp.zeros((1, nq), F32), jnp.zeros((HEAD_DIM, nq), F32))
    _, l, acc = lax.fori_loop(0, (s0 + qb + tk - 1) // tk, body, init)
    o_slc = acc / l

    wlen = NSA_WINDOW + qb
    start = pl.multiple_of(jnp.maximum(s0 - NSA_WINDOW, 0), qb)
    sw = _dot(win_ref[0, pl.ds(start, wlen), :], qst)
    kpos = start + _iota(sw.shape, 0)
    okw = (kpos <= qpos) & (kpos > qpos - NSA_WINDOW) & (kpos >= 0)
    pw = _softmax_axis(sw, okw, 0)
    o_win = _dot(vt_ref[0, HEAD_DIM:2 * HEAD_DIM, pl.ds(start, wlen)], pw.astype(BF16))

    gt = gt_ref[...]
    outs = []
    for h in range(NSA_HEADS):
        c = slice(h * qb, (h + 1) * qb)
        outs.append(gt[3 * h:3 * h + 1, :] * o_cmp[:, c] + gt[3 * h + 1:3 * h + 2, :] * o_slc[:, c]
                    + gt[3 * h + 2:3 * h + 3, :] * o_win[:, c])
    o_ref[...] = jnp.concatenate(outs, axis=0)


def _nsa_prompt_call(qt, gt, kcvc, kcvct, slc_tm3, win_tm3, vt3, *, qb=128, tk=512):
    b, t, _ = slc_tm3.shape
    nq = t // qb
    g = kcvc.shape[1]
    assert t % tk == 0 and t >= NSA_WINDOW + qb
    return pl.pallas_call(
        functools.partial(_nsa_prompt_kernel, t=t, qb=qb, tk=tk),
        grid=(b, nq),
        in_specs=[pl.BlockSpec((256, qb), lambda i, j: (0, i * nq + j)),
                  pl.BlockSpec((2 * SUBLANES, qb), lambda i, j: (0, i * nq + j)),
                  pl.BlockSpec((1, g, LANES), lambda i, j: (i, 0, 0)),
                  pl.BlockSpec((1, LANES, g), lambda i, j: (i, 0, 0)),
                  pl.BlockSpec((1, t, LANES), lambda i, j: (i, 0, 0)),
                  pl.BlockSpec((1, t, LANES), lambda i, j: (i, 0, 0)),
                  pl.BlockSpec((1, LANES, t), lambda i, j: (i, 2, 0))],
        out_specs=pl.BlockSpec((256, qb), lambda i, j: (0, i * nq + j)),
        out_shape=jax.ShapeDtypeStruct((256, b * t), F32),
        scratch_shapes=[pltpu.VMEM((t // NSA_SLC_BLOCK, NSA_HEADS * qb), F32)],
        compiler_params=pltpu.CompilerParams(dimension_semantics=("arbitrary", "arbitrary"),
                                             vmem_limit_bytes=VMEM_LIMIT),
        name="nsa_prompt",
    )(qt, gt, kcvc, kcvct, slc_tm3, win_tm3, vt3)


def _nsa_sample_kernel(ptab_ref, *refs, past, tq, pps, ppt):
    del ptab_ref
    pages = refs[:pps]
    q_ref, g_ref, kcvct_ref, new_ref, wst_ref, wnew_ref, o_ref, kt_ref, vt_ref = refs[pps:]
    j = pl.program_id(1)
    n_pages = past // PAGE_SIZE
    for k in range(pps):
        pg = pages[k][0, 0]
        kt_ref[j * pps + k] = pg[0].astype(BF16)
        vt_ref[j * pps + k] = pg[1].astype(BF16)

    @pl.when(j == pl.num_programs(1) - 1)
    def _():
        kt_ref[n_pages] = new_ref[0, 0].astype(BF16)
        vt_ref[n_pages] = new_ref[0, 1].astype(BF16)
        for k in range(1, ppt):
            kt_ref[n_pages + k] = jnp.zeros((HEAD_DIM, PAGE_SIZE), BF16)
            vt_ref[n_pages + k] = jnp.zeros((HEAD_DIM, PAGE_SIZE), BF16)

        rows = NSA_HEADS * tq
        qs = (q_ref[0] * (HEAD_DIM ** -0.5)).astype(BF16)
        qpos_q = past + _iota((tq, 1), 0)
        qpos = jnp.concatenate([qpos_q] * NSA_HEADS, axis=0)
        l_all = past + tq
        n_cmp = (l_all - NSA_CMP_LEN) // NSA_CMP_STRIDE + 1
        n_blk = -(-l_all // NSA_SLC_BLOCK)
        nb_pad = -(-n_blk // LANES) * LANES

        kcvct = kcvct_ref[0].astype(BF16)
        g = kcvct.shape[1]
        s = _dot(qs, kcvct[0:HEAD_DIM])
        n_idx = _iota(s.shape, 1)
        ok = (n_idx * NSA_CMP_STRIDE + (NSA_CMP_LEN - 1) <= qpos) & (n_idx < n_cmp)
        p = _softmax_axis(s, ok, 1)
        o_cmp = _dot_nt(p.astype(BF16), kcvct[HEAD_DIM:2 * HEAD_DIM])
        psum = p[0:tq]
        for h in range(1, NSA_HEADS):
            psum = psum + p[h * tq:(h + 1) * tq]
        cn = _iota((g, nb_pad), 0)
        cj = _iota((g, nb_pad), 1)
        cover = ((cn * NSA_CMP_STRIDE <= cj * NSA_SLC_BLOCK + (NSA_SLC_BLOCK - 1))
                 & (cn * NSA_CMP_STRIDE + (NSA_CMP_LEN - 1) >= cj * NSA_SLC_BLOCK) & (cn < n_cmp)).astype(BF16)
        p_hi, p_lo = _split(psum)
        imp = _dot(p_hi, cover) + _dot(p_lo, cover)
        imp = _forced_importance(imp, _iota(imp.shape, 1), qpos_q // NSA_SLC_BLOCK)
        sel = jnp.where(_topk_mask(imp, n_blk, NSA_TOPN), 1.0, 0.0).astype(BF16)
        sel = jnp.concatenate([sel] * NSA_HEADS, axis=0)

        tk = ppt * PAGE_SIZE

        def body(kt, carry):
            m, l, acc = carry
            ktile = jnp.concatenate([kt_ref[kt * ppt + c] for c in range(ppt)], axis=1)
            vtile = jnp.concatenate([vt_ref[kt * ppt + c] for c in range(ppt)], axis=1)
            sc = _dot(qs, ktile)
            expand = (_iota((nb_pad, tk), 0) == kt * (tk // NSA_SLC_BLOCK)
                      + _iota((nb_pad, tk), 1) // NSA_SLC_BLOCK).astype(BF16)
            okk = (_dot(sel, expand) > 0.5) & (kt * tk + _iota(sc.shape, 1) <= qpos)
            sc = jnp.where(okk, sc, NEG)
            m_new = jnp.maximum(m, jnp.max(sc, axis=-1, keepdims=True))
            alpha = jnp.exp(m - m_new)
            pp = jnp.where(okk, jnp.exp(sc - m_new), 0.0)
            l = alpha * l + jnp.sum(pp, axis=-1, keepdims=True)
            acc = alpha * acc + _dot_nt(pp.astype(BF16), vtile)
            return m_new, l, acc

        init = (jnp.full((rows, 1), NEG, F32), jnp.zeros((rows, 1), F32), jnp.zeros((rows, HEAD_DIM), F32))
        _, l, acc = lax.fori_loop(0, n_pages // ppt + 1, body, init)
        o_slc = acc / l

        kw = jnp.concatenate([wst_ref[0, 0, 0], wnew_ref[0, 0]], axis=1).astype(BF16)
        vw = jnp.concatenate([wst_ref[0, 0, 1], wnew_ref[0, 1]], axis=1).astype(BF16)
        sw = _dot(qs, kw)
        kpos = past - NSA_WINDOW + _iota(sw.shape, 1)
        okw = (kpos <= qpos) & (kpos > qpos - NSA_WINDOW) & (kpos >= 0)
        o_win = _dot_nt(_softmax_axis(sw, okw, 1).astype(BF16), vw)

        gg = g_ref[0]
        o_ref[0] = gg[:, 0:1] * o_cmp + gg[:, 1:2] * o_slc + gg[:, 2:3] * o_win


def _nsa_sample_call(cache5, layer, page_table, q32, g32, kcvct, newt, wstate5, wnewt, *, ppt=4):
    b, n_pages = page_table.shape
    past = n_pages * PAGE_SIZE
    tq = q32.shape[1] // NSA_HEADS
    pps = _pages_per_step(n_pages)
    assert n_pages % ppt == 0 and tq <= PAGE_SIZE and past >= NSA_WINDOW
    assert (past + tq - NSA_CMP_LEN) // NSA_CMP_STRIDE + 1 <= past // NSA_CMP_STRIDE - 1
    rows = NSA_HEADS * tq
    page_specs = [pl.BlockSpec((1, 1, 2, HEAD_DIM, PAGE_SIZE), functools.partial(
        lambda i, j, pt, k: (layer, pt[i, j * pps + k], 1, 0, 0), k=k)) for k in range(pps)]
    per_b = lambda a: pl.BlockSpec((1,) + a.shape[1:], lambda i, j, pt: (i,) + (0,) * (a.ndim - 1))
    grid_spec = pltpu.PrefetchScalarGridSpec(
        num_scalar_prefetch=1, grid=(b, n_pages // pps),
        in_specs=page_specs + [per_b(q32), per_b(g32), per_b(kcvct), per_b(newt),
                               pl.BlockSpec((1, 1, 2, HEAD_DIM, NSA_WINDOW), lambda i, j, pt: (layer, i, 0, 0, 0)),
                               per_b(wnewt)],
        out_specs=pl.BlockSpec((1, rows, HEAD_DIM), lambda i, j, pt: (i, 0, 0)),
        scratch_shapes=[pltpu.VMEM((n_pages + ppt, HEAD_DIM, PAGE_SIZE), BF16),
                        pltpu.VMEM((n_pages + ppt, HEAD_DIM, PAGE_SIZE), BF16)])
    return pl.pallas_call(
        functools.partial(_nsa_sample_kernel, past=past, tq=tq, pps=pps, ppt=ppt),
        grid_spec=grid_spec, out_shape=jax.ShapeDtypeStruct((b, rows, HEAD_DIM), F32),
        compiler_params=pltpu.CompilerParams(dimension_semantics=("arbitrary", "arbitrary"),
                                             vmem_limit_bytes=VMEM_LIMIT),
        name="nsa_sample",
    )(page_table, *([cache5] * pps), q32, g32, kcvct, newt, wstate5, wnewt)


def _moba_prompt_kernel(qt_ref, kmean_ref, k_ref, vt_ref, o_ref, sel_ref, *, nb):
    i = pl.program_id(1)
    r0 = pl.multiple_of(i * MOBA_BLOCK, MOBA_BLOCK)
    qt = qt_ref[...]
    rowgrp = _iota(qt.shape, 0) // HEAD_DIM
    kmean = kmean_ref[0]
    qpad = []
    for h in range(MOBA_HEADS):
        qh = jnp.where(rowgrp == h, qt, 0.0)
        gs = _dot_3pass(kmean, qh)
        n_idx = _iota(gs.shape, 0)
        gs = jnp.where(n_idx < i, gs, NEG)
        sel_ref[h] = jnp.where(_topk_mask_t(gs, nb, MOBA_TOPK) & (n_idx < i), 1.0, 0.0)
        qpad.append((qh * (HEAD_DIM ** -0.5)).astype(BF16))

    def step(carry, kb, vb, oks):
        new = []
        for h in range(MOBA_HEADS):
            m, l, acc = carry[h]
            s = jnp.where(oks[h], _dot(kb, qpad[h]), NEG)
            m_new = jnp.maximum(m, jnp.max(s, axis=0, keepdims=True))
            alpha = jnp.exp(m - m_new)
            p = jnp.where(oks[h], jnp.exp(s - m_new), 0.0)
            l = alpha * l + jnp.sum(p, axis=0, keepdims=True)
            acc = alpha * acc + _dot(vb[h * HEAD_DIM:(h + 1) * HEAD_DIM, :], p.astype(BF16))
            new.append((m_new, l, acc))
        return tuple(new)

    def body(n, carry):
        k0 = pl.multiple_of(n * MOBA_BLOCK, MOBA_BLOCK)
        oks = [jnp.broadcast_to(sel_ref[h, pl.ds(n, 1), :] > 0.5, (MOBA_BLOCK, MOBA_BLOCK))
               for h in range(MOBA_HEADS)]
        return step(carry, k_ref[0, pl.ds(k0, MOBA_BLOCK), :], vt_ref[0, :, pl.ds(k0, MOBA_BLOCK)], oks)

    one = (jnp.full((1, MOBA_BLOCK), NEG, F32), jnp.zeros((1, MOBA_BLOCK), F32),
           jnp.zeros((HEAD_DIM, MOBA_BLOCK), F32))
    carry = lax.fori_loop(0, i, body, (one,) * MOBA_HEADS)
    causal = _iota((MOBA_BLOCK, MOBA_BLOCK), 0) <= _iota((MOBA_BLOCK, MOBA_BLOCK), 1)
    carry = step(carry, k_ref[0, pl.ds(r0, MOBA_BLOCK), :], vt_ref[0, :, pl.ds(r0, MOBA_BLOCK)],
                 [causal] * MOBA_HEADS)
    o_ref[...] = jnp.concatenate([acc / l for _, l, acc in carry], axis=0)


def _moba_prompt_call(qt, kmean3, mk_tm3, vt3):
    b, t, _ = mk_tm3.shape
    nb = t // MOBA_BLOCK
    nbp = kmean3.shape[1]
    return pl.pallas_call(
        functools.partial(_moba_prompt_kernel, nb=nb),
        grid=(b, nb),
        in_specs=[pl.BlockSpec((256, MOBA_BLOCK), lambda i, j: (1, i * nb + j)),
                  pl.BlockSpec((1, nbp, 256), lambda i, j: (i, 0, 0)),
                  pl.BlockSpec((1, t, 256), lambda i, j: (i, 0, 0)),
                  pl.BlockSpec((1, 256, t), lambda i, j: (i, 0, 0))],
        out_specs=pl.BlockSpec((256, MOBA_BLOCK), lambda i, j: (0, i * nb + j)),
        out_shape=jax.ShapeDtypeStruct((256, b * t), F32),
        scratch_shapes=[pltpu.VMEM((MOBA_HEADS, nbp, MOBA_BLOCK), F32)],
        compiler_params=pltpu.CompilerParams(dimension_semantics=("arbitrary", "arbitrary"),
                                             vmem_limit_bytes=VMEM_LIMIT),
        name="moba_prompt",
    )(qt, kmean3, mk_tm3, vt3)


def _moba_sample_kernel(ptab_ref, *refs, past, tq, pps):
    del ptab_ref
    pages = refs[:pps]
    q_ref, new_ref, o_ref, kt_ref, vt_ref, km_ref = refs[pps:]
    j = pl.program_id(1)
    ppb = MOBA_BLOCK // PAGE_SIZE

    @pl.when(j == 0)
    def _():
        km_ref[...] = jnp.zeros_like(km_ref)

    lane = _iota(km_ref.shape, 1)
    for k in range(0, pps, ppb):
        ksum = jnp.zeros((256, PAGE_SIZE), F32)
        for kk in range(ppb):
            pg = pages[k + kk][0, 0]
            kt_ref[j * pps + k + kk] = pg[0].astype(BF16)
            vt_ref[j * pps + k + kk] = pg[1].astype(BF16)
            ksum = ksum + pg[0]
        col = jnp.sum(ksum, axis=1, keepdims=True) * (1.0 / MOBA_BLOCK)
        km_ref[...] = jnp.where(lane == j * (pps // ppb) + k // ppb, col, km_ref[...])

    @pl.when(j == pl.num_programs(1) - 1)
    def _():
        n_past = past // MOBA_BLOCK
        q = q_ref[0]
        rows = q.shape[0]
        gs = _dot_3pass(q, km_ref[...])
        n_idx = _iota(gs.shape, 1)
        gs = jnp.where(n_idx < n_past, gs, NEG)
        sel = jnp.where(_topk_mask(gs, n_past, MOBA_TOPK) & (n_idx < n_past), 1.0, 0.0)
        qb = (q * (HEAD_DIM ** -0.5)).astype(BF16)

        def step(carry, kb, vb, ok):
            m, l, acc = carry
            s = jnp.where(ok, _dot(qb, kb), NEG)
            m_new = jnp.maximum(m, jnp.max(s, axis=-1, keepdims=True))
            alpha = jnp.exp(m - m_new)
            p = jnp.where(ok, jnp.exp(s - m_new), 0.0)
            return m_new, alpha * l + jnp.sum(p, axis=-1, keepdims=True), alpha * acc + _dot_nt(p.astype(BF16), vb)

        def body(n, carry):
            picked = jnp.sum(jnp.where(n_idx == n, sel, 0.0), axis=-1, keepdims=True) > 0.5
            kb = jnp.concatenate([kt_ref[n * ppb + c] for c in range(ppb)], axis=1)
            vb = jnp.concatenate([vt_ref[n * ppb + c] for c in range(ppb)], axis=1)
            return step(carry, kb, vb, jnp.broadcast_to(picked, (rows, MOBA_BLOCK)))

        init = (jnp.full((rows, 1), NEG, F32), jnp.zeros((rows, 1), F32), jnp.zeros((rows, 256), F32))
        carry = lax.fori_loop(0, n_past, body, init)
        own_ok = _iota((rows, PAGE_SIZE), 1) <= _iota((rows, PAGE_SIZE), 0) % tq
        _, l, acc = step(carry, new_ref[0, 0].astype(BF16), new_ref[0, 1].astype(BF16), own_ok)
        oh = acc / l
        grp = _iota((tq, 256), 1) // HEAD_DIM
        out = oh[0:tq]
        for h in range(1, MOBA_HEADS):
            out = jnp.where(grp == h, oh[h * tq:(h + 1) * tq], out)
        o_ref[0] = out


def _moba_sample_call(cache5, layer, page_table, qbd, newt):
    b, n_pages = page_table.shape
    past = n_pages * PAGE_SIZE
    rows = qbd.shape[1]
    tq = rows // MOBA_HEADS
    pps = _pages_per_step(n_pages)
    assert past % MOBA_BLOCK == 0 and pps % (MOBA_BLOCK // PAGE_SIZE) == 0 and tq <= PAGE_SIZE
    assert past // MOBA_BLOCK <= LANES
    page_specs = [pl.BlockSpec((1, 1, 2, 256, PAGE_SIZE), functools.partial(
        lambda i, j, pt, k: (layer, pt[i, j * pps + k], 0, 0, 0), k=k)) for k in range(pps)]
    per_b = lambda a: pl.BlockSpec((1,) + a.shape[1:], lambda i, j, pt: (i,) + (0,) * (a.ndim - 1))
    grid_spec = pltpu.PrefetchScalarGridSpec(
        num_scalar_prefetch=1, grid=(b, n_pages // pps),
        in_specs=page_specs + [per_b(qbd), per_b(newt)],
        out_specs=pl.BlockSpec((1, tq, 256), lambda i, j, pt: (i, 0, 0)),
        scratch_shapes=[pltpu.VMEM((n_pages, 256, PAGE_SIZE), BF16), pltpu.VMEM((n_pages, 256, PAGE_SIZE), BF16),
                        pltpu.VMEM((256, LANES), F32)])
    return pl.pallas_call(
        functools.partial(_moba_sample_kernel, past=past, tq=tq, pps=pps),
        grid_spec=grid_spec, out_shape=jax.ShapeDtypeStruct((b, tq, 256), F32),
        compiler_params=pltpu.CompilerParams(dimension_semantics=("arbitrary", "arbitrary"),
                                             vmem_limit_bytes=VMEM_LIMIT),
        name="moba_sample",
    )(page_table, *([cache5] * pps), qbd, newt)


def _merge_kernel(x_ref, oab_ref, oct_ref, odt_ref, n1_ref, wgt_ref, wbr_ref, wo_ref, y_ref):
    x = x_ref[...]
    xn = (x * lax.rsqrt(jnp.mean(x * x, axis=-1, keepdims=True) + EPS) * n1_ref[...]).astype(BF16)
    branches = [oab_ref[:, 0:256], oab_ref[:, 256:512], oct_ref[...].T.astype(BF16), odt_ref[...].T.astype(BF16)]
    merged = jnp.zeros(x.shape, F32)
    for n in range(N_BRANCH):
        g = jax.nn.sigmoid(_dot_nt(xn, wgt_ref[n * D_MODEL:(n + 1) * D_MODEL, :]))
        merged = merged + g * _dot(branches[n], wbr_ref[n])
    y_ref[...] = x + _dot(merged.astype(BF16), wo_ref[...])


def _merge_call(x2d, oab, oct, odt, lw, *, tm):
    rows = x2d.shape[0]
    row = lambda w: pl.BlockSpec((tm, w), lambda i: (i, 0))
    col = lambda r: pl.BlockSpec((r, tm), lambda i: (0, i))
    full = lambda a: pl.BlockSpec(a.shape, lambda i: (0,) * a.ndim)
    ws = [lw['norm1'], lw['wg_t'], lw['w_branch'], lw['w_o']]
    return pl.pallas_call(
        _merge_kernel, grid=(rows // tm,),
        in_specs=[row(D_MODEL), row(512), col(256), col(256)] + [full(a) for a in ws],
        out_specs=row(D_MODEL), out_shape=jax.ShapeDtypeStruct((rows, D_MODEL), F32),
        compiler_params=pltpu.CompilerParams(dimension_semantics=("arbitrary",), vmem_limit_bytes=VMEM_LIMIT),
        name="merge",
    )(x2d, oab, oct, odt, *ws)


def _ffn_kernel(*refs, tiles_per_seq, seg):
    it = iter(refs)
    x_ref, n2_ref, wgate_ref, wup_ref, cw_ref, cb_ref, wdown_ref = (next(it) for _ in range(7))
    if seg is not None:
        h1_ref, h2_ref = next(it), next(it)
    y_ref, tail_ref = next(it), next(it)
    if seg is None:
        prev_ref = next(it)

    x = x_ref[...]
    tm = x.shape[0]
    h2 = (x * lax.rsqrt(jnp.mean(x * x, axis=-1, keepdims=True) + EPS) * n2_ref[...]).astype(BF16)
    gate = _dot(h2, wgate_ref[...])
    if seg is None:
        @pl.when(pl.program_id(0) % tiles_per_seq == 0)
        def _():
            prev_ref[...] = jnp.zeros_like(prev_ref)

        g1, g2 = _shift_rows(gate, prev_ref[SUBLANES - 1:SUBLANES, :], prev_ref[SUBLANES - 2:SUBLANES - 1, :], None)
        prev_ref[...] = gate[tm - SUBLANES:, :]
        tail_ref[0] = gate[tm - SUBLANES:, :]
    else:
        g1, g2 = _shift_rows(gate, h1_ref[...], h2_ref[...], seg)
        tail_ref[...] = gate.reshape(tail_ref.shape)
    ac = g2 * cw_ref[0:1, :] + cb_ref[...] + g1 * cw_ref[1:2, :] + gate * cw_ref[2:3, :]
    act = jax.nn.silu(ac) * _dot(h2, wup_ref[...])
    y_ref[...] = x + _dot(act.astype(BF16), wdown_ref[...])


def _ffn_call(x2d, lw, halos, *, tm, tiles_per_seq, n_seq, seg):
    rows = x2d.shape[0]
    row = lambda w: pl.BlockSpec((tm, w), lambda i: (i, 0))
    full = lambda a: pl.BlockSpec(a.shape, lambda i: (0,) * a.ndim)
    ws = [lw['norm2'], lw['w_gate'], lw['w_up'], lw['ffn_conv_w'], lw['ffn_conv_b'], lw['w_down']]
    ins = [x2d] + ws
    in_specs = [row(D_MODEL)] + [full(a) for a in ws]
    scratch = []
    if seg is None:
        tail_spec = pl.BlockSpec((1, SUBLANES, D_FF), lambda i: (i // tiles_per_seq, 0, 0))
        scratch.append(pltpu.VMEM((SUBLANES, D_FF), F32))
    else:
        ins += list(halos)
        in_specs += [row(D_FF), row(D_FF)]
        tail_spec = pl.BlockSpec((n_seq, SUBLANES, D_FF), lambda i: (0, 0, 0))
    return pl.pallas_call(
        functools.partial(_ffn_kernel, tiles_per_seq=tiles_per_seq, seg=seg),
        grid=(rows // tm,), in_specs=in_specs, out_specs=[row(D_MODEL), tail_spec],
        out_shape=[jax.ShapeDtypeStruct((rows, D_MODEL), F32), jax.ShapeDtypeStruct((n_seq, SUBLANES, D_FF), F32)],
        scratch_shapes=scratch,
        compiler_params=pltpu.CompilerParams(dimension_semantics=("arbitrary",), vmem_limit_bytes=VMEM_LIMIT),
        name="ffn_prompt" if seg is None else "ffn_sample",
    )(*ins)


def _rope_tables_t(pos):
    half = ROT_DIM // 2
    inv = jnp.power(jnp.float32(ROPE_THETA), -jnp.arange(half, dtype=F32) / half)
    ang = inv[:, None] * pos.astype(F32)[None, :]
    return jnp.cos(ang), jnp.sin(ang)


def _layer_weights(l, p, sample_tq, dec_batch):
    wt = jnp.transpose(p['w_in'], (2, 0, 1))[:, l, :]
    o_ck, o_cv, o_g = 1536, 1728, 1920
    o_dq = o_g + 3 * NSA_HEADS
    o_br = o_dq + 3 * 256
    hd = HEAD_DIM
    w_tm = jnp.concatenate([wt[0:1280], wt[o_g:o_g + 12], jnp.zeros((LANES - 12, D_MODEL), F32)], axis=0)
    w_tr = jnp.concatenate([
        wt[1280:1536], wt[o_dq:o_dq + 256],
        wt[o_ck:o_ck + hd], wt[o_cv:o_cv + hd], wt[o_ck + hd:o_ck + 2 * hd], wt[o_cv + hd:o_cv + 2 * hd],
        wt[o_ck + 2 * hd:o_ck + 3 * hd], wt[o_cv + 2 * hd:o_cv + 3 * hd],
        wt[o_dq + 256:o_dq + 768], wt[o_g:o_g + 12], jnp.zeros((2 * SUBLANES - 12, D_MODEL), F32)], axis=0)
    lw = {'w_tm': w_tm.astype(BF16), 'w_tr': w_tr.astype(BF16), 'wg_t': wt[o_br:].astype(BF16)}
    row = lambda a: a.reshape(1, -1)
    lw['norm1'] = row(p['norm1'][l])
    lw['norm2'] = row(p['norm2'][l])
    lw['conv_w'] = p['conv_w'][l]
    lw['conv_b'] = row(p['conv_b'][l])
    lw['ln_g'] = row(p['gmlp_ln_g'][l])
    lw['ln_b'] = row(p['gmlp_ln_b'][l])
    tril = jnp.tril(jnp.ones((GMLP_CHUNK, GMLP_CHUNK), bool))
    ws = jnp.where(tril[None], p['gmlp_ws'][l], 0.0)
    bs = p['gmlp_bs'][l]
    lw['mix_p'] = ws.reshape(GMLP_GROUPS * GMLP_CHUNK, GMLP_CHUNK).astype(BF16)
    lw['mixb_p'] = jnp.repeat(bs.T, HEAD_DIM, axis=1)
    eye = jnp.eye(dec_batch, dtype=F32)
    lw['mix_s'] = jnp.concatenate([jnp.kron(eye, ws[g, :sample_tq, :sample_tq]) for g in range(GMLP_GROUPS)],
                                  axis=0).astype(BF16)
    lw['mixb_s'] = jnp.tile(jnp.repeat(bs.T[:sample_tq], HEAD_DIM, axis=1), (dec_batch, 1))
    kg = p['nsa_k_gain'][l]
    one = jnp.ones((hd,), F32)
    gains = jnp.concatenate([jnp.tile(p['nsa_q_gain'][l], NSA_HEADS), jnp.tile(p['moba_q_gain'][l], MOBA_HEADS),
                             kg[0], one, kg[1], one, kg[2], one, jnp.tile(p['moba_k_gain'][l], MOBA_HEADS)])
    lw['gain_t'] = jnp.broadcast_to(gains[:, None], (R_MV, LANES))
    w1c = p['nsa_cmp_w1'][l].reshape(2, NSA_CMP_LEN, HEAD_DIM, NSA_CMP_HIDDEN)
    pos = p['nsa_cmp_pos'][l]
    zero = jnp.zeros((HEAD_DIM, NSA_CMP_HIDDEN), F32)

    def pair_w(r_off):
        mats = []
        for pp in range(NSA_CMP_STRIDE // 2):
            blocks = []
            for r in (2 * pp, 2 * pp + 1):
                blocks.append(jnp.concatenate([w1c[0, r + r_off], zero], axis=1))
                blocks.append(jnp.concatenate([zero, w1c[1, r + r_off]], axis=1))
            mats.append(jnp.concatenate(blocks, axis=0))
        return jnp.stack(mats).astype(BF16)

    def pair_pos(r_off):
        return jnp.stack([jnp.concatenate([pos[0, 2 * pp + r_off], pos[1, 2 * pp + r_off],
                                           pos[0, 2 * pp + 1 + r_off], pos[1, 2 * pp + 1 + r_off]])
                          for pp in range(NSA_CMP_STRIDE // 2)])

    lw['cmp_wa'], lw['cmp_wb'] = pair_w(0), pair_w(NSA_CMP_STRIDE)
    lw['cmp_pt'], lw['cmp_pb'] = pair_pos(0), pair_pos(NSA_CMP_STRIDE)
    w2 = p['nsa_cmp_w2'][l]
    z2 = jnp.zeros((NSA_CMP_HIDDEN, HEAD_DIM), F32)
    lw['cmp_w2'] = jnp.concatenate([jnp.concatenate([w2[0], z2], axis=1),
                                    jnp.concatenate([z2, w2[1]], axis=1)], axis=0).astype(BF16)
    lw['w_branch'] = p['w_branch'][l].astype(BF16)
    lw['w_o'] = p['w_o'][l].astype(BF16)
    lw['w_gate'] = p['w_ffn_gate'][l].astype(BF16)
    lw['w_up'] = p['w_ffn_up'][l].astype(BF16)
    lw['w_down'] = p['w_ffn_down'][l].astype(BF16)
    lw['ffn_conv_w'] = p['ffn_conv_w'][l]
    lw['ffn_conv_b'] = row(p['ffn_conv_b'][l])
    return lw


def _halos(buf, tq):
    b, _, c = buf.shape
    z = jnp.zeros((b, tq - 1, c), F32)
    h1 = jnp.concatenate([buf[:, 1:2], z], axis=1)
    h2 = jnp.concatenate([buf[:, 0:2], z[:, 1:]], axis=1)
    return h1.reshape(b * tq, c), h2.reshape(b * tq, c)


def _row_tile(rows, cap):
    tm = cap
    while rows % tm:
        tm //= 2
    return tm


def _per_seq_pages(xt, b, tq):
    c = xt.shape[0]
    x = xt.reshape(c, b, tq).transpose(1, 0, 2)
    return jnp.concatenate([x, jnp.zeros((b, c, PAGE_SIZE - tq), F32)], axis=-1)


def _prompt_layer(x2d, lw, tables, b, t):
    tm = _row_tile(t, 512)
    assert tm % MOBA_BLOCK == 0
    tps = t // tm
    pw = dict(lw, mix=lw['mix_p'], mixb=lw['mixb_p'])
    (oab, qt, nsat, wint, mobat, gt, _, ztail, cmp_tm, slc_tm, win_tm, mk_tm, vt16, kmean) = _proj_call(
        x2d, pw, tables, None, tm=tm, tiles_per_seq=tps, n_seq=b, chunk=GMLP_CHUNK, seg=None)
    kcvc, kcvct = _compress_prompt_call(cmp_tm.reshape(b, t, LANES), lw)
    o_ct = _nsa_prompt_call(qt, gt, kcvc, kcvct, slc_tm.reshape(b, t, LANES), win_tm.reshape(b, t, LANES), vt16)
    nb = t // MOBA_BLOCK
    kmean3 = kmean[:, 0:tm // MOBA_BLOCK].reshape(b, nb, 256)
    nbp = -(-nb // SUBLANES) * SUBLANES
    if nbp != nb:
        kmean3 = jnp.concatenate([kmean3, jnp.zeros((b, nbp - nb, 256), F32)], axis=1)
    o_dt = _moba_prompt_call(qt, kmean3, mk_tm.reshape(b, t, 256), vt16)
    x_mid = _merge_call(x2d, oab, o_ct, o_dt, lw, tm=tm)
    y, ftail = _ffn_call(x_mid, lw, None, tm=tm, tiles_per_seq=tps, n_seq=b, seg=None)
    hd = HEAD_DIM
    wlen = min(NSA_WINDOW, t)
    state = (nsat.reshape(b, 4, hd, t).transpose(0, 3, 1, 2),
             mobat.reshape(b, 2, MOBA_HEADS, hd, t).transpose(0, 4, 1, 2, 3),
             wint[:, :, t - wlen:].reshape(b, 2, hd, wlen).transpose(0, 3, 1, 2),
             ztail[:, SUBLANES - 2:], ftail[:, SUBLANES - 2:])
    return y, state


def _sample_layer(x2d, lw, tables, b, tq, layer, nsa_cache5, moba_cache5, page_table, win_state5, conv_state,
                  ffn_state):
    rows = b * tq
    hd = HEAD_DIM
    pw = dict(lw, mix=lw['mix_s'], mixb=lw['mixb_s'])
    oab, qt, nsat, wint, mobat, _, gtm, ztail, vn = _proj_call(
        x2d, pw, tables, _halos(conv_state, tq), tm=rows, tiles_per_seq=1, n_seq=b, chunk=rows, seg=tq)
    nsat, wint, mobat = nsat[0], wint[0], mobat[0]
    kcvct = _compress_paged_call(nsa_cache5, layer, page_table, lw)
    q32 = qt[0:256].reshape(NSA_HEADS, hd, b, tq).transpose(2, 0, 3, 1).reshape(b, NSA_HEADS * tq, hd)
    g4 = gtm[:, 0:3 * NSA_HEADS].reshape(b, tq, NSA_HEADS, 3).transpose(0, 2, 1, 3).reshape(b, NSA_HEADS * tq, 3)
    g32 = jnp.concatenate([g4, jnp.zeros((b, NSA_HEADS * tq, LANES - 3), F32)], axis=-1)
    newt = _per_seq_pages(nsat[2 * hd:4 * hd], b, tq).reshape(b, 2, hd, PAGE_SIZE)
    wnewt = _per_seq_pages(wint, b, tq).reshape(b, 2, hd, PAGE_SIZE)
    o32 = _nsa_sample_call(nsa_cache5, layer, page_table, q32, g32, kcvct, newt, win_state5, wnewt)
    o_ct = o32.reshape(b, NSA_HEADS, tq, hd).transpose(1, 3, 0, 2).reshape(256, rows)
    qm = qt[256:512].reshape(MOBA_HEADS, hd, b, tq).transpose(2, 0, 3, 1)
    eye = jnp.eye(MOBA_HEADS, dtype=F32)
    qbd = (qm[:, :, :, None, :] * eye[None, :, None, :, None]).reshape(b, MOBA_HEADS * tq, 256)
    mnewt = _per_seq_pages(mobat, b, tq).reshape(b, 2, 256, PAGE_SIZE)
    o_dt = _moba_sample_call(moba_cache5, layer, page_table, qbd, mnewt).transpose(2, 0, 1).reshape(256, rows)
    x_mid = _merge_call(x2d, oab, o_ct, o_dt, lw, tm=rows)
    y, ftail = _ffn_call(x_mid, lw, _halos(ffn_state, tq), tm=rows, tiles_per_seq=1, n_seq=b, seg=tq)
    wnew = wint.reshape(2, hd, b, tq).transpose(2, 0, 1, 3)
    win_all = jnp.concatenate([win_state5[layer], wnew], axis=-1)
    win_new = win_all[..., win_all.shape[-1] - NSA_WINDOW:].transpose(0, 3, 1, 2)
    state = (nsat.T.reshape(b, tq, 4, hd), mobat.T.reshape(b, tq, 2, MOBA_HEADS, hd), win_new,
             ztail[:, tq - 2:tq], ftail[:, tq - 2:tq], vn.reshape(b, tq, 256))
    return y, state


def kernel(x_prompt, x_sample, cache_nsa_kv, cache_moba_kv, state_nsa_win, state_conv, state_ffn_conv, page_table, norm1, w_in, conv_w, conv_b, gmlp_ln_g, gmlp_ln_b, gmlp_ws, gmlp_bs, nsa_q_gain, nsa_k_gain, nsa_cmp_pos, nsa_cmp_w1, nsa_cmp_w2, moba_q_gain, moba_k_gain, w_branch, w_o, norm2, w_ffn_gate, w_ffn_up, ffn_conv_w, ffn_conv_b, w_ffn_down):
    params = dict(norm1=norm1, w_in=w_in, conv_w=conv_w, conv_b=conv_b, gmlp_ln_g=gmlp_ln_g, gmlp_ln_b=gmlp_ln_b,
                  gmlp_ws=gmlp_ws, gmlp_bs=gmlp_bs, nsa_q_gain=nsa_q_gain, nsa_k_gain=nsa_k_gain,
                  nsa_cmp_pos=nsa_cmp_pos, nsa_cmp_w1=nsa_cmp_w1, nsa_cmp_w2=nsa_cmp_w2, moba_q_gain=moba_q_gain,
                  moba_k_gain=moba_k_gain, w_branch=w_branch, w_o=w_o, norm2=norm2, w_ffn_gate=w_ffn_gate,
                  w_ffn_up=w_ffn_up, ffn_conv_w=ffn_conv_w, ffn_conv_b=ffn_conv_b, w_ffn_down=w_ffn_down)
    bp, tp, _ = x_prompt.shape
    bs, ts, _ = x_sample.shape
    depth = w_in.shape[0]
    n_pool = cache_nsa_kv.shape[1]
    past = page_table.shape[1] * PAGE_SIZE
    assert ts == SUBLANES and state_nsa_win.shape[2] == NSA_WINDOW
    nsa_cache5 = jnp.transpose(cache_nsa_kv, (0, 1, 3, 4, 2))
    moba_cache5 = jnp.transpose(cache_moba_kv, (0, 1, 3, 4, 5, 2)).reshape(depth, n_pool, 2, 256, PAGE_SIZE)
    win_state5 = jnp.transpose(state_nsa_win, (0, 1, 3, 4, 2))
    tab_p = _rope_tables_t(jnp.arange(tp, dtype=jnp.int32))
    tab_s = tuple(jnp.tile(a, (1, bs)) for a in _rope_tables_t(past + jnp.arange(ts, dtype=jnp.int32)))
    yp = x_prompt.reshape(bp * tp, D_MODEL)
    ys = x_sample.reshape(bs * ts, D_MODEL)
    sp, ss = [], []
    for l in range(depth):
        lw = _layer_weights(l, params, ts, bs)
        yp, st_p = _prompt_layer(yp, lw, tab_p, bp, tp)
        ys, st_s = _sample_layer(ys, lw, tab_s, bs, ts, l, nsa_cache5, moba_cache5, page_table, win_state5,
                                 state_conv[l], state_ffn_conv[l])
        sp.append(st_p)
        ss.append(st_s)
    stack = lambda lst, k: jnp.stack([s[k] for s in lst])
    return (yp.reshape(bp, tp, D_MODEL), ys.reshape(bs, ts, D_MODEL),
            stack(sp, 0), stack(ss, 0), stack(sp, 1), stack(ss, 1), stack(sp, 2), stack(ss, 2),
            stack(sp, 3), stack(ss, 3), stack(sp, 4), stack(ss, 4), stack(ss, 5))
```

```python
import functools

import jax
import jax.numpy as jnp
from jax import lax
from jax.experimental import pallas as pl
from jax.experimental.pallas import tpu as pltpu

F32 = jnp.float32
BF16 = jnp.bfloat16

D_MODEL = 1024
HEAD_DIM = 64
ROT_DIM = HEAD_DIM // 4
ROPE_THETA = 500000.0
PAGE_SIZE = 128
BRANCH_CH = 256
N_BRANCH = 4
GMLP_GROUPS = 4
GMLP_CHUNK = 128
NSA_HEADS = 4
NSA_CMP_LEN = 32
NSA_CMP_STRIDE = 16
NSA_CMP_HIDDEN = 256
NSA_SLC_BLOCK = 64
NSA_TOPN = 16
NSA_WINDOW = 512
MOBA_HEADS = 4
MOBA_BLOCK = 256
MOBA_TOPK = 3
D_FF = 2816
EPS = 1e-6
NEG = -1e30
BIG = 1e30

SLC_TILE = 512
LANES = 128
SUBLANES = 8
VMEM_LIMIT = 56 * 1024 * 1024

C_AH, C_AB, C_AC, C_U, C_V, C_G, C_END = 0, 256, 512, 768, 1024, 1280, 1408
R_Q, R_NSA, R_WIN, R_MK, R_MV, R_G, R_END = 0, 512, 768, 896, 1152, 1408, 1424
NORMED_GROUPS = (0, 1, 2, 3, 4, 5, 6, 7, 8, 10, 12, 14, 15, 16, 17)

_NT = (((1,), (1,)), ((), ()))


def _dot(a, b):
    return jnp.dot(a, b, preferred_element_type=F32)


def _dot_nt(a, b):
    return lax.dot_general(a, b, _NT, preferred_element_type=F32)


def _split(a):
    hi = a.astype(BF16)
    lo = (a - hi.astype(F32)).astype(BF16)
    return hi, lo


def _dot_3pass(a, b):
    ah, al = _split(a)
    bh, bl = _split(b)
    return _dot(ah, bh) + _dot(ah, bl) + _dot(al, bh)


def _iota(shape, dim):
    return lax.broadcasted_iota(jnp.int32, shape, dim)


def _lane_tile(a, n):
    return a if n == 1 else jnp.concatenate([a] * n, axis=1)


def _norm_rope_t(xh, gain, cos, sin):
    ss = jnp.sum(xh * xh, axis=0, keepdims=True)
    y = xh * lax.rsqrt(ss * (1.0 / HEAD_DIM) + EPS) * gain
    half = ROT_DIM // 2
    y0, y1 = y[0:half], y[half:ROT_DIM]
    return jnp.concatenate([y0 * cos - y1 * sin, y1 * cos + y0 * sin, y[ROT_DIM:]], axis=0)


def _shift_rows(z, prev1, prev2, seg):
    rows = _iota(z.shape, 0)
    z1 = pltpu.roll(z, 1, 0)
    z2 = pltpu.roll(z, 2, 0)
    if seg is None:
        z1 = jnp.where(rows == 0, prev1, z1)
        z2 = jnp.where(rows == 0, prev2, jnp.where(rows == 1, prev1, z2))
    else:
        t = rows % seg
        z1 = jnp.where(t == 0, prev1, z1)
        z2 = jnp.where(t <= 1, prev2, z2)
    return z1, z2


def _proj_kernel(*refs, tiles_per_seq, chunk, seg):
    it = iter(refs)
    x_ref, n1_ref, wtm_ref, wtr_ref, cw_ref, cb_ref, lng_ref, lnb_ref, mix_ref, mixb_ref = (next(it) for _ in range(10))
    gain_ref, cos_ref, sin_ref = (next(it) for _ in range(3))
    if seg is not None:
        h1_ref, h2_ref = next(it), next(it)
    oab_ref, qt_ref, nsat_ref, wint_ref, mobat_ref, gt_ref, gtm_ref, ztail_ref = (next(it) for _ in range(8))
    if seg is not None:
        vn_ref = next(it)
    else:
        cmp_ref, slc_ref, win_ref, mk_ref, vt_ref, kmean_ref, zprev_ref = (next(it) for _ in range(7))

    x = x_ref[...]
    tm = x.shape[0]
    xn = (x * lax.rsqrt(jnp.mean(x * x, axis=-1, keepdims=True) + EPS) * n1_ref[...]).astype(BF16)
    proj = _dot_nt(xn, wtm_ref[...])
    projt = _dot_nt(wtr_ref[...], xn)

    z = proj[:, C_AC:C_AC + 256] * proj[:, C_AH:C_AH + 256]
    if seg is None:
        @pl.when(pl.program_id(0) % tiles_per_seq == 0)
        def _():
            zprev_ref[...] = jnp.zeros_like(zprev_ref)

        z1, z2 = _shift_rows(z, zprev_ref[SUBLANES - 1:SUBLANES, :], zprev_ref[SUBLANES - 2:SUBLANES - 1, :], None)
        zprev_ref[...] = z[tm - SUBLANES:, :]
        ztail_ref[0] = z[tm - SUBLANES:, :]
    else:
        z1, z2 = _shift_rows(z, h1_ref[...], h2_ref[...], seg)
        ztail_ref[...] = z.reshape(ztail_ref.shape)
    ya = z2 * cw_ref[0:1, :] + cb_ref[...] + z1 * cw_ref[1:2, :] + z * cw_ref[2:3, :]
    oab_ref[:, 0:256] = (proj[:, C_AB:C_AB + 256] * ya).astype(BF16)

    u = jax.nn.gelu(proj[:, C_U:C_U + 256])
    v = jax.nn.gelu(proj[:, C_V:C_V + 256])
    vc = v - jnp.mean(v, axis=-1, keepdims=True)
    vn = vc * lax.rsqrt(jnp.mean(vc * vc, axis=-1, keepdims=True) + EPS) * lng_ref[...] + lnb_ref[...]
    if seg is not None:
        vn_ref[...] = vn
    vnb = vn.astype(BF16)
    grp = _iota((chunk, 256), 1) // HEAD_DIM
    for c in range(tm // chunk):
        r = _dot(mix_ref[...], vnb[c * chunk:(c + 1) * chunk, :])
        s = r[0:chunk]
        for g in range(1, GMLP_GROUPS):
            s = jnp.where(grp == g, r[g * chunk:(g + 1) * chunk], s)
        s = s + mixb_ref[...]
        oab_ref[c * chunk:(c + 1) * chunk, 256:512] = (u[c * chunk:(c + 1) * chunk, :] * s).astype(BF16)

    gtm_ref[...] = jax.nn.sigmoid(proj[:, C_G:C_END])
    gt_ref[...] = jax.nn.sigmoid(projt[R_G:R_END, :])

    cos, sin = cos_ref[...], sin_ref[...]
    rep = tm // LANES
    groups = []
    for g in range(R_MV // HEAD_DIM):
        xh = projt[g * HEAD_DIM:(g + 1) * HEAD_DIM, :]
        if g in NORMED_GROUPS:
            xh = _norm_rope_t(xh, _lane_tile(gain_ref[g * HEAD_DIM:(g + 1) * HEAD_DIM, :], rep), cos, sin)
        groups.append(xh)
    qkv = jnp.concatenate(groups, axis=0)
    qt_ref[...] = qkv[R_Q:R_NSA]
    nsat_ref[0] = qkv[R_NSA:R_WIN]
    wint_ref[0] = qkv[R_WIN:R_MK]
    mobat_ref[0, 0:256, :] = qkv[R_MK:R_MV]
    vmt = projt[R_MV:R_G, :]
    mobat_ref[0, 256:512, :] = vmt

    if seg is None:
        nsa_tm = qkv[R_NSA:R_WIN].T
        cmp_ref[...] = nsa_tm[:, 0:LANES]
        lane = _iota((tm, LANES), 1) - HEAD_DIM
        blk_in_tile = (_iota((tm, LANES), 0) // NSA_SLC_BLOCK) % (SLC_TILE // NSA_SLC_BLOCK)
        slc_ref[...] = jnp.where(lane < 0, nsa_tm[:, LANES:2 * LANES],
                                 jnp.where(lane == blk_in_tile, 1.0, 0.0)).astype(BF16)
        win_ref[...] = qkv[R_WIN:R_MK].T.astype(BF16)
        mk_tm = qkv[R_MK:R_MV].T
        mk_ref[...] = mk_tm.astype(BF16)
        vt_ref[0, 0:256, :] = vmt.astype(BF16)
        vt_ref[0, 256:320, :] = qkv[R_NSA + 192:R_NSA + 256].astype(BF16)
        vt_ref[0, 320:384, :] = qkv[R_WIN + 64:R_WIN + 128].astype(BF16)
        kmean_ref[...] = jnp.zeros_like(kmean_ref)
        for r in range(tm // MOBA_BLOCK):
            kmean_ref[0, r:r + 1, :] = jnp.sum(mk_tm[r * MOBA_BLOCK:(r + 1) * MOBA_BLOCK, :], axis=0,
                                               keepdims=True) * (1.0 / MOBA_BLOCK)


def _proj_call(x2d, lw, tables, halos, *, tm, tiles_per_seq, n_seq, chunk, seg):
    rows = x2d.shape[0]
    nt = rows // tm
    tps = tiles_per_seq
    n_out_seq = n_seq if seg is None else 1
    t_out = rows // n_out_seq
    row = lambda w: pl.BlockSpec((tm, w), lambda i: (i, 0))
    col = lambda r: pl.BlockSpec((r, tm), lambda i: (0, i))
    seq3 = lambda r: pl.BlockSpec((1, r, tm), lambda i: (i // tps, 0, i % tps))
    full = lambda a: pl.BlockSpec(a.shape, lambda i: (0,) * a.ndim)
    n_tab = tables[0].shape[1] // tm
    tab = pl.BlockSpec((ROT_DIM // 2, tm), lambda i: (0, i % n_tab))
    ws = [lw['norm1'], lw['w_tm'], lw['w_tr'], lw['conv_w'], lw['conv_b'], lw['ln_g'], lw['ln_b'], lw['mix'],
          lw['mixb'], lw['gain_t']]
    ins = [x2d] + ws + list(tables)
    in_specs = [row(D_MODEL)] + [full(a) for a in ws] + [tab] * 2
    sds = jax.ShapeDtypeStruct
    out_shape = [sds((rows, 512), BF16), sds((512, rows), F32), sds((n_out_seq, 256, t_out), F32),
                 sds((n_out_seq, 128, t_out), F32), sds((n_out_seq, 512, t_out), F32),
                 sds((2 * SUBLANES, rows), F32), sds((rows, LANES), F32), sds((n_seq, SUBLANES, 256), F32)]
    out_specs = [row(512), col(512), seq3(256), seq3(128), seq3(512), col(2 * SUBLANES), row(LANES)]
    scratch = []
    if seg is None:
        out_specs.append(pl.BlockSpec((1, SUBLANES, 256), lambda i: (i // tps, 0, 0)))
        out_shape += [sds((rows, LANES), F32), sds((rows, LANES), BF16), sds((rows, LANES), BF16),
                      sds((rows, 256), BF16), sds((n_seq, 384, t_out), BF16), sds((nt, SUBLANES, 256), F32)]
        out_specs += [row(LANES), row(LANES), row(LANES), row(256), seq3(384),
                      pl.BlockSpec((1, SUBLANES, 256), lambda i: (i, 0, 0))]
        scratch.append(pltpu.VMEM((SUBLANES, 256), F32))
    else:
        ins += list(halos)
        in_specs += [row(256), row(256)]
        out_specs.append(pl.BlockSpec((n_seq, SUBLANES, 256), lambda i: (0, 0, 0)))
        out_shape.append(sds((rows, 256), F32))
        out_specs.append(row(256))
    return pl.pallas_call(
        functools.partial(_proj_kernel, tiles_per_seq=tps, chunk=chunk, seg=seg),
        grid=(nt,), in_specs=in_specs, out_specs=out_specs, out_shape=out_shape, scratch_shapes=scratch,
        compiler_params=pltpu.CompilerParams(dimension_semantics=("arbitrary",), vmem_limit_bytes=VMEM_LIMIT),
        name="proj_prompt" if seg is None else "proj_sample",
    )(*ins)


def _compress_compute(src_ref, wa_ref, wb_ref, pt_ref, pb_ref, w2_ref, out_ref, outt_ref, groups):
    acc_a = jnp.zeros((groups, 2 * NSA_CMP_HIDDEN), F32)
    acc_b = jnp.zeros((groups, 2 * NSA_CMP_HIDDEN), F32)
    for p in range(NSA_CMP_STRIDE // 2):
        xp = jnp.concatenate([src_ref[pl.ds(2 * p, groups, stride=NSA_CMP_STRIDE), :],
                              src_ref[pl.ds(2 * p + 1, groups, stride=NSA_CMP_STRIDE), :]], axis=1)
        acc_a = acc_a + _dot((xp + pt_ref[p:p + 1, :]).astype(BF16), wa_ref[p])
        acc_b = acc_b + _dot((xp + pb_ref[p:p + 1, :]).astype(BF16), wb_ref[p])
    hdn = jax.nn.gelu(acc_a + pltpu.roll(acc_b, groups - 1, 0))
    out = _dot(hdn.astype(BF16), w2_ref[...])
    if out_ref is not None:
        out_ref[0] = out
    outt_ref[0] = out.T


def _compress_prompt_kernel(src_ref, wa_ref, wb_ref, pt_ref, pb_ref, w2_ref, out_ref, outt_ref, *, groups):
    _compress_compute(src_ref.at[0], wa_ref, wb_ref, pt_ref, pb_ref, w2_ref, out_ref, outt_ref, groups)


def _compress_paged_kernel(ptab_ref, *refs, groups, pps):
    del ptab_ref
    pages = refs[:pps]
    wa_ref, wb_ref, pt_ref, pb_ref, w2_ref, outt_ref, stage_ref = refs[pps:]
    j = pl.program_id(1)
    for k in range(pps):
        pg = pages[k][0, 0].reshape(2 * HEAD_DIM, PAGE_SIZE)
        stage_ref[pl.ds(pl.multiple_of((j * pps + k) * PAGE_SIZE, PAGE_SIZE), PAGE_SIZE), :] = pg.T

    @pl.when(j == pl.num_programs(1) - 1)
    def _():
        _compress_compute(stage_ref, wa_ref, wb_ref, pt_ref, pb_ref, w2_ref, None, outt_ref, groups)


def _compress_prompt_call(cmp_tm3, lw):
    b, t, _ = cmp_tm3.shape
    groups = t // NSA_CMP_STRIDE
    full = lambda a: pl.BlockSpec(a.shape, lambda i: (0,) * a.ndim)
    ws = [lw['cmp_wa'], lw['cmp_wb'], lw['cmp_pt'], lw['cmp_pb'], lw['cmp_w2']]
    return pl.pallas_call(
        functools.partial(_compress_prompt_kernel, groups=groups),
        grid=(b,),
        in_specs=[pl.BlockSpec((1, t, LANES), lambda i: (i, 0, 0))] + [full(a) for a in ws],
        out_specs=[pl.BlockSpec((1, groups, LANES), lambda i: (i, 0, 0)),
                   pl.BlockSpec((1, LANES, groups), lambda i: (i, 0, 0))],
        out_shape=[jax.ShapeDtypeStruct((b, groups, LANES), F32), jax.ShapeDtypeStruct((b, LANES, groups), F32)],
        compiler_params=pltpu.CompilerParams(dimension_semantics=("arbitrary",), vmem_limit_bytes=VMEM_LIMIT),
        name="compress_prompt",
    )(cmp_tm3, *ws)


def _pages_per_step(n_pages):
    pps = 16
    while n_pages % pps:
        pps //= 2
    return pps


def _compress_paged_call(cache5, layer, page_table, lw):
    b, n_pages = page_table.shape
    past = n_pages * PAGE_SIZE
    groups = past // NSA_CMP_STRIDE
    pps = _pages_per_step(n_pages)
    full = lambda a: pl.BlockSpec(a.shape, lambda i, j, pt: (0,) * a.ndim)
    ws = [lw['cmp_wa'], lw['cmp_wb'], lw['cmp_pt'], lw['cmp_pb'], lw['cmp_w2']]
    page_specs = [pl.BlockSpec((1, 1, 2, HEAD_DIM, PAGE_SIZE), functools.partial(
        lambda i, j, pt, k: (layer, pt[i, j * pps + k], 0, 0, 0), k=k)) for k in range(pps)]
    grid_spec = pltpu.PrefetchScalarGridSpec(
        num_scalar_prefetch=1, grid=(b, n_pages // pps),
        in_specs=page_specs + [full(a) for a in ws],
        out_specs=pl.BlockSpec((1, LANES, groups), lambda i, j, pt: (i, 0, 0)),
        scratch_shapes=[pltpu.VMEM((past, LANES), F32)])
    return pl.pallas_call(
        functools.partial(_compress_paged_kernel, groups=groups, pps=pps),
        grid_spec=grid_spec, out_shape=jax.ShapeDtypeStruct((b, LANES, groups), F32),
        compiler_params=pltpu.CompilerParams(dimension_semantics=("arbitrary", "arbitrary"),
                                             vmem_limit_bytes=VMEM_LIMIT),
        name="compress_sample",
    )(page_table, *([cache5] * pps), *ws)


def _topk_mask(score, n_cols, k):
    idx = _iota(score.shape, 1)
    rank = jnp.zeros(score.shape, jnp.int32)
    for j in range(n_cols):
        col = score[:, j:j + 1]
        rank = rank + jnp.where(col > score, 1, jnp.where(col == score, jnp.where(idx > j, 1, 0), 0))
    return rank < k


def _topk_mask_t(score, n_rows, k):
    idx = _iota(score.shape, 0)
    rank = jnp.zeros(score.shape, jnp.int32)
    for j in range(n_rows):
        row = score[j:j + 1, :]
        rank = rank + jnp.where(row > score, 1, jnp.where(row == score, jnp.where(idx > j, 1, 0), 0))
    return rank < k


def _softmax_axis(s, ok, axis):
    s = jnp.where(ok, s, NEG)
    m = jnp.max(s, axis=axis, keepdims=True)
    p = jnp.where(ok, jnp.exp(s - m), 0.0)
    l = jnp.sum(p, axis=axis, keepdims=True)
    return p / jnp.where(l > 0.0, l, 1.0)


def _forced_importance(imp, blk, cur):
    forced = (blk == 0) | (blk == cur) | (blk == cur - 1)
    imp = jnp.where(forced, BIG, imp)
    return jnp.where(blk > cur, NEG, imp)


def _nsa_prompt_kernel(qt_ref, gt_ref, kcvc_ref, kcvct_ref, slc_ref, win_ref, vt_ref, o_ref, sel_ref, s_ref, *, t, qb, tk):
    iq = pl.program_id(1)
    s0 = iq * qb
    nq = NSA_HEADS * qb
    qt = qt_ref[...] * (HEAD_DIM ** -0.5)
    qst = jnp.concatenate([qt[h * HEAD_DIM:(h + 1) * HEAD_DIM, :] for h in range(NSA_HEADS)], axis=1)
    qst = jnp.concatenate([qst, jnp.zeros_like(qst)], axis=0).astype(BF16)
    qpos_q = s0 + _iota((1, qb), 1)
    qpos = _lane_tile(qpos_q, NSA_HEADS)

    n_cmp = (t - NSA_CMP_LEN) // NSA_CMP_STRIDE + 1
    n_blk = t // NSA_SLC_BLOCK
    kcvc = kcvc_ref[0].astype(BF16)
    g = kcvc.shape[0]
    s = _dot(kcvc, qst)
    n_idx = _iota(s.shape, 0)
    ok = (n_idx * NSA_CMP_STRIDE + (NSA_CMP_LEN - 1) <= qpos) & (n_idx < n_cmp)
    s = jnp.where(ok, s, NEG)
    m = jnp.max(s, axis=0, keepdims=True)
    e = jnp.exp(s - m)
    p = e * jnp.where(m > 0.5 * NEG, 1.0 / jnp.sum(e, axis=0, keepdims=True), 0.0)
    o_cmp = _dot(kcvct_ref[0, HEAD_DIM:2 * HEAD_DIM, :].astype(BF16), p.astype(BF16))
    psum = p[:, 0:qb]
    for h in range(1, NSA_HEADS):
        psum = psum + p[:, h * qb:(h + 1) * qb]
    cj = _iota((n_blk, g), 0)
    cn = _iota((n_blk, g), 1)
    cover = ((cn * NSA_CMP_STRIDE <= cj * NSA_SLC_BLOCK + (NSA_SLC_BLOCK - 1))
             & (cn * NSA_CMP_STRIDE + (NSA_CMP_LEN - 1) >= cj * NSA_SLC_BLOCK) & (cn < n_cmp)).astype(BF16)
    p_hi, p_lo = _split(psum)
    imp = _dot(cover, p_hi) + _dot(cover, p_lo)
    imp = _forced_importance(imp, _iota(imp.shape, 0), qpos_q // NSA_SLC_BLOCK)
    cur = qpos_q // NSA_SLC_BLOCK
    picked = _topk_mask_t(imp, n_blk, NSA_TOPN) & (_iota(imp.shape, 0) <= cur)
    sel_ref[...] = _lane_tile(jnp.where(picked, 0.0, NEG), NSA_HEADS)

    bpt = tk // NSA_SLC_BLOCK
    q64 = qst[0:HEAD_DIM].astype(F32)
    zpad = jnp.zeros((LANES - HEAD_DIM - bpt, nq), F32)
    kd = s0 // tk

    def scores(kt):
        k0 = pl.multiple_of(kt * tk, tk)
        slab = sel_ref[pl.ds(pl.multiple_of(kt * bpt, bpt), bpt), :]
        w = jnp.concatenate([q64, slab, zpad], axis=0).astype(BF16)
        return _dot(slc_ref[0, pl.ds(k0, tk), :], w)

    def fold(x, op):
        return op(x.reshape(tk // SUBLANES, SUBLANES, nq), axis=0)

    def sweep1(kt, m8):
        sc = scores(kt)
        s_ref[kt] = sc
        return jnp.maximum(m8, fold(sc, jnp.max))

    m8 = lax.fori_loop(0, kd, sweep1, jnp.full((SUBLANES, nq), NEG, F32))
    sc = scores(kd)
    sc = jnp.where(kd * tk + _iota(sc.shape, 0) <= qpos, sc, NEG)
    s_ref[kd] = sc
    m = jnp.max(jnp.maximum(m8, fold(sc, jnp.max)), axis=0, keepdims=True)

    def sweep2(kt, carry):
        l8, acc = carry
        k0 = pl.multiple_of(kt * tk, tk)
        pp = jnp.exp(s_ref[kt] - m)
        return l8 + fold(pp, jnp.sum), acc + _dot(vt_ref[0, 0:HEAD_DIM, pl.ds(k0, tk)], pp.astype(BF16))

    l8, acc = lax.fori_loop(0, kd + 1, sweep2, (jnp.zeros((SUBLANES, nq), F32), jnp.zeros((HEAD_DIM, nq), F32)))
    o_slc = acc / jnp.sum(l8, axis=0, keepdims=True)

    wlen = NSA_WINDOW + qb
    start = pl.multiple_of(jnp.maximum(s0 - NSA_WINDOW, 0), qb)
    sw = _dot(win_ref[0, pl.ds(start, wlen), :], qst)
    back = (qpos - start) - _iota(sw.shape, 0)
    sw = jnp.where((back & -NSA_WINDOW) == 0, sw, NEG)
    ew = jnp.exp(sw - jnp.max(sw, axis=0, keepdims=True))
    o_win = (_dot(vt_ref[0, HEAD_DIM:2 * HEAD_DIM, pl.ds(start, wlen)], ew.astype(BF16))
             / jnp.sum(ew, axis=0, keepdims=True))

    gt = gt_ref[...]
    outs = []
    for h in range(NSA_HEADS):
        c = slice(h * qb, (h + 1) * qb)
        outs.append(gt[3 * h:3 * h + 1, :] * o_cmp[:, c] + gt[3 * h + 1:3 * h + 2, :] * o_slc[:, c]
                    + gt[3 * h + 2:3 * h + 3, :] * o_win[:, c])
    o_ref[...] = jnp.concatenate(outs, axis=0)


def _nsa_prompt_call(qt, gt, kcvc, kcvct, slc_tm3, win_tm3, vt3, *, qb=128, tk=SLC_TILE):
    b, t, _ = slc_tm3.shape
    nq = t // qb
    g = kcvc.shape[1]
    assert t % tk == 0 and t >= NSA_WINDOW + qb
    return pl.pallas_call(
        functools.partial(_nsa_prompt_kernel, t=t, qb=qb, tk=tk),
        grid=(b, nq),
        in_specs=[pl.BlockSpec((256, qb), lambda i, j: (0, i * nq + j)),
                  pl.BlockSpec((2 * SUBLANES, qb), lambda i, j: (0, i * nq + j)),
                  pl.BlockSpec((1, g, LANES), lambda i, j: (i, 0, 0)),
                  pl.BlockSpec((1, LANES, g), lambda i, j: (i, 0, 0)),
                  pl.BlockSpec((1, t, LANES), lambda i, j: (i, 0, 0)),
                  pl.BlockSpec((1, t, LANES), lambda i, j: (i, 0, 0)),
                  pl.BlockSpec((1, LANES, t), lambda i, j: (i, 2, 0))],
        out_specs=pl.BlockSpec((256, qb), lambda i, j: (0, i * nq + j)),
        out_shape=jax.ShapeDtypeStruct((256, b * t), F32),
        scratch_shapes=[pltpu.VMEM((t // NSA_SLC_BLOCK, NSA_HEADS * qb), F32),
                        pltpu.VMEM((t // tk, tk, NSA_HEADS * qb), F32)],
        compiler_params=pltpu.CompilerParams(dimension_semantics=("arbitrary", "arbitrary"),
                                             vmem_limit_bytes=VMEM_LIMIT),
        name="nsa_prompt",
    )(qt, gt, kcvc, kcvct, slc_tm3, win_tm3, vt3)


def _nsa_sample_kernel(ptab_ref, *refs, past, tq, pps):
    del ptab_ref
    pages = refs[:pps]
    q_ref, g_ref, kcvct_ref, new_ref, wst_ref, wnew_ref, o_ref, kt_ref, vt_ref, e_ref = refs[pps:]
    j = pl.program_id(1)

    @pl.when((pl.program_id(0) == 0) & (j == 0))
    def _():
        e_ref[...] = (_iota(e_ref.shape, 0) == _iota(e_ref.shape, 1) // NSA_SLC_BLOCK).astype(BF16)

    for k in range(pps):
        pg = pages[k][0, 0]
        c0 = pl.multiple_of((j * pps + k) * PAGE_SIZE, PAGE_SIZE)
        kt_ref[:, pl.ds(c0, PAGE_SIZE)] = pg[0].astype(BF16)
        vt_ref[:, pl.ds(c0, PAGE_SIZE)] = pg[1].astype(BF16)

    @pl.when(j == pl.num_programs(1) - 1)
    def _():
        rows = NSA_HEADS * tq
        qs = (q_ref[0] * (HEAD_DIM ** -0.5)).astype(BF16)
        qpos_q = past + _iota((tq, 1), 0)
        qpos = jnp.concatenate([qpos_q] * NSA_HEADS, axis=0)
        l_all = past + tq
        n_cmp = (l_all - NSA_CMP_LEN) // NSA_CMP_STRIDE + 1
        n_blk = -(-l_all // NSA_SLC_BLOCK)
        nb_pad = -(-n_blk // LANES) * LANES

        kcvct = kcvct_ref[0].astype(BF16)
        g = kcvct.shape[1]
        s = _dot(qs, kcvct[0:HEAD_DIM])
        n_idx = _iota(s.shape, 1)
        ok = (n_idx * NSA_CMP_STRIDE + (NSA_CMP_LEN - 1) <= qpos) & (n_idx < n_cmp)
        p = _softmax_axis(s, ok, 1)
        o_cmp = _dot_nt(p.astype(BF16), kcvct[HEAD_DIM:2 * HEAD_DIM])
        psum = p[0:tq]
        for h in range(1, NSA_HEADS):
            psum = psum + p[h * tq:(h + 1) * tq]
        cn = _iota((g, nb_pad), 0)
        cj = _iota((g, nb_pad), 1)
        cover = ((cn * NSA_CMP_STRIDE <= cj * NSA_SLC_BLOCK + (NSA_SLC_BLOCK - 1))
                 & (cn * NSA_CMP_STRIDE + (NSA_CMP_LEN - 1) >= cj * NSA_SLC_BLOCK) & (cn < n_cmp)).astype(BF16)
        p_hi, p_lo = _split(psum)
        imp = _dot(p_hi, cover) + _dot(p_lo, cover)
        imp = _forced_importance(imp, _iota(imp.shape, 1), qpos_q // NSA_SLC_BLOCK)
        picked = _topk_mask(imp, n_blk, NSA_TOPN)

        nbe = e_ref.shape[0]
        bias = jnp.where(picked[:, 0:nbe], 0.0, NEG).astype(BF16)
        bias = jnp.concatenate([bias] * NSA_HEADS, axis=0)
        sc = _dot(qs, kt_ref[...]) + _dot(bias, e_ref[...])
        new_ok = past + _iota((rows, PAGE_SIZE), 1) <= qpos
        sc_new = jnp.where(new_ok, _dot(qs, new_ref[0, 0].astype(BF16)), NEG)
        m = jnp.maximum(jnp.max(sc, axis=-1, keepdims=True), jnp.max(sc_new, axis=-1, keepdims=True))
        pp = jnp.exp(sc - m)
        pp_new = jnp.exp(sc_new - m)
        l = jnp.sum(pp, axis=-1, keepdims=True) + jnp.sum(pp_new, axis=-1, keepdims=True)
        o_slc = (_dot_nt(pp.astype(BF16), vt_ref[...]) + _dot_nt(pp_new.astype(BF16), new_ref[0, 1].astype(BF16))) / l

        kw = jnp.concatenate([wst_ref[0, 0, 0], wnew_ref[0, 0]], axis=1).astype(BF16)
        vw = jnp.concatenate([wst_ref[0, 0, 1], wnew_ref[0, 1]], axis=1).astype(BF16)
        sw = _dot(qs, kw)
        kpos = past - NSA_WINDOW + _iota(sw.shape, 1)
        okw = (kpos <= qpos) & (kpos > qpos - NSA_WINDOW) & (kpos >= 0)
        o_win = _dot_nt(_softmax_axis(sw, okw, 1).astype(BF16), vw)

        gg = g_ref[0]
        o_ref[0] = gg[:, 0:1] * o_cmp + gg[:, 1:2] * o_slc + gg[:, 2:3] * o_win


def _nsa_sample_call(cache5, layer, page_table, q32, g32, kcvct, newt, wstate5, wnewt):
    b, n_pages = page_table.shape
    past = n_pages * PAGE_SIZE
    tq = q32.shape[1] // NSA_HEADS
    pps = _pages_per_step(n_pages)
    nbe = -(-(past // NSA_SLC_BLOCK) // LANES) * LANES
    assert tq <= NSA_SLC_BLOCK and past >= NSA_WINDOW
    assert (past + tq - NSA_CMP_LEN) // NSA_CMP_STRIDE + 1 <= past // NSA_CMP_STRIDE - 1
    rows = NSA_HEADS * tq
    page_specs = [pl.BlockSpec((1, 1, 2, HEAD_DIM, PAGE_SIZE), functools.partial(
        lambda i, j, pt, k: (layer, pt[i, j * pps + k], 1, 0, 0), k=k)) for k in range(pps)]
    per_b = lambda a: pl.BlockSpec((1,) + a.shape[1:], lambda i, j, pt: (i,) + (0,) * (a.ndim - 1))
    grid_spec = pltpu.PrefetchScalarGridSpec(
        num_scalar_prefetch=1, grid=(b, n_pages // pps),
        in_specs=page_specs + [per_b(q32), per_b(g32), per_b(kcvct), per_b(newt),
                               pl.BlockSpec((1, 1, 2, HEAD_DIM, NSA_WINDOW), lambda i, j, pt: (layer, i, 0, 0, 0)),
                               per_b(wnewt)],
        out_specs=pl.BlockSpec((1, rows, HEAD_DIM), lambda i, j, pt: (i, 0, 0)),
        scratch_shapes=[pltpu.VMEM((HEAD_DIM, past), BF16), pltpu.VMEM((HEAD_DIM, past), BF16),
                        pltpu.VMEM((nbe, past), BF16)])
    return pl.pallas_call(
        functools.partial(_nsa_sample_kernel, past=past, tq=tq, pps=pps),
        grid_spec=grid_spec, out_shape=jax.ShapeDtypeStruct((b, rows, HEAD_DIM), F32),
        compiler_params=pltpu.CompilerParams(dimension_semantics=("arbitrary", "arbitrary"),
                                             vmem_limit_bytes=VMEM_LIMIT),
        name="nsa_sample",
    )(page_table, *([cache5] * pps), q32, g32, kcvct, newt, wstate5, wnewt)


def _moba_prompt_kernel(qt_ref, kmean_ref, k_ref, vt_ref, o_ref, sel_ref, acc_ref, *, nb):
    i = pl.program_id(1)
    r0 = pl.multiple_of(i * MOBA_BLOCK, MOBA_BLOCK)
    qb = MOBA_BLOCK
    nq = MOBA_HEADS * qb
    qt = qt_ref[...]
    rowgrp = _iota(qt.shape, 0) // HEAD_DIM
    kmean = kmean_ref[0]
    qpad = []
    for h in range(MOBA_HEADS):
        qh = jnp.where(rowgrp == h, qt, 0.0)
        gs = _dot_3pass(kmean, qh)
        n_idx = _iota(gs.shape, 0)
        gs = jnp.where(n_idx < i, gs, NEG)
        sel_ref[:, h * qb:(h + 1) * qb] = jnp.where(_topk_mask_t(gs, nb, MOBA_TOPK) & (n_idx < i), 0.0, NEG)
        qpad.append((qh * (HEAD_DIM ** -0.5)).astype(BF16))
    qcat = jnp.concatenate(qpad, axis=1)

    def block(n):
        k0 = pl.multiple_of(n * MOBA_BLOCK, MOBA_BLOCK)
        return k_ref[0, pl.ds(k0, MOBA_BLOCK), :], vt_ref[0, :, pl.ds(k0, MOBA_BLOCK)]

    def fold(x, op):
        return op(x.reshape(MOBA_BLOCK // SUBLANES, SUBLANES, nq), axis=0)

    def sweep1(n, m8):
        return jnp.maximum(m8, fold(_dot(block(n)[0], qcat) + sel_ref[pl.ds(n, 1), :], jnp.max))

    m8 = lax.fori_loop(0, i, sweep1, jnp.full((SUBLANES, nq), NEG, F32))
    own_k, own_v = block(i)
    causal = _iota((MOBA_BLOCK, nq), 0) <= _iota((MOBA_BLOCK, nq), 1) % qb
    own_s = jnp.where(causal, _dot(own_k, qcat), NEG)
    m = jnp.max(jnp.maximum(m8, fold(own_s, jnp.max)), axis=0, keepdims=True)

    acc_ref[...] = jnp.zeros_like(acc_ref)

    def accumulate(p, vb):
        pb = p.astype(BF16)
        for h in range(MOBA_HEADS):
            acc_ref[h] += _dot(vb[h * HEAD_DIM:(h + 1) * HEAD_DIM, :], pb[:, h * qb:(h + 1) * qb])
        return fold(p, jnp.sum)

    def sweep2(n, l8):
        kb, vb = block(n)
        return l8 + accumulate(jnp.exp(_dot(kb, qcat) + (sel_ref[pl.ds(n, 1), :] - m)), vb)

    l8 = lax.fori_loop(0, i, sweep2, jnp.zeros((SUBLANES, nq), F32))
    l = jnp.sum(l8 + accumulate(jnp.exp(own_s - m), own_v), axis=0, keepdims=True)
    o_ref[...] = jnp.concatenate([acc_ref[h] / l[:, h * qb:(h + 1) * qb] for h in range(MOBA_HEADS)], axis=0)


def _moba_prompt_call(qt, kmean3, mk_tm3, vt3):
    b, t, _ = mk_tm3.shape
    nb = t // MOBA_BLOCK
    nbp = kmean3.shape[1]
    return pl.pallas_call(
        functools.partial(_moba_prompt_kernel, nb=nb),
        grid=(b, nb),
        in_specs=[pl.BlockSpec((256, MOBA_BLOCK), lambda i, j: (1, i * nb + j)),
                  pl.BlockSpec((1, nbp, 256), lambda i, j: (i, 0, 0)),
                  pl.BlockSpec((1, t, 256), lambda i, j: (i, 0, 0)),
                  pl.BlockSpec((1, 256, t), lambda i, j: (i, 0, 0))],
        out_specs=pl.BlockSpec((256, MOBA_BLOCK), lambda i, j: (0, i * nb + j)),
        out_shape=jax.ShapeDtypeStruct((256, b * t), F32),
        scratch_shapes=[pltpu.VMEM((nbp, MOBA_HEADS * MOBA_BLOCK), F32),
                        pltpu.VMEM((MOBA_HEADS, HEAD_DIM, MOBA_BLOCK), F32)],
        compiler_params=pltpu.CompilerParams(dimension_semantics=("arbitrary", "arbitrary"),
                                             vmem_limit_bytes=VMEM_LIMIT),
        name="moba_prompt",
    )(qt, kmean3, mk_tm3, vt3)


def _moba_sample_kernel(ptab_ref, *refs, past, tq, pps):
    del ptab_ref
    pages = refs[:pps]
    q_ref, new_ref, o_ref, kt_ref, vt_ref, km_ref, e_ref = refs[pps:]
    j = pl.program_id(1)
    ppb = MOBA_BLOCK // PAGE_SIZE

    @pl.when((pl.program_id(0) == 0) & (j == 0))
    def _():
        e_ref[...] = (_iota(e_ref.shape, 0) == _iota(e_ref.shape, 1) // MOBA_BLOCK).astype(BF16)

    @pl.when(j == 0)
    def _():
        km_ref[...] = jnp.zeros_like(km_ref)

    lane = _iota(km_ref.shape, 1)
    for k in range(0, pps, ppb):
        ksum = jnp.zeros((256, PAGE_SIZE), F32)
        for kk in range(ppb):
            pg = pages[k + kk][0, 0]
            c0 = pl.multiple_of((j * pps + k + kk) * PAGE_SIZE, PAGE_SIZE)
            kt_ref[:, pl.ds(c0, PAGE_SIZE)] = pg[0].astype(BF16)
            vt_ref[:, pl.ds(c0, PAGE_SIZE)] = pg[1].astype(BF16)
            ksum = ksum + pg[0]
        col = jnp.sum(ksum, axis=1, keepdims=True) * (1.0 / MOBA_BLOCK)
        km_ref[...] = jnp.where(lane == j * (pps // ppb) + k // ppb, col, km_ref[...])

    @pl.when(j == pl.num_programs(1) - 1)
    def _():
        n_past = past // MOBA_BLOCK
        q = q_ref[0]
        rows = q.shape[0]
        gs = _dot_3pass(q, km_ref[...])
        n_idx = _iota(gs.shape, 1)
        gs = jnp.where(n_idx < n_past, gs, NEG)
        bias = jnp.where(_topk_mask(gs, n_past, MOBA_TOPK) & (n_idx < n_past), 0.0, NEG).astype(BF16)
        qb = (q * (HEAD_DIM ** -0.5)).astype(BF16)
        s = _dot(qb, kt_ref[...]) + _dot(bias, e_ref[...])
        own_ok = _iota((rows, PAGE_SIZE), 1) <= _iota((rows, PAGE_SIZE), 0) % tq
        s_own = jnp.where(own_ok, _dot(qb, new_ref[0, 0].astype(BF16)), NEG)
        m = jnp.maximum(jnp.max(s, axis=-1, keepdims=True), jnp.max(s_own, axis=-1, keepdims=True))
        p = jnp.exp(s - m)
        p_own = jnp.exp(s_own - m)
        l = jnp.sum(p, axis=-1, keepdims=True) + jnp.sum(p_own, axis=-1, keepdims=True)
        oh = (_dot_nt(p.astype(BF16), vt_ref[...]) + _dot_nt(p_own.astype(BF16), new_ref[0, 1].astype(BF16))) / l
        grp = _iota((tq, 256), 1) // HEAD_DIM
        out = oh[0:tq]
        for h in range(1, MOBA_HEADS):
            out = jnp.where(grp == h, oh[h * tq:(h + 1) * tq], out)
        o_ref[0] = out


def _moba_sample_call(cache5, layer, page_table, qbd, newt):
    b, n_pages = page_table.shape
    past = n_pages * PAGE_SIZE
    rows = qbd.shape[1]
    tq = rows // MOBA_HEADS
    pps = _pages_per_step(n_pages)
    assert past % MOBA_BLOCK == 0 and pps % (MOBA_BLOCK // PAGE_SIZE) == 0 and tq <= PAGE_SIZE
    assert past // MOBA_BLOCK <= LANES
    page_specs = [pl.BlockSpec((1, 1, 2, 256, PAGE_SIZE), functools.partial(
        lambda i, j, pt, k: (layer, pt[i, j * pps + k], 0, 0, 0), k=k)) for k in range(pps)]
    per_b = lambda a: pl.BlockSpec((1,) + a.shape[1:], lambda i, j, pt: (i,) + (0,) * (a.ndim - 1))
    grid_spec = pltpu.PrefetchScalarGridSpec(
        num_scalar_prefetch=1, grid=(b, n_pages // pps),
        in_specs=page_specs + [per_b(qbd), per_b(newt)],
        out_specs=pl.BlockSpec((1, tq, 256), lambda i, j, pt: (i, 0, 0)),
        scratch_shapes=[pltpu.VMEM((256, past), BF16), pltpu.VMEM((256, past), BF16),
                        pltpu.VMEM((256, LANES), F32), pltpu.VMEM((LANES, past), BF16)])
    return pl.pallas_call(
        functools.partial(_moba_sample_kernel, past=past, tq=tq, pps=pps),
        grid_spec=grid_spec, out_shape=jax.ShapeDtypeStruct((b, tq, 256), F32),
        compiler_params=pltpu.CompilerParams(dimension_semantics=("arbitrary", "arbitrary"),
                                             vmem_limit_bytes=VMEM_LIMIT),
        name="moba_sample",
    )(page_table, *([cache5] * pps), qbd, newt)


def _merge_kernel(x_ref, oab_ref, oct_ref, odt_ref, n1_ref, wgt_ref, wbr_ref, wo_ref, y_ref):
    x = x_ref[...]
    xn = (x * lax.rsqrt(jnp.mean(x * x, axis=-1, keepdims=True) + EPS) * n1_ref[...]).astype(BF16)
    branches = [oab_ref[:, 0:256], oab_ref[:, 256:512], oct_ref[...].T.astype(BF16), odt_ref[...].T.astype(BF16)]
    merged = jnp.zeros(x.shape, F32)
    for n in range(N_BRANCH):
        g = jax.nn.sigmoid(_dot_nt(xn, wgt_ref[n * D_MODEL:(n + 1) * D_MODEL, :]))
        merged = merged + g * _dot(branches[n], wbr_ref[n])
    y_ref[...] = x + _dot(merged.astype(BF16), wo_ref[...])


def _merge_call(x2d, oab, oct, odt, lw, *, tm):
    rows = x2d.shape[0]
    row = lambda w: pl.BlockSpec((tm, w), lambda i: (i, 0))
    col = lambda r: pl.BlockSpec((r, tm), lambda i: (0, i))
    full = lambda a: pl.BlockSpec(a.shape, lambda i: (0,) * a.ndim)
    ws = [lw['norm1'], lw['wg_t'], lw['w_branch'], lw['w_o']]
    return pl.pallas_call(
        _merge_kernel, grid=(rows // tm,),
        in_specs=[row(D_MODEL), row(512), col(256), col(256)] + [full(a) for a in ws],
        out_specs=row(D_MODEL), out_shape=jax.ShapeDtypeStruct((rows, D_MODEL), F32),
        compiler_params=pltpu.CompilerParams(dimension_semantics=("arbitrary",), vmem_limit_bytes=VMEM_LIMIT),
        name="merge",
    )(x2d, oab, oct, odt, *ws)


def _ffn_kernel(*refs, tiles_per_seq, seg):
    it = iter(refs)
    x_ref, n2_ref, wgate_ref, wup_ref, cw_ref, cb_ref, wdown_ref = (next(it) for _ in range(7))
    if seg is not None:
        h1_ref, h2_ref = next(it), next(it)
    y_ref, tail_ref = next(it), next(it)
    if seg is None:
        prev_ref = next(it)

    x = x_ref[...]
    tm = x.shape[0]
    h2 = (x * lax.rsqrt(jnp.mean(x * x, axis=-1, keepdims=True) + EPS) * n2_ref[...]).astype(BF16)
    gate = _dot(h2, wgate_ref[...])
    if seg is None:
        @pl.when(pl.program_id(0) % tiles_per_seq == 0)
        def _():
            prev_ref[...] = jnp.zeros_like(prev_ref)

        g1, g2 = _shift_rows(gate, prev_ref[SUBLANES - 1:SUBLANES, :], prev_ref[SUBLANES - 2:SUBLANES - 1, :], None)
        prev_ref[...] = gate[tm - SUBLANES:, :]
        tail_ref[0] = gate[tm - SUBLANES:, :]
    else:
        g1, g2 = _shift_rows(gate, h1_ref[...], h2_ref[...], seg)
        tail_ref[...] = gate.reshape(tail_ref.shape)
    ac = g2 * cw_ref[0:1, :] + cb_ref[...] + g1 * cw_ref[1:2, :] + gate * cw_ref[2:3, :]
    act = jax.nn.silu(ac) * _dot(h2, wup_ref[...])
    y_ref[...] = x + _dot(act.astype(BF16), wdown_ref[...])


def _ffn_call(x2d, lw, halos, *, tm, tiles_per_seq, n_seq, seg):
    rows = x2d.shape[0]
    row = lambda w: pl.BlockSpec((tm, w), lambda i: (i, 0))
    full = lambda a: pl.BlockSpec(a.shape, lambda i: (0,) * a.ndim)
    ws = [lw['norm2'], lw['w_gate'], lw['w_up'], lw['ffn_conv_w'], lw['ffn_conv_b'], lw['w_down']]
    ins = [x2d] + ws
    in_specs = [row(D_MODEL)] + [full(a) for a in ws]
    scratch = []
    if seg is None:
        tail_spec = pl.BlockSpec((1, SUBLANES, D_FF), lambda i: (i // tiles_per_seq, 0, 0))
        scratch.append(pltpu.VMEM((SUBLANES, D_FF), F32))
    else:
        ins += list(halos)
        in_specs += [row(D_FF), row(D_FF)]
        tail_spec = pl.BlockSpec((n_seq, SUBLANES, D_FF), lambda i: (0, 0, 0))
    return pl.pallas_call(
        functools.partial(_ffn_kernel, tiles_per_seq=tiles_per_seq, seg=seg),
        grid=(rows // tm,), in_specs=in_specs, out_specs=[row(D_MODEL), tail_spec],
        out_shape=[jax.ShapeDtypeStruct((rows, D_MODEL), F32), jax.ShapeDtypeStruct((n_seq, SUBLANES, D_FF), F32)],
        scratch_shapes=scratch,
        compiler_params=pltpu.CompilerParams(dimension_semantics=("arbitrary",), vmem_limit_bytes=VMEM_LIMIT),
        name="ffn_prompt" if seg is None else "ffn_sample",
    )(*ins)


def _rope_tables_t(pos):
    half = ROT_DIM // 2
    inv = jnp.power(jnp.float32(ROPE_THETA), -jnp.arange(half, dtype=F32) / half)
    ang = inv[:, None] * pos.astype(F32)[None, :]
    return jnp.cos(ang), jnp.sin(ang)


def _layer_weights(l, p, sample_tq, dec_batch):
    wt = jnp.transpose(p['w_in'], (2, 0, 1))[:, l, :]
    o_ck, o_cv, o_g = 1536, 1728, 1920
    o_dq = o_g + 3 * NSA_HEADS
    o_br = o_dq + 3 * 256
    hd = HEAD_DIM
    w_tm = jnp.concatenate([wt[0:1280], wt[o_g:o_g + 12], jnp.zeros((LANES - 12, D_MODEL), F32)], axis=0)
    w_tr = jnp.concatenate([
        wt[1280:1536], wt[o_dq:o_dq + 256],
        wt[o_ck:o_ck + hd], wt[o_cv:o_cv + hd], wt[o_ck + hd:o_ck + 2 * hd], wt[o_cv + hd:o_cv + 2 * hd],
        wt[o_ck + 2 * hd:o_ck + 3 * hd], wt[o_cv + 2 * hd:o_cv + 3 * hd],
        wt[o_dq + 256:o_dq + 768], wt[o_g:o_g + 12], jnp.zeros((2 * SUBLANES - 12, D_MODEL), F32)], axis=0)
    lw = {'w_tm': w_tm.astype(BF16), 'w_tr': w_tr.astype(BF16), 'wg_t': wt[o_br:].astype(BF16)}
    row = lambda a: a.reshape(1, -1)
    lw['norm1'] = row(p['norm1'][l])
    lw['norm2'] = row(p['norm2'][l])
    lw['conv_w'] = p['conv_w'][l]
    lw['conv_b'] = row(p['conv_b'][l])
    lw['ln_g'] = row(p['gmlp_ln_g'][l])
    lw['ln_b'] = row(p['gmlp_ln_b'][l])
    tril = jnp.tril(jnp.ones((GMLP_CHUNK, GMLP_CHUNK), bool))
    ws = jnp.where(tril[None], p['gmlp_ws'][l], 0.0)
    bs = p['gmlp_bs'][l]
    lw['mix_p'] = ws.reshape(GMLP_GROUPS * GMLP_CHUNK, GMLP_CHUNK).astype(BF16)
    lw['mixb_p'] = jnp.repeat(bs.T, HEAD_DIM, axis=1)
    eye = jnp.eye(dec_batch, dtype=F32)
    lw['mix_s'] = jnp.concatenate([jnp.kron(eye, ws[g, :sample_tq, :sample_tq]) for g in range(GMLP_GROUPS)],
                                  axis=0).astype(BF16)
    lw['mixb_s'] = jnp.tile(jnp.repeat(bs.T[:sample_tq], HEAD_DIM, axis=1), (dec_batch, 1))
    kg = p['nsa_k_gain'][l]
    one = jnp.ones((hd,), F32)
    gains = jnp.concatenate([jnp.tile(p['nsa_q_gain'][l], NSA_HEADS), jnp.tile(p['moba_q_gain'][l], MOBA_HEADS),
                             kg[0], one, kg[1], one, kg[2], one, jnp.tile(p['moba_k_gain'][l], MOBA_HEADS)])
    lw['gain_t'] = jnp.broadcast_to(gains[:, None], (R_MV, LANES))
    w1c = p['nsa_cmp_w1'][l].reshape(2, NSA_CMP_LEN, HEAD_DIM, NSA_CMP_HIDDEN)
    pos = p['nsa_cmp_pos'][l]
    zero = jnp.zeros((HEAD_DIM, NSA_CMP_HIDDEN), F32)

    def pair_w(r_off):
        mats = []
        for pp in range(NSA_CMP_STRIDE // 2):
            blocks = []
            for r in (2 * pp, 2 * pp + 1):
                blocks.append(jnp.concatenate([w1c[0, r + r_off], zero], axis=1))
                blocks.append(jnp.concatenate([zero, w1c[1, r + r_off]], axis=1))
            mats.append(jnp.concatenate(blocks, axis=0))
        return jnp.stack(mats).astype(BF16)

    def pair_pos(r_off):
        return jnp.stack([jnp.concatenate([pos[0, 2 * pp + r_off], pos[1, 2 * pp + r_off],
                                           pos[0, 2 * pp + 1 + r_off], pos[1, 2 * pp + 1 + r_off]])
                          for pp in range(NSA_CMP_STRIDE // 2)])

    lw['cmp_wa'], lw['cmp_wb'] = pair_w(0), pair_w(NSA_CMP_STRIDE)
    lw['cmp_pt'], lw['cmp_pb'] = pair_pos(0), pair_pos(NSA_CMP_STRIDE)
    w2 = p['nsa_cmp_w2'][l]
    z2 = jnp.zeros((NSA_CMP_HIDDEN, HEAD_DIM), F32)
    lw['cmp_w2'] = jnp.concatenate([jnp.concatenate([w2[0], z2], axis=1),
                                    jnp.concatenate([z2, w2[1]], axis=1)], axis=0).astype(BF16)
    lw['w_branch'] = p['w_branch'][l].astype(BF16)
    lw['w_o'] = p['w_o'][l].astype(BF16)
    lw['w_gate'] = p['w_ffn_gate'][l].astype(BF16)
    lw['w_up'] = p['w_ffn_up'][l].astype(BF16)
    lw['w_down'] = p['w_ffn_down'][l].astype(BF16)
    lw['ffn_conv_w'] = p['ffn_conv_w'][l]
    lw['ffn_conv_b'] = row(p['ffn_conv_b'][l])
    return lw


def _halos(buf, tq):
    b, _, c = buf.shape
    z = jnp.zeros((b, tq - 1, c), F32)
    h1 = jnp.concatenate([buf[:, 1:2], z], axis=1)
    h2 = jnp.concatenate([buf[:, 0:2], z[:, 1:]], axis=1)
    return h1.reshape(b * tq, c), h2.reshape(b * tq, c)


def _row_tile(rows, cap):
    tm = cap
    while rows % tm:
        tm //= 2
    return tm


def _per_seq_pages(xt, b, tq):
    c = xt.shape[0]
    x = xt.reshape(c, b, tq).transpose(1, 0, 2)
    return jnp.concatenate([x, jnp.zeros((b, c, PAGE_SIZE - tq), F32)], axis=-1)


def _prompt_layer(x2d, lw, tables, b, t):
    tm = _row_tile(t, 512)
    assert tm % MOBA_BLOCK == 0 and tm % SLC_TILE == 0
    tps = t // tm
    pw = dict(lw, mix=lw['mix_p'], mixb=lw['mixb_p'])
    (oab, qt, nsat, wint, mobat, gt, _, ztail, cmp_tm, slc_tm, win_tm, mk_tm, vt16, kmean) = _proj_call(
        x2d, pw, tables, None, tm=tm, tiles_per_seq=tps, n_seq=b, chunk=GMLP_CHUNK, seg=None)
    kcvc, kcvct = _compress_prompt_call(cmp_tm.reshape(b, t, LANES), lw)
    o_ct = _nsa_prompt_call(qt, gt, kcvc, kcvct, slc_tm.reshape(b, t, LANES), win_tm.reshape(b, t, LANES), vt16)
    nb = t // MOBA_BLOCK
    kmean3 = kmean[:, 0:tm // MOBA_BLOCK].reshape(b, nb, 256)
    nbp = -(-nb // SUBLANES) * SUBLANES
    if nbp != nb:
        kmean3 = jnp.concatenate([kmean3, jnp.zeros((b, nbp - nb, 256), F32)], axis=1)
    o_dt = _moba_prompt_call(qt, kmean3, mk_tm.reshape(b, t, 256), vt16)
    x_mid = _merge_call(x2d, oab, o_ct, o_dt, lw, tm=tm)
    y, ftail = _ffn_call(x_mid, lw, None, tm=tm, tiles_per_seq=tps, n_seq=b, seg=None)
    hd = HEAD_DIM
    wlen = min(NSA_WINDOW, t)
    state = (nsat.reshape(b, 4, hd, t).transpose(0, 3, 1, 2),
             mobat.reshape(b, 2, MOBA_HEADS, hd, t).transpose(0, 4, 1, 2, 3),
             wint[:, :, t - wlen:].reshape(b, 2, hd, wlen).transpose(0, 3, 1, 2),
             ztail[:, SUBLANES - 2:], ftail[:, SUBLANES - 2:])
    return y, state


def _sample_layer(x2d, lw, tables, b, tq, layer, nsa_cache5, moba_cache5, page_table, win_state5, conv_state,
                  ffn_state):
    rows = b * tq
    hd = HEAD_DIM
    pw = dict(lw, mix=lw['mix_s'], mixb=lw['mixb_s'])
    oab, qt, nsat, wint, mobat, _, gtm, ztail, vn = _proj_call(
        x2d, pw, tables, _halos(conv_state, tq), tm=rows, tiles_per_seq=1, n_seq=b, chunk=rows, seg=tq)
    nsat, wint, mobat = nsat[0], wint[0], mobat[0]
    kcvct = _compress_paged_call(nsa_cache5, layer, page_table, lw)
    q32 = qt[0:256].reshape(NSA_HEADS, hd, b, tq).transpose(2, 0, 3, 1).reshape(b, NSA_HEADS * tq, hd)
    g4 = gtm[:, 0:3 * NSA_HEADS].reshape(b, tq, NSA_HEADS, 3).transpose(0, 2, 1, 3).reshape(b, NSA_HEADS * tq, 3)
    g32 = jnp.concatenate([g4, jnp.zeros((b, NSA_HEADS * tq, LANES - 3), F32)], axis=-1)
    newt = _per_seq_pages(nsat[2 * hd:4 * hd], b, tq).reshape(b, 2, hd, PAGE_SIZE)
    wnewt = _per_seq_pages(wint, b, tq).reshape(b, 2, hd, PAGE_SIZE)
    o32 = _nsa_sample_call(nsa_cache5, layer, page_table, q32, g32, kcvct, newt, win_state5, wnewt)
    o_ct = o32.reshape(b, NSA_HEADS, tq, hd).transpose(1, 3, 0, 2).reshape(256, rows)
    qm = qt[256:512].reshape(MOBA_HEADS, hd, b, tq).transpose(2, 0, 3, 1)
    eye = jnp.eye(MOBA_HEADS, dtype=F32)
    qbd = (qm[:, :, :, None, :] * eye[None, :, None, :, None]).reshape(b, MOBA_HEADS * tq, 256)
    mnewt = _per_seq_pages(mobat, b, tq).reshape(b, 2, 256, PAGE_SIZE)
    o_dt = _moba_sample_call(moba_cache5, layer, page_table, qbd, mnewt).transpose(2, 0, 1).reshape(256, rows)
    x_mid = _merge_call(x2d, oab, o_ct, o_dt, lw, tm=rows)
    y, ftail = _ffn_call(x_mid, lw, _halos(ffn_state, tq), tm=rows, tiles_per_seq=1, n_seq=b, seg=tq)
    wnew = wint.reshape(2, hd, b, tq).transpose(2, 0, 1, 3)
    win_all = jnp.concatenate([win_state5[layer], wnew], axis=-1)
    win_new = win_all[..., win_all.shape[-1] - NSA_WINDOW:].transpose(0, 3, 1, 2)
    state = (nsat.T.reshape(b, tq, 4, hd), mobat.T.reshape(b, tq, 2, MOBA_HEADS, hd), win_new,
             ztail[:, tq - 2:tq], ftail[:, tq - 2:tq], vn.reshape(b, tq, 256))
    return y, state


def kernel(x_prompt, x_sample, cache_nsa_kv, cache_moba_kv, state_nsa_win, state_conv, state_ffn_conv, page_table, norm1, w_in, conv_w, conv_b, gmlp_ln_g, gmlp_ln_b, gmlp_ws, gmlp_bs, nsa_q_gain, nsa_k_gain, nsa_cmp_pos, nsa_cmp_w1, nsa_cmp_w2, moba_q_gain, moba_k_gain, w_branch, w_o, norm2, w_ffn_gate, w_ffn_up, ffn_conv_w, ffn_conv_b, w_ffn_down):
    params = dict(norm1=norm1, w_in=w_in, conv_w=conv_w, conv_b=conv_b, gmlp_ln_g=gmlp_ln_g, gmlp_ln_b=gmlp_ln_b,
                  gmlp_ws=gmlp_ws, gmlp_bs=gmlp_bs, nsa_q_gain=nsa_q_gain, nsa_k_gain=nsa_k_gain,
                  nsa_cmp_pos=nsa_cmp_pos, nsa_cmp_w1=nsa_cmp_w1, nsa_cmp_w2=nsa_cmp_w2, moba_q_gain=moba_q_gain,
                  moba_k_gain=moba_k_gain, w_branch=w_branch, w_o=w_o, norm2=norm2, w_ffn_gate=w_ffn_gate,
                  w_ffn_up=w_ffn_up, ffn_conv_w=ffn_conv_w, ffn_conv_b=ffn_conv_b, w_ffn_down=w_ffn_down)
    bp, tp, _ = x_prompt.shape
    bs, ts, _ = x_sample.shape
    depth = w_in.shape[0]
    n_pool = cache_nsa_kv.shape[1]
    past = page_table.shape[1] * PAGE_SIZE
    assert ts == SUBLANES and state_nsa_win.shape[2] == NSA_WINDOW
    nsa_cache5 = jnp.transpose(cache_nsa_kv, (0, 1, 3, 4, 2))
    moba_cache5 = jnp.transpose(cache_moba_kv, (0, 1, 3, 4, 5, 2)).reshape(depth, n_pool, 2, 256, PAGE_SIZE)
    win_state5 = jnp.transpose(state_nsa_win, (0, 1, 3, 4, 2))
    tab_p = _rope_tables_t(jnp.arange(tp, dtype=jnp.int32))
    tab_s = tuple(jnp.tile(a, (1, bs)) for a in _rope_tables_t(past + jnp.arange(ts, dtype=jnp.int32)))
    yp = x_prompt.reshape(bp * tp, D_MODEL)
    ys = x_sample.reshape(bs * ts, D_MODEL)
    sp, ss = [], []
    for l in range(depth):
        lw = _layer_weights(l, params, ts, bs)
        yp, st_p = _prompt_layer(yp, lw, tab_p, bp, tp)
        ys, st_s = _sample_layer(ys, lw, tab_s, bs, ts, l, nsa_cache5, moba_cache5, page_table, win_state5,
                                 state_conv[l], state_ffn_conv[l])
        sp.append(st_p)
        ss.append(st_s)
    stack = lambda lst, k: jnp.stack([s[k] for s in lst])
    return (yp.reshape(bp, tp, D_MODEL), ys.reshape(bs, ts, D_MODEL),
            stack(sp, 0), stack(ss, 0), stack(sp, 1), stack(ss, 1), stack(sp, 2), stack(ss, 2),
            stack(sp, 3), stack(ss, 3), stack(sp, 4), stack(ss, 4), stack(ss, 5))
```

```python
import functools

import jax
import jax.numpy as jnp
from jax import lax
from jax.experimental import pallas as pl
from jax.experimental.pallas import tpu as pltpu

F32 = jnp.float32
BF16 = jnp.bfloat16

D_MODEL = 1024
HEAD_DIM = 64
ROT_DIM = HEAD_DIM // 4
ROPE_THETA = 500000.0
PAGE_SIZE = 128
BRANCH_CH = 256
N_BRANCH = 4
GMLP_GROUPS = 4
GMLP_CHUNK = 128
NSA_HEADS = 4
NSA_CMP_LEN = 32
NSA_CMP_STRIDE = 16
NSA_CMP_HIDDEN = 256
NSA_SLC_BLOCK = 64
NSA_TOPN = 16
NSA_WINDOW = 512
MOBA_HEADS = 4
MOBA_BLOCK = 256
MOBA_TOPK = 3
D_FF = 2816
EPS = 1e-6
NEG = -1e30
BIG = 1e30

SLC_TILE = 512
LANES = 128
SUBLANES = 8
VMEM_LIMIT = 56 * 1024 * 1024

C_AH, C_AB, C_AC, C_U, C_V, C_G, C_END = 0, 256, 512, 768, 1024, 1280, 1408
R_Q, R_NSA, R_WIN, R_MK, R_MV, R_G, R_END = 0, 512, 768, 896, 1152, 1408, 1424
NORMED_GROUPS = (0, 1, 2, 3, 4, 5, 6, 7, 8, 10, 12, 14, 15, 16, 17)

_NT = (((1,), (1,)), ((), ()))


def _dot(a, b):
    return jnp.dot(a, b, preferred_element_type=F32)


def _dot_nt(a, b):
    return lax.dot_general(a, b, _NT, preferred_element_type=F32)


def _split(a):
    hi = a.astype(BF16)
    lo = (a - hi.astype(F32)).astype(BF16)
    return hi, lo


def _dot_3pass(a, b):
    ah, al = _split(a)
    bh, bl = _split(b)
    return _dot(ah, bh) + _dot(ah, bl) + _dot(al, bh)


def _iota(shape, dim):
    return lax.broadcasted_iota(jnp.int32, shape, dim)


def _lane_tile(a, n):
    return a if n == 1 else jnp.concatenate([a] * n, axis=1)


def _norm_rope_t(xh, gain, cos, sin):
    ss = jnp.sum(xh * xh, axis=0, keepdims=True)
    y = xh * lax.rsqrt(ss * (1.0 / HEAD_DIM) + EPS) * gain
    half = ROT_DIM // 2
    y0, y1 = y[0:half], y[half:ROT_DIM]
    return jnp.concatenate([y0 * cos - y1 * sin, y1 * cos + y0 * sin, y[ROT_DIM:]], axis=0)


def _shift_rows(z, prev1, prev2, seg):
    rows = _iota(z.shape, 0)
    z1 = pltpu.roll(z, 1, 0)
    z2 = pltpu.roll(z, 2, 0)
    if seg is None:
        z1 = jnp.where(rows == 0, prev1, z1)
        z2 = jnp.where(rows == 0, prev2, jnp.where(rows == 1, prev1, z2))
    else:
        t = rows % seg
        z1 = jnp.where(t == 0, prev1, z1)
        z2 = jnp.where(t <= 1, prev2, z2)
    return z1, z2


def _proj_kernel(*refs, tiles_per_seq, chunk, seg):
    it = iter(refs)
    x_ref, n1_ref, wtm_ref, wtr_ref, cw_ref, cb_ref, lng_ref, lnb_ref, mix_ref, mixb_ref = (next(it) for _ in range(10))
    gain_ref, cos_ref, sin_ref = (next(it) for _ in range(3))
    if seg is not None:
        h1_ref, h2_ref = next(it), next(it)
    oab_ref, qt_ref, nsat_ref, wint_ref, mobat_ref, gt_ref, gtm_ref, ztail_ref = (next(it) for _ in range(8))
    if seg is not None:
        vn_ref = next(it)
    else:
        cmp_ref, slc_ref, win_ref, mk_ref, vt_ref, kmean_ref, zprev_ref = (next(it) for _ in range(7))

    x = x_ref[...]
    tm = x.shape[0]
    xn = (x * lax.rsqrt(jnp.mean(x * x, axis=-1, keepdims=True) + EPS) * n1_ref[...]).astype(BF16)
    proj = _dot_nt(xn, wtm_ref[...])
    projt = _dot_nt(wtr_ref[...], xn)

    z = proj[:, C_AC:C_AC + 256] * proj[:, C_AH:C_AH + 256]
    if seg is None:
        @pl.when(pl.program_id(0) % tiles_per_seq == 0)
        def _():
            zprev_ref[...] = jnp.zeros_like(zprev_ref)

        z1, z2 = _shift_rows(z, zprev_ref[SUBLANES - 1:SUBLANES, :], zprev_ref[SUBLANES - 2:SUBLANES - 1, :], None)
        zprev_ref[...] = z[tm - SUBLANES:, :]
        ztail_ref[0] = z[tm - SUBLANES:, :]
    else:
        z1, z2 = _shift_rows(z, h1_ref[...], h2_ref[...], seg)
        ztail_ref[...] = z.reshape(ztail_ref.shape)
    ya = z2 * cw_ref[0:1, :] + cb_ref[...] + z1 * cw_ref[1:2, :] + z * cw_ref[2:3, :]
    oab_ref[:, 0:256] = (proj[:, C_AB:C_AB + 256] * ya).astype(BF16)

    u = jax.nn.gelu(proj[:, C_U:C_U + 256])
    v = jax.nn.gelu(proj[:, C_V:C_V + 256])
    vc = v - jnp.mean(v, axis=-1, keepdims=True)
    vn = vc * lax.rsqrt(jnp.mean(vc * vc, axis=-1, keepdims=True) + EPS) * lng_ref[...] + lnb_ref[...]
    if seg is not None:
        vn_ref[...] = vn
    vnb = vn.astype(BF16)
    grp = _iota((chunk, 256), 1) // HEAD_DIM
    for c in range(tm // chunk):
        r = _dot(mix_ref[...], vnb[c * chunk:(c + 1) * chunk, :])
        s = r[0:chunk]
        for g in range(1, GMLP_GROUPS):
            s = jnp.where(grp == g, r[g * chunk:(g + 1) * chunk], s)
        s = s + mixb_ref[...]
        oab_ref[c * chunk:(c + 1) * chunk, 256:512] = (u[c * chunk:(c + 1) * chunk, :] * s).astype(BF16)

    gtm_ref[...] = jax.nn.sigmoid(proj[:, C_G:C_END])
    gt_ref[...] = jax.nn.sigmoid(projt[R_G:R_END, :])

    cos, sin = cos_ref[...], sin_ref[...]
    rep = tm // LANES
    groups = []
    for g in range(R_MV // HEAD_DIM):
        xh = projt[g * HEAD_DIM:(g + 1) * HEAD_DIM, :]
        if g in NORMED_GROUPS:
            xh = _norm_rope_t(xh, _lane_tile(gain_ref[g * HEAD_DIM:(g + 1) * HEAD_DIM, :], rep), cos, sin)
        groups.append(xh)
    qkv = jnp.concatenate(groups, axis=0)
    qt_ref[...] = qkv[R_Q:R_NSA]
    nsat_ref[0] = qkv[R_NSA:R_WIN]
    wint_ref[0] = qkv[R_WIN:R_MK]
    mobat_ref[0, 0:256, :] = qkv[R_MK:R_MV]
    vmt = projt[R_MV:R_G, :]
    mobat_ref[0, 256:512, :] = vmt

    if seg is None:
        nsa_tm = qkv[R_NSA:R_WIN].T
        cmp_ref[...] = nsa_tm[:, 0:LANES]
        lane = _iota((tm, LANES), 1) - HEAD_DIM
        blk_in_tile = (_iota((tm, LANES), 0) // NSA_SLC_BLOCK) % (SLC_TILE // NSA_SLC_BLOCK)
        slc_ref[...] = jnp.where(lane < 0, nsa_tm[:, LANES:2 * LANES],
                                 jnp.where(lane == blk_in_tile, 1.0, 0.0)).astype(BF16)
        win_ref[...] = qkv[R_WIN:R_MK].T.astype(BF16)
        mk_tm = qkv[R_MK:R_MV].T
        mk_ref[...] = mk_tm.astype(BF16)
        vt_ref[0, 0:256, :] = vmt.astype(BF16)
        vt_ref[0, 256:320, :] = qkv[R_NSA + 192:R_NSA + 256].astype(BF16)
        vt_ref[0, 320:384, :] = qkv[R_WIN + 64:R_WIN + 128].astype(BF16)
        kmean_ref[...] = jnp.zeros_like(kmean_ref)
        for r in range(tm // MOBA_BLOCK):
            kmean_ref[0, r:r + 1, :] = jnp.sum(mk_tm[r * MOBA_BLOCK:(r + 1) * MOBA_BLOCK, :], axis=0,
                                               keepdims=True) * (1.0 / MOBA_BLOCK)


def _proj_call(x2d, lw, tables, halos, *, tm, tiles_per_seq, n_seq, chunk, seg):
    rows = x2d.shape[0]
    nt = rows // tm
    tps = tiles_per_seq
    n_out_seq = n_seq if seg is None else 1
    t_out = rows // n_out_seq
    row = lambda w: pl.BlockSpec((tm, w), lambda i: (i, 0))
    col = lambda r: pl.BlockSpec((r, tm), lambda i: (0, i))
    seq3 = lambda r: pl.BlockSpec((1, r, tm), lambda i: (i // tps, 0, i % tps))
    full = lambda a: pl.BlockSpec(a.shape, lambda i: (0,) * a.ndim)
    n_tab = tables[0].shape[1] // tm
    tab = pl.BlockSpec((ROT_DIM // 2, tm), lambda i: (0, i % n_tab))
    ws = [lw['norm1'], lw['w_tm'], lw['w_tr'], lw['conv_w'], lw['conv_b'], lw['ln_g'], lw['ln_b'], lw['mix'],
          lw['mixb'], lw['gain_t']]
    ins = [x2d] + ws + list(tables)
    in_specs = [row(D_MODEL)] + [full(a) for a in ws] + [tab] * 2
    sds = jax.ShapeDtypeStruct
    out_shape = [sds((rows, 512), BF16), sds((512, rows), F32), sds((n_out_seq, 256, t_out), F32),
                 sds((n_out_seq, 128, t_out), F32), sds((n_out_seq, 512, t_out), F32),
                 sds((2 * SUBLANES, rows), F32), sds((rows, LANES), F32), sds((n_seq, SUBLANES, 256), F32)]
    out_specs = [row(512), col(512), seq3(256), seq3(128), seq3(512), col(2 * SUBLANES), row(LANES)]
    scratch = []
    if seg is None:
        out_specs.append(pl.BlockSpec((1, SUBLANES, 256), lambda i: (i // tps, 0, 0)))
        out_shape += [sds((rows, LANES), F32), sds((rows, LANES), BF16), sds((rows, LANES), BF16),
                      sds((rows, 256), BF16), sds((n_seq, 384, t_out), BF16), sds((nt, SUBLANES, 256), F32)]
        out_specs += [row(LANES), row(LANES), row(LANES), row(256), seq3(384),
                      pl.BlockSpec((1, SUBLANES, 256), lambda i: (i, 0, 0))]
        scratch.append(pltpu.VMEM((SUBLANES, 256), F32))
    else:
        ins += list(halos)
        in_specs += [row(256), row(256)]
        out_specs.append(pl.BlockSpec((n_seq, SUBLANES, 256), lambda i: (0, 0, 0)))
        out_shape.append(sds((rows, 256), F32))
        out_specs.append(row(256))
    return pl.pallas_call(
        functools.partial(_proj_kernel, tiles_per_seq=tps, chunk=chunk, seg=seg),
        grid=(nt,), in_specs=in_specs, out_specs=out_specs, out_shape=out_shape, scratch_shapes=scratch,
        compiler_params=pltpu.CompilerParams(dimension_semantics=("arbitrary",), vmem_limit_bytes=VMEM_LIMIT),
        name="proj_prompt" if seg is None else "proj_sample",
    )(*ins)


def _compress_compute(src_ref, wa_ref, wb_ref, pt_ref, pb_ref, w2_ref, out_ref, outt_ref, groups):
    acc_a = jnp.zeros((groups, 2 * NSA_CMP_HIDDEN), F32)
    acc_b = jnp.zeros((groups, 2 * NSA_CMP_HIDDEN), F32)
    for p in range(NSA_CMP_STRIDE // 2):
        xp = jnp.concatenate([src_ref[pl.ds(2 * p, groups, stride=NSA_CMP_STRIDE), :],
                              src_ref[pl.ds(2 * p + 1, groups, stride=NSA_CMP_STRIDE), :]], axis=1)
        acc_a = acc_a + _dot((xp + pt_ref[p:p + 1, :]).astype(BF16), wa_ref[p])
        acc_b = acc_b + _dot((xp + pb_ref[p:p + 1, :]).astype(BF16), wb_ref[p])
    hdn = jax.nn.gelu(acc_a + pltpu.roll(acc_b, groups - 1, 0))
    out = _dot(hdn.astype(BF16), w2_ref[...])
    if out_ref is not None:
        out_ref[0] = out
    outt_ref[0] = out.T


def _compress_prompt_kernel(src_ref, wa_ref, wb_ref, pt_ref, pb_ref, w2_ref, out_ref, outt_ref, *, groups):
    _compress_compute(src_ref.at[0], wa_ref, wb_ref, pt_ref, pb_ref, w2_ref, out_ref, outt_ref, groups)


def _compress_paged_kernel(ptab_ref, *refs, groups, pps):
    del ptab_ref
    pages = refs[:pps]
    wa_ref, wb_ref, pt_ref, pb_ref, w2_ref, outt_ref, stage_ref = refs[pps:]
    j = pl.program_id(1)
    for k in range(pps):
        pg = pages[k][0, 0].reshape(2 * HEAD_DIM, PAGE_SIZE)
        stage_ref[pl.ds(pl.multiple_of((j * pps + k) * PAGE_SIZE, PAGE_SIZE), PAGE_SIZE), :] = pg.T

    @pl.when(j == pl.num_programs(1) - 1)
    def _():
        _compress_compute(stage_ref, wa_ref, wb_ref, pt_ref, pb_ref, w2_ref, None, outt_ref, groups)


def _compress_prompt_call(cmp_tm3, lw):
    b, t, _ = cmp_tm3.shape
    groups = t // NSA_CMP_STRIDE
    full = lambda a: pl.BlockSpec(a.shape, lambda i: (0,) * a.ndim)
    ws = [lw['cmp_wa'], lw['cmp_wb'], lw['cmp_pt'], lw['cmp_pb'], lw['cmp_w2']]
    return pl.pallas_call(
        functools.partial(_compress_prompt_kernel, groups=groups),
        grid=(b,),
        in_specs=[pl.BlockSpec((1, t, LANES), lambda i: (i, 0, 0))] + [full(a) for a in ws],
        out_specs=[pl.BlockSpec((1, groups, LANES), lambda i: (i, 0, 0)),
                   pl.BlockSpec((1, LANES, groups), lambda i: (i, 0, 0))],
        out_shape=[jax.ShapeDtypeStruct((b, groups, LANES), F32), jax.ShapeDtypeStruct((b, LANES, groups), F32)],
        compiler_params=pltpu.CompilerParams(dimension_semantics=("arbitrary",), vmem_limit_bytes=VMEM_LIMIT),
        name="compress_prompt",
    )(cmp_tm3, *ws)


def _pages_per_step(n_pages, cap):
    pps = cap
    while n_pages % pps:
        pps //= 2
    return pps


def _compress_paged_call(cache5, layer, page_table, lw):
    b, n_pages = page_table.shape
    past = n_pages * PAGE_SIZE
    groups = past // NSA_CMP_STRIDE
    pps = _pages_per_step(n_pages, 64)
    full = lambda a: pl.BlockSpec(a.shape, lambda i, j, pt: (0,) * a.ndim)
    ws = [lw['cmp_wa'], lw['cmp_wb'], lw['cmp_pt'], lw['cmp_pb'], lw['cmp_w2']]
    page_specs = [pl.BlockSpec((1, 1, 2, HEAD_DIM, PAGE_SIZE), functools.partial(
        lambda i, j, pt, k: (layer, pt[i, j * pps + k], 0, 0, 0), k=k)) for k in range(pps)]
    grid_spec = pltpu.PrefetchScalarGridSpec(
        num_scalar_prefetch=1, grid=(b, n_pages // pps),
        in_specs=page_specs + [full(a) for a in ws],
        out_specs=pl.BlockSpec((1, LANES, groups), lambda i, j, pt: (i, 0, 0)),
        scratch_shapes=[pltpu.VMEM((past, LANES), F32)])
    return pl.pallas_call(
        functools.partial(_compress_paged_kernel, groups=groups, pps=pps),
        grid_spec=grid_spec, out_shape=jax.ShapeDtypeStruct((b, LANES, groups), F32),
        compiler_params=pltpu.CompilerParams(dimension_semantics=("arbitrary", "arbitrary"),
                                             vmem_limit_bytes=VMEM_LIMIT),
        name="compress_sample",
    )(page_table, *([cache5] * pps), *ws)


def _topk_mask(score, n_cols, k):
    idx = _iota(score.shape, 1)
    rank = jnp.zeros(score.shape, jnp.int32)
    for j in range(n_cols):
        col = score[:, j:j + 1]
        rank = rank + jnp.where(col > score, 1, jnp.where(col == score, jnp.where(idx > j, 1, 0), 0))
    return rank < k


def _topk_mask_t(score, n_rows, k):
    n_tiles = score.shape[0] // SUBLANES
    tiles = [score[r * SUBLANES:(r + 1) * SUBLANES, :] for r in range(n_tiles)]
    idx = _iota(tiles[0].shape, 0)
    ranks = [jnp.zeros(tiles[0].shape, jnp.int32) for _ in range(n_tiles)]
    for j in range(n_rows):
        row = score[j:j + 1, :]
        for r in range(n_tiles):
            if r * SUBLANES > j:
                ranks[r] = jnp.where(row >= tiles[r], ranks[r] + 1, ranks[r])
            elif (r + 1) * SUBLANES - 1 <= j:
                ranks[r] = jnp.where(row > tiles[r], ranks[r] + 1, ranks[r])
            else:
                ranks[r] = ranks[r] + jnp.where(idx + r * SUBLANES > j, jnp.where(row >= tiles[r], 1, 0),
                                                jnp.where(row > tiles[r], 1, 0))
    return jnp.concatenate(ranks, axis=0) < k


def _softmax_axis(s, ok, axis):
    s = jnp.where(ok, s, NEG)
    m = jnp.max(s, axis=axis, keepdims=True)
    p = jnp.where(ok, jnp.exp(s - m), 0.0)
    l = jnp.sum(p, axis=axis, keepdims=True)
    return p / jnp.where(l > 0.0, l, 1.0)


def _forced_importance(imp, blk, cur):
    forced = (blk == 0) | (blk == cur) | (blk == cur - 1)
    imp = jnp.where(forced, BIG, imp)
    return jnp.where(blk > cur, NEG, imp)


def _nsa_prompt_kernel(qt_ref, gt_ref, kcvc_ref, kcvct_ref, slc_ref, win_ref, vt_ref, o_ref, sel_ref, s_ref, *, t, qb, tk):
    iq = pl.program_id(1)
    s0 = iq * qb
    nq = NSA_HEADS * qb
    qt = qt_ref[...] * (HEAD_DIM ** -0.5)
    qst = jnp.concatenate([qt[h * HEAD_DIM:(h + 1) * HEAD_DIM, :] for h in range(NSA_HEADS)], axis=1)
    qst = jnp.concatenate([qst, jnp.zeros_like(qst)], axis=0).astype(BF16)
    qpos_q = s0 + _iota((1, qb), 1)
    qpos = _lane_tile(qpos_q, NSA_HEADS)

    wlen = NSA_WINDOW + qb
    start = pl.multiple_of(jnp.maximum(s0 - NSA_WINDOW, 0), qb)
    sw = _dot(win_ref[0, pl.ds(start, wlen), :], qst)
    back = (qpos - start) - _iota(sw.shape, 0)
    sw = jnp.where((back & -NSA_WINDOW) == 0, sw, NEG)
    ew = jnp.exp(sw - jnp.max(sw, axis=0, keepdims=True))
    o_win = (_dot(vt_ref[0, HEAD_DIM:2 * HEAD_DIM, pl.ds(start, wlen)], ew.astype(BF16))
             / jnp.sum(ew, axis=0, keepdims=True))

    n_cmp = (t - NSA_CMP_LEN) // NSA_CMP_STRIDE + 1
    n_blk = t // NSA_SLC_BLOCK
    kcvc = kcvc_ref[0].astype(BF16)
    g = kcvc.shape[0]
    s = _dot(kcvc, qst)
    n_idx = _iota(s.shape, 0)
    ok = (n_idx * NSA_CMP_STRIDE + (NSA_CMP_LEN - 1) <= qpos) & (n_idx < n_cmp)
    s = jnp.where(ok, s, NEG)
    m = jnp.max(s, axis=0, keepdims=True)
    e = jnp.exp(s - m)
    p = e * jnp.where(m > 0.5 * NEG, 1.0 / jnp.sum(e, axis=0, keepdims=True), 0.0)
    o_cmp = _dot(kcvct_ref[0, HEAD_DIM:2 * HEAD_DIM, :].astype(BF16), p.astype(BF16))
    psum = p[:, 0:qb]
    for h in range(1, NSA_HEADS):
        psum = psum + p[:, h * qb:(h + 1) * qb]
    cj = _iota((n_blk, g), 0)
    cn = _iota((n_blk, g), 1)
    cover = ((cn * NSA_CMP_STRIDE <= cj * NSA_SLC_BLOCK + (NSA_SLC_BLOCK - 1))
             & (cn * NSA_CMP_STRIDE + (NSA_CMP_LEN - 1) >= cj * NSA_SLC_BLOCK) & (cn < n_cmp)).astype(BF16)
    p_hi, p_lo = _split(psum)
    imp = _dot(cover, p_hi) + _dot(cover, p_lo)
    imp = _forced_importance(imp, _iota(imp.shape, 0), qpos_q // NSA_SLC_BLOCK)
    cur = qpos_q // NSA_SLC_BLOCK
    picked = _topk_mask_t(imp, n_blk, NSA_TOPN) & (_iota(imp.shape, 0) <= cur)
    sel_ref[...] = _lane_tile(jnp.where(picked, 0.0, NEG), NSA_HEADS)

    bpt = tk // NSA_SLC_BLOCK
    q64 = qst[0:HEAD_DIM].astype(F32)
    zpad = jnp.zeros((LANES - HEAD_DIM - bpt, nq), F32)
    kd = s0 // tk

    def scores(kt):
        k0 = pl.multiple_of(kt * tk, tk)
        slab = sel_ref[pl.ds(pl.multiple_of(kt * bpt, bpt), bpt), :]
        w = jnp.concatenate([q64, slab, zpad], axis=0).astype(BF16)
        return _dot(slc_ref[0, pl.ds(k0, tk), :], w)

    def fold(x, op):
        return op(x.reshape(tk // SUBLANES, SUBLANES, nq), axis=0)

    def sweep1(kt, m8):
        sc = scores(kt)
        s_ref[kt] = sc
        return jnp.maximum(m8, fold(sc, jnp.max))

    m8 = lax.fori_loop(0, kd, sweep1, jnp.full((SUBLANES, nq), NEG, F32))
    sc = scores(kd)
    sc = jnp.where(kd * tk + _iota(sc.shape, 0) <= qpos, sc, NEG)
    s_ref[kd] = sc
    m = jnp.max(jnp.maximum(m8, fold(sc, jnp.max)), axis=0, keepdims=True)

    def sweep2(kt, carry):
        l8, acc = carry
        k0 = pl.multiple_of(kt * tk, tk)
        pp = jnp.exp(s_ref[kt] - m)
        return l8 + fold(pp, jnp.sum), acc + _dot(vt_ref[0, 0:HEAD_DIM, pl.ds(k0, tk)], pp.astype(BF16))

    l8, acc = lax.fori_loop(0, kd + 1, sweep2, (jnp.zeros((SUBLANES, nq), F32), jnp.zeros((HEAD_DIM, nq), F32)))
    o_slc = acc / jnp.sum(l8, axis=0, keepdims=True)

    gt = gt_ref[...]
    outs = []
    for h in range(NSA_HEADS):
        c = slice(h * qb, (h + 1) * qb)
        outs.append(gt[3 * h:3 * h + 1, :] * o_cmp[:, c] + gt[3 * h + 1:3 * h + 2, :] * o_slc[:, c]
                    + gt[3 * h + 2:3 * h + 3, :] * o_win[:, c])
    o_ref[...] = jnp.concatenate(outs, axis=0)


def _nsa_prompt_call(qt, gt, kcvc, kcvct, slc_tm3, win_tm3, vt3, *, qb=256, tk=SLC_TILE):
    b, t, _ = slc_tm3.shape
    nq = t // qb
    g = kcvc.shape[1]
    assert t % tk == 0 and tk % qb == 0 and t >= NSA_WINDOW + qb
    return pl.pallas_call(
        functools.partial(_nsa_prompt_kernel, t=t, qb=qb, tk=tk),
        grid=(b, nq),
        in_specs=[pl.BlockSpec((256, qb), lambda i, j: (0, i * nq + j)),
                  pl.BlockSpec((2 * SUBLANES, qb), lambda i, j: (0, i * nq + j)),
                  pl.BlockSpec((1, g, LANES), lambda i, j: (i, 0, 0)),
                  pl.BlockSpec((1, LANES, g), lambda i, j: (i, 0, 0)),
                  pl.BlockSpec((1, t, LANES), lambda i, j: (i, 0, 0)),
                  pl.BlockSpec((1, t, LANES), lambda i, j: (i, 0, 0)),
                  pl.BlockSpec((1, LANES, t), lambda i, j: (i, 2, 0))],
        out_specs=pl.BlockSpec((256, qb), lambda i, j: (0, i * nq + j)),
        out_shape=jax.ShapeDtypeStruct((256, b * t), F32),
        scratch_shapes=[pltpu.VMEM((t // NSA_SLC_BLOCK, NSA_HEADS * qb), F32),
                        pltpu.VMEM((t // tk, tk, NSA_HEADS * qb), F32)],
        compiler_params=pltpu.CompilerParams(dimension_semantics=("arbitrary", "arbitrary"),
                                             vmem_limit_bytes=VMEM_LIMIT),
        name="nsa_prompt",
    )(qt, gt, kcvc, kcvct, slc_tm3, win_tm3, vt3)


def _nsa_sample_kernel(ptab_ref, *refs, past, tq, pps):
    del ptab_ref
    pages = refs[:pps]
    q_ref, g_ref, kcvct_ref, new_ref, wst_ref, wnew_ref, o_ref, kt_ref, vt_ref, e_ref = refs[pps:]
    j = pl.program_id(1)

    @pl.when((pl.program_id(0) == 0) & (j == 0))
    def _():
        e_ref[...] = (_iota(e_ref.shape, 0) == _iota(e_ref.shape, 1) // NSA_SLC_BLOCK).astype(BF16)

    for k in range(pps):
        pg = pages[k][0, 0]
        c0 = pl.multiple_of((j * pps + k) * PAGE_SIZE, PAGE_SIZE)
        kt_ref[:, pl.ds(c0, PAGE_SIZE)] = pg[0].astype(BF16)
        vt_ref[:, pl.ds(c0, PAGE_SIZE)] = pg[1].astype(BF16)

    @pl.when(j == pl.num_programs(1) - 1)
    def _():
        rows = NSA_HEADS * tq
        qs = (q_ref[0] * (HEAD_DIM ** -0.5)).astype(BF16)
        qpos_q = past + _iota((tq, 1), 0)
        qpos = jnp.concatenate([qpos_q] * NSA_HEADS, axis=0)
        l_all = past + tq
        n_cmp = (l_all - NSA_CMP_LEN) // NSA_CMP_STRIDE + 1
        n_blk = -(-l_all // NSA_SLC_BLOCK)
        nb_pad = -(-n_blk // LANES) * LANES

        kcvct = kcvct_ref[0].astype(BF16)
        g = kcvct.shape[1]
        s = _dot(qs, kcvct[0:HEAD_DIM])
        n_idx = _iota(s.shape, 1)
        ok = (n_idx * NSA_CMP_STRIDE + (NSA_CMP_LEN - 1) <= qpos) & (n_idx < n_cmp)
        p = _softmax_axis(s, ok, 1)
        o_cmp = _dot_nt(p.astype(BF16), kcvct[HEAD_DIM:2 * HEAD_DIM])
        psum = p[0:tq]
        for h in range(1, NSA_HEADS):
            psum = psum + p[h * tq:(h + 1) * tq]
        cn = _iota((g, nb_pad), 0)
        cj = _iota((g, nb_pad), 1)
        cover = ((cn * NSA_CMP_STRIDE <= cj * NSA_SLC_BLOCK + (NSA_SLC_BLOCK - 1))
                 & (cn * NSA_CMP_STRIDE + (NSA_CMP_LEN - 1) >= cj * NSA_SLC_BLOCK) & (cn < n_cmp)).astype(BF16)
        p_hi, p_lo = _split(psum)
        imp = _dot(p_hi, cover) + _dot(p_lo, cover)
        imp = _forced_importance(imp, _iota(imp.shape, 1), qpos_q // NSA_SLC_BLOCK)
        picked = _topk_mask(imp, n_blk, NSA_TOPN)

        nbe = e_ref.shape[0]
        bias = jnp.where(picked[:, 0:nbe], 0.0, NEG).astype(BF16)
        bias = jnp.concatenate([bias] * NSA_HEADS, axis=0)
        sc = _dot(qs, kt_ref[...]) + _dot(bias, e_ref[...])
        new_ok = past + _iota((rows, PAGE_SIZE), 1) <= qpos
        sc_new = jnp.where(new_ok, _dot(qs, new_ref[0, 0].astype(BF16)), NEG)
        m = jnp.maximum(jnp.max(sc, axis=-1, keepdims=True), jnp.max(sc_new, axis=-1, keepdims=True))
        pp = jnp.exp(sc - m)
        pp_new = jnp.exp(sc_new - m)
        l = jnp.sum(pp, axis=-1, keepdims=True) + jnp.sum(pp_new, axis=-1, keepdims=True)
        o_slc = (_dot_nt(pp.astype(BF16), vt_ref[...]) + _dot_nt(pp_new.astype(BF16), new_ref[0, 1].astype(BF16))) / l

        kw = jnp.concatenate([wst_ref[0, 0, 0], wnew_ref[0, 0]], axis=1).astype(BF16)
        vw = jnp.concatenate([wst_ref[0, 0, 1], wnew_ref[0, 1]], axis=1).astype(BF16)
        sw = _dot(qs, kw)
        kpos = past - NSA_WINDOW + _iota(sw.shape, 1)
        okw = (kpos <= qpos) & (kpos > qpos - NSA_WINDOW) & (kpos >= 0)
        o_win = _dot_nt(_softmax_axis(sw, okw, 1).astype(BF16), vw)

        gg = g_ref[0]
        o_ref[0] = gg[:, 0:1] * o_cmp + gg[:, 1:2] * o_slc + gg[:, 2:3] * o_win


def _nsa_sample_call(cache5, layer, page_table, q32, g32, kcvct, newt, wstate5, wnewt):
    b, n_pages = page_table.shape
    past = n_pages * PAGE_SIZE
    tq = q32.shape[1] // NSA_HEADS
    pps = _pages_per_step(n_pages, 64)
    nbe = -(-(past // NSA_SLC_BLOCK) // LANES) * LANES
    assert tq <= NSA_SLC_BLOCK and past >= NSA_WINDOW
    assert (past + tq - NSA_CMP_LEN) // NSA_CMP_STRIDE + 1 <= past // NSA_CMP_STRIDE - 1
    rows = NSA_HEADS * tq
    page_specs = [pl.BlockSpec((1, 1, 2, HEAD_DIM, PAGE_SIZE), functools.partial(
        lambda i, j, pt, k: (layer, pt[i, j * pps + k], 1, 0, 0), k=k)) for k in range(pps)]
    per_b = lambda a: pl.BlockSpec((1,) + a.shape[1:], lambda i, j, pt: (i,) + (0,) * (a.ndim - 1))
    grid_spec = pltpu.PrefetchScalarGridSpec(
        num_scalar_prefetch=1, grid=(b, n_pages // pps),
        in_specs=page_specs + [per_b(q32), per_b(g32), per_b(kcvct), per_b(newt),
                               pl.BlockSpec((1, 1, 2, HEAD_DIM, NSA_WINDOW), lambda i, j, pt: (layer, i, 0, 0, 0)),
                               per_b(wnewt)],
        out_specs=pl.BlockSpec((1, rows, HEAD_DIM), lambda i, j, pt: (i, 0, 0)),
        scratch_shapes=[pltpu.VMEM((HEAD_DIM, past), BF16), pltpu.VMEM((HEAD_DIM, past), BF16),
                        pltpu.VMEM((nbe, past), BF16)])
    return pl.pallas_call(
        functools.partial(_nsa_sample_kernel, past=past, tq=tq, pps=pps),
        grid_spec=grid_spec, out_shape=jax.ShapeDtypeStruct((b, rows, HEAD_DIM), F32),
        compiler_params=pltpu.CompilerParams(dimension_semantics=("arbitrary", "arbitrary"),
                                             vmem_limit_bytes=VMEM_LIMIT),
        name="nsa_sample",
    )(page_table, *([cache5] * pps), q32, g32, kcvct, newt, wstate5, wnewt)


def _moba_prompt_kernel(qt_ref, kmean_ref, k_ref, vt_ref, o_ref, sel_ref, acc_ref, *, nb):
    i = pl.program_id(1)
    r0 = pl.multiple_of(i * MOBA_BLOCK, MOBA_BLOCK)
    qb = MOBA_BLOCK
    nq = MOBA_HEADS * qb
    qt = qt_ref[...]
    rowgrp = _iota(qt.shape, 0) // HEAD_DIM
    kmean = kmean_ref[0]
    qpad = []
    for h in range(MOBA_HEADS):
        qh = jnp.where(rowgrp == h, qt, 0.0)
        gs = _dot_3pass(kmean, qh)
        n_idx = _iota(gs.shape, 0)
        gs = jnp.where(n_idx < i, gs, NEG)
        sel_ref[:, h * qb:(h + 1) * qb] = jnp.where(_topk_mask_t(gs, nb, MOBA_TOPK) & (n_idx < i), 0.0, NEG)
        qpad.append((qh * (HEAD_DIM ** -0.5)).astype(BF16))
    qcat = jnp.concatenate(qpad, axis=1)

    def block(n, width=1):
        k0 = pl.multiple_of(n * MOBA_BLOCK, width * MOBA_BLOCK)
        return k_ref[0, pl.ds(k0, width * MOBA_BLOCK), :], vt_ref[0, :, pl.ds(k0, width * MOBA_BLOCK)]

    def fold(x, op):
        return op(x.reshape(x.shape[0] // SUBLANES, SUBLANES, nq), axis=0)

    n_pairs = (i + 1) // 2

    def pair_scores(n2, kb, shift):
        bias = jnp.stack([sel_ref[pl.ds(2 * n2, 1), :] - shift, sel_ref[pl.ds(2 * n2 + 1, 1), :] - shift])
        return (_dot(kb, qcat).reshape(2, MOBA_BLOCK, nq) + bias).reshape(2 * MOBA_BLOCK, nq)

    def sweep1(n2, m8):
        return jnp.maximum(m8, fold(pair_scores(n2, block(2 * n2, 2)[0], 0.0), jnp.max))

    m8 = lax.fori_loop(0, n_pairs, sweep1, jnp.full((SUBLANES, nq), NEG, F32))
    own_k, own_v = block(i)
    causal = _iota((MOBA_BLOCK, nq), 0) <= _iota((MOBA_BLOCK, nq), 1) % qb
    own_s = jnp.where(causal, _dot(own_k, qcat), NEG)
    m = jnp.max(jnp.maximum(m8, fold(own_s, jnp.max)), axis=0, keepdims=True)

    acc_ref[...] = jnp.zeros_like(acc_ref)

    def accumulate(p, vb):
        pb = p.astype(BF16)
        for h in range(MOBA_HEADS):
            acc_ref[h] += _dot(vb[h * HEAD_DIM:(h + 1) * HEAD_DIM, :], pb[:, h * qb:(h + 1) * qb])
        return fold(p, jnp.sum)

    def sweep2(n2, l8):
        kb, vb = block(2 * n2, 2)
        return l8 + accumulate(jnp.exp(pair_scores(n2, kb, m)), vb)

    l8 = lax.fori_loop(0, n_pairs, sweep2, jnp.zeros((SUBLANES, nq), F32))
    l = jnp.sum(l8 + accumulate(jnp.exp(own_s - m), own_v), axis=0, keepdims=True)
    o_ref[...] = jnp.concatenate([acc_ref[h] / l[:, h * qb:(h + 1) * qb] for h in range(MOBA_HEADS)], axis=0)


def _moba_prompt_call(qt, kmean3, mk_tm3, vt3):
    b, t, _ = mk_tm3.shape
    nb = t // MOBA_BLOCK
    nbp = kmean3.shape[1]
    return pl.pallas_call(
        functools.partial(_moba_prompt_kernel, nb=nb),
        grid=(b, nb),
        in_specs=[pl.BlockSpec((256, MOBA_BLOCK), lambda i, j: (1, i * nb + j)),
                  pl.BlockSpec((1, nbp, 256), lambda i, j: (i, 0, 0)),
                  pl.BlockSpec((1, t, 256), lambda i, j: (i, 0, 0)),
                  pl.BlockSpec((1, 256, t), lambda i, j: (i, 0, 0))],
        out_specs=pl.BlockSpec((256, MOBA_BLOCK), lambda i, j: (0, i * nb + j)),
        out_shape=jax.ShapeDtypeStruct((256, b * t), F32),
        scratch_shapes=[pltpu.VMEM((nbp, MOBA_HEADS * MOBA_BLOCK), F32),
                        pltpu.VMEM((MOBA_HEADS, HEAD_DIM, MOBA_BLOCK), F32)],
        compiler_params=pltpu.CompilerParams(dimension_semantics=("arbitrary", "arbitrary"),
                                             vmem_limit_bytes=VMEM_LIMIT),
        name="moba_prompt",
    )(qt, kmean3, mk_tm3, vt3)


def _moba_sample_kernel(ptab_ref, *refs, past, tq, pps):
    del ptab_ref
    pages = refs[:pps]
    q_ref, new_ref, o_ref, kt_ref, vt_ref, km_ref, e_ref = refs[pps:]
    j = pl.program_id(1)
    ppb = MOBA_BLOCK // PAGE_SIZE

    @pl.when((pl.program_id(0) == 0) & (j == 0))
    def _():
        e_ref[...] = (_iota(e_ref.shape, 0) == _iota(e_ref.shape, 1) // MOBA_BLOCK).astype(BF16)

    @pl.when(j == 0)
    def _():
        km_ref[...] = jnp.zeros_like(km_ref)

    lane = _iota(km_ref.shape, 1)
    for k in range(0, pps, ppb):
        ksum = jnp.zeros((256, PAGE_SIZE), F32)
        for kk in range(ppb):
            pg = pages[k + kk][0, 0]
            c0 = pl.multiple_of((j * pps + k + kk) * PAGE_SIZE, PAGE_SIZE)
            kt_ref[:, pl.ds(c0, PAGE_SIZE)] = pg[0].astype(BF16)
            vt_ref[:, pl.ds(c0, PAGE_SIZE)] = pg[1].astype(BF16)
            ksum = ksum + pg[0]
        col = jnp.sum(ksum, axis=1, keepdims=True) * (1.0 / MOBA_BLOCK)
        km_ref[...] = jnp.where(lane == j * (pps // ppb) + k // ppb, col, km_ref[...])

    @pl.when(j == pl.num_programs(1) - 1)
    def _():
        n_past = past // MOBA_BLOCK
        q = q_ref[0]
        rows = q.shape[0]
        gs = _dot_3pass(q, km_ref[...])
        n_idx = _iota(gs.shape, 1)
        gs = jnp.where(n_idx < n_past, gs, NEG)
        bias = jnp.where(_topk_mask(gs, n_past, MOBA_TOPK) & (n_idx < n_past), 0.0, NEG).astype(BF16)
        qb = (q * (HEAD_DIM ** -0.5)).astype(BF16)
        s = _dot(qb, kt_ref[...]) + _dot(bias, e_ref[...])
        own_ok = _iota((rows, PAGE_SIZE), 1) <= _iota((rows, PAGE_SIZE), 0) % tq
        s_own = jnp.where(own_ok, _dot(qb, new_ref[0, 0].astype(BF16)), NEG)
        m = jnp.maximum(jnp.max(s, axis=-1, keepdims=True), jnp.max(s_own, axis=-1, keepdims=True))
        p = jnp.exp(s - m)
        p_own = jnp.exp(s_own - m)
        l = jnp.sum(p, axis=-1, keepdims=True) + jnp.sum(p_own, axis=-1, keepdims=True)
        oh = (_dot_nt(p.astype(BF16), vt_ref[...]) + _dot_nt(p_own.astype(BF16), new_ref[0, 1].astype(BF16))) / l
        grp = _iota((tq, 256), 1) // HEAD_DIM
        out = oh[0:tq]
        for h in range(1, MOBA_HEADS):
            out = jnp.where(grp == h, oh[h * tq:(h + 1) * tq], out)
        o_ref[0] = out


def _moba_sample_call(cache5, layer, page_table, qbd, newt):
    b, n_pages = page_table.shape
    past = n_pages * PAGE_SIZE
    rows = qbd.shape[1]
    tq = rows // MOBA_HEADS
    pps = _pages_per_step(n_pages, 32)
    assert past % MOBA_BLOCK == 0 and pps % (MOBA_BLOCK // PAGE_SIZE) == 0 and tq <= PAGE_SIZE
    assert past // MOBA_BLOCK <= LANES
    page_specs = [pl.BlockSpec((1, 1, 2, 256, PAGE_SIZE), functools.partial(
        lambda i, j, pt, k: (layer, pt[i, j * pps + k], 0, 0, 0), k=k)) for k in range(pps)]
    per_b = lambda a: pl.BlockSpec((1,) + a.shape[1:], lambda i, j, pt: (i,) + (0,) * (a.ndim - 1))
    grid_spec = pltpu.PrefetchScalarGridSpec(
        num_scalar_prefetch=1, grid=(b, n_pages // pps),
        in_specs=page_specs + [per_b(qbd), per_b(newt)],
        out_specs=pl.BlockSpec((1, tq, 256), lambda i, j, pt: (i, 0, 0)),
        scratch_shapes=[pltpu.VMEM((256, past), BF16), pltpu.VMEM((256, past), BF16),
                        pltpu.VMEM((256, LANES), F32), pltpu.VMEM((LANES, past), BF16)])
    return pl.pallas_call(
        functools.partial(_moba_sample_kernel, past=past, tq=tq, pps=pps),
        grid_spec=grid_spec, out_shape=jax.ShapeDtypeStruct((b, tq, 256), F32),
        compiler_params=pltpu.CompilerParams(dimension_semantics=("arbitrary", "arbitrary"),
                                             vmem_limit_bytes=VMEM_LIMIT),
        name="moba_sample",
    )(page_table, *([cache5] * pps), qbd, newt)


def _merge_kernel(x_ref, oab_ref, oct_ref, odt_ref, n1_ref, wgt_ref, wbr_ref, wo_ref, y_ref):
    x = x_ref[...]
    xn = (x * lax.rsqrt(jnp.mean(x * x, axis=-1, keepdims=True) + EPS) * n1_ref[...]).astype(BF16)
    branches = [oab_ref[:, 0:256], oab_ref[:, 256:512], oct_ref[...].T.astype(BF16), odt_ref[...].T.astype(BF16)]
    merged = jnp.zeros(x.shape, F32)
    for n in range(N_BRANCH):
        g = jax.nn.sigmoid(_dot_nt(xn, wgt_ref[n * D_MODEL:(n + 1) * D_MODEL, :]))
        merged = merged + g * _dot(branches[n], wbr_ref[n])
    y_ref[...] = x + _dot(merged.astype(BF16), wo_ref[...])


def _merge_call(x2d, oab, oct, odt, lw, *, tm):
    rows = x2d.shape[0]
    row = lambda w: pl.BlockSpec((tm, w), lambda i: (i, 0))
    col = lambda r: pl.BlockSpec((r, tm), lambda i: (0, i))
    full = lambda a: pl.BlockSpec(a.shape, lambda i: (0,) * a.ndim)
    ws = [lw['norm1'], lw['wg_t'], lw['w_branch'], lw['w_o']]
    return pl.pallas_call(
        _merge_kernel, grid=(rows // tm,),
        in_specs=[row(D_MODEL), row(512), col(256), col(256)] + [full(a) for a in ws],
        out_specs=row(D_MODEL), out_shape=jax.ShapeDtypeStruct((rows, D_MODEL), F32),
        compiler_params=pltpu.CompilerParams(dimension_semantics=("arbitrary",), vmem_limit_bytes=VMEM_LIMIT),
        name="merge",
    )(x2d, oab, oct, odt, *ws)


def _ffn_kernel(*refs, tiles_per_seq, seg):
    it = iter(refs)
    x_ref, n2_ref, wgate_ref, wup_ref, cw_ref, cb_ref, wdown_ref = (next(it) for _ in range(7))
    if seg is not None:
        h1_ref, h2_ref = next(it), next(it)
    y_ref, tail_ref = next(it), next(it)
    if seg is None:
        prev_ref = next(it)

    x = x_ref[...]
    tm = x.shape[0]
    h2 = (x * lax.rsqrt(jnp.mean(x * x, axis=-1, keepdims=True) + EPS) * n2_ref[...]).astype(BF16)
    gate = _dot(h2, wgate_ref[...])
    if seg is None:
        @pl.when(pl.program_id(0) % tiles_per_seq == 0)
        def _():
            prev_ref[...] = jnp.zeros_like(prev_ref)

        g1, g2 = _shift_rows(gate, prev_ref[SUBLANES - 1:SUBLANES, :], prev_ref[SUBLANES - 2:SUBLANES - 1, :], None)
        prev_ref[...] = gate[tm - SUBLANES:, :]
        tail_ref[0] = gate[tm - SUBLANES:, :]
    else:
        g1, g2 = _shift_rows(gate, h1_ref[...], h2_ref[...], seg)
        tail_ref[...] = gate.reshape(tail_ref.shape)
    ac = g2 * cw_ref[0:1, :] + cb_ref[...] + g1 * cw_ref[1:2, :] + gate * cw_ref[2:3, :]
    act = jax.nn.silu(ac) * _dot(h2, wup_ref[...])
    y_ref[...] = x + _dot(act.astype(BF16), wdown_ref[...])


def _ffn_call(x2d, lw, halos, *, tm, tiles_per_seq, n_seq, seg):
    rows = x2d.shape[0]
    row = lambda w: pl.BlockSpec((tm, w), lambda i: (i, 0))
    full = lambda a: pl.BlockSpec(a.shape, lambda i: (0,) * a.ndim)
    ws = [lw['norm2'], lw['w_gate'], lw['w_up'], lw['ffn_conv_w'], lw['ffn_conv_b'], lw['w_down']]
    ins = [x2d] + ws
    in_specs = [row(D_MODEL)] + [full(a) for a in ws]
    scratch = []
    if seg is None:
        tail_spec = pl.BlockSpec((1, SUBLANES, D_FF), lambda i: (i // tiles_per_seq, 0, 0))
        scratch.append(pltpu.VMEM((SUBLANES, D_FF), F32))
    else:
        ins += list(halos)
        in_specs += [row(D_FF), row(D_FF)]
        tail_spec = pl.BlockSpec((n_seq, SUBLANES, D_FF), lambda i: (0, 0, 0))
    return pl.pallas_call(
        functools.partial(_ffn_kernel, tiles_per_seq=tiles_per_seq, seg=seg),
        grid=(rows // tm,), in_specs=in_specs, out_specs=[row(D_MODEL), tail_spec],
        out_shape=[jax.ShapeDtypeStruct((rows, D_MODEL), F32), jax.ShapeDtypeStruct((n_seq, SUBLANES, D_FF), F32)],
        scratch_shapes=scratch,
        compiler_params=pltpu.CompilerParams(dimension_semantics=("arbitrary",), vmem_limit_bytes=VMEM_LIMIT),
        name="ffn_prompt" if seg is None else "ffn_sample",
    )(*ins)


def _rope_tables_t(pos):
    half = ROT_DIM // 2
    inv = jnp.power(jnp.float32(ROPE_THETA), -jnp.arange(half, dtype=F32) / half)
    ang = inv[:, None] * pos.astype(F32)[None, :]
    return jnp.cos(ang), jnp.sin(ang)


def _layer_weights(l, p, sample_tq, dec_batch):
    wt = jnp.transpose(p['w_in'], (2, 0, 1))[:, l, :]
    o_ck, o_cv, o_g = 1536, 1728, 1920
    o_dq = o_g + 3 * NSA_HEADS
    o_br = o_dq + 3 * 256
    hd = HEAD_DIM
    w_tm = jnp.concatenate([wt[0:1280], wt[o_g:o_g + 12], jnp.zeros((LANES - 12, D_MODEL), F32)], axis=0)
    w_tr = jnp.concatenate([
        wt[1280:1536], wt[o_dq:o_dq + 256],
        wt[o_ck:o_ck + hd], wt[o_cv:o_cv + hd], wt[o_ck + hd:o_ck + 2 * hd], wt[o_cv + hd:o_cv + 2 * hd],
        wt[o_ck + 2 * hd:o_ck + 3 * hd], wt[o_cv + 2 * hd:o_cv + 3 * hd],
        wt[o_dq + 256:o_dq + 768], wt[o_g:o_g + 12], jnp.zeros((2 * SUBLANES - 12, D_MODEL), F32)], axis=0)
    lw = {'w_tm': w_tm.astype(BF16), 'w_tr': w_tr.astype(BF16), 'wg_t': wt[o_br:].astype(BF16)}
    row = lambda a: a.reshape(1, -1)
    lw['norm1'] = row(p['norm1'][l])
    lw['norm2'] = row(p['norm2'][l])
    lw['conv_w'] = p['conv_w'][l]
    lw['conv_b'] = row(p['conv_b'][l])
    lw['ln_g'] = row(p['gmlp_ln_g'][l])
    lw['ln_b'] = row(p['gmlp_ln_b'][l])
    tril = jnp.tril(jnp.ones((GMLP_CHUNK, GMLP_CHUNK), bool))
    ws = jnp.where(tril[None], p['gmlp_ws'][l], 0.0)
    bs = p['gmlp_bs'][l]
    lw['mix_p'] = ws.reshape(GMLP_GROUPS * GMLP_CHUNK, GMLP_CHUNK).astype(BF16)
    lw['mixb_p'] = jnp.repeat(bs.T, HEAD_DIM, axis=1)
    eye = jnp.eye(dec_batch, dtype=F32)
    lw['mix_s'] = jnp.concatenate([jnp.kron(eye, ws[g, :sample_tq, :sample_tq]) for g in range(GMLP_GROUPS)],
                                  axis=0).astype(BF16)
    lw['mixb_s'] = jnp.tile(jnp.repeat(bs.T[:sample_tq], HEAD_DIM, axis=1), (dec_batch, 1))
    kg = p['nsa_k_gain'][l]
    one = jnp.ones((hd,), F32)
    gains = jnp.concatenate([jnp.tile(p['nsa_q_gain'][l], NSA_HEADS), jnp.tile(p['moba_q_gain'][l], MOBA_HEADS),
                             kg[0], one, kg[1], one, kg[2], one, jnp.tile(p['moba_k_gain'][l], MOBA_HEADS)])
    lw['gain_t'] = jnp.broadcast_to(gains[:, None], (R_MV, LANES))
    w1c = p['nsa_cmp_w1'][l].reshape(2, NSA_CMP_LEN, HEAD_DIM, NSA_CMP_HIDDEN)
    pos = p['nsa_cmp_pos'][l]
    zero = jnp.zeros((HEAD_DIM, NSA_CMP_HIDDEN), F32)

    def pair_w(r_off):
        mats = []
        for pp in range(NSA_CMP_STRIDE // 2):
            blocks = []
            for r in (2 * pp, 2 * pp + 1):
                blocks.append(jnp.concatenate([w1c[0, r + r_off], zero], axis=1))
                blocks.append(jnp.concatenate([zero, w1c[1, r + r_off]], axis=1))
            mats.append(jnp.concatenate(blocks, axis=0))
        return jnp.stack(mats).astype(BF16)

    def pair_pos(r_off):
        return jnp.stack([jnp.concatenate([pos[0, 2 * pp + r_off], pos[1, 2 * pp + r_off],
                                           pos[0, 2 * pp + 1 + r_off], pos[1, 2 * pp + 1 + r_off]])
                          for pp in range(NSA_CMP_STRIDE // 2)])

    lw['cmp_wa'], lw['cmp_wb'] = pair_w(0), pair_w(NSA_CMP_STRIDE)
    lw['cmp_pt'], lw['cmp_pb'] = pair_pos(0), pair_pos(NSA_CMP_STRIDE)
    w2 = p['nsa_cmp_w2'][l]
    z2 = jnp.zeros((NSA_CMP_HIDDEN, HEAD_DIM), F32)
    lw['cmp_w2'] = jnp.concatenate([jnp.concatenate([w2[0], z2], axis=1),
                                    jnp.concatenate([z2, w2[1]], axis=1)], axis=0).astype(BF16)
    lw['w_branch'] = p['w_branch'][l].astype(BF16)
    lw['w_o'] = p['w_o'][l].astype(BF16)
    lw['w_gate'] = p['w_ffn_gate'][l].astype(BF16)
    lw['w_up'] = p['w_ffn_up'][l].astype(BF16)
    lw['w_down'] = p['w_ffn_down'][l].astype(BF16)
    lw['ffn_conv_w'] = p['ffn_conv_w'][l]
    lw['ffn_conv_b'] = row(p['ffn_conv_b'][l])
    return lw


def _halos(buf, tq):
    b, _, c = buf.shape
    z = jnp.zeros((b, tq - 1, c), F32)
    h1 = jnp.concatenate([buf[:, 1:2], z], axis=1)
    h2 = jnp.concatenate([buf[:, 0:2], z[:, 1:]], axis=1)
    return h1.reshape(b * tq, c), h2.reshape(b * tq, c)


def _row_tile(rows, cap):
    tm = cap
    while rows % tm:
        tm //= 2
    return tm


def _per_seq_pages(xt, b, tq):
    c = xt.shape[0]
    x = xt.reshape(c, b, tq).transpose(1, 0, 2)
    return jnp.concatenate([x, jnp.zeros((b, c, PAGE_SIZE - tq), F32)], axis=-1)


def _prompt_layer(x2d, lw, tables, b, t):
    tm = _row_tile(t, 512)
    assert tm % MOBA_BLOCK == 0 and tm % SLC_TILE == 0
    tps = t // tm
    pw = dict(lw, mix=lw['mix_p'], mixb=lw['mixb_p'])
    (oab, qt, nsat, wint, mobat, gt, _, ztail, cmp_tm, slc_tm, win_tm, mk_tm, vt16, kmean) = _proj_call(
        x2d, pw, tables, None, tm=tm, tiles_per_seq=tps, n_seq=b, chunk=GMLP_CHUNK, seg=None)
    kcvc, kcvct = _compress_prompt_call(cmp_tm.reshape(b, t, LANES), lw)
    o_ct = _nsa_prompt_call(qt, gt, kcvc, kcvct, slc_tm.reshape(b, t, LANES), win_tm.reshape(b, t, LANES), vt16)
    nb = t // MOBA_BLOCK
    kmean3 = kmean[:, 0:tm // MOBA_BLOCK].reshape(b, nb, 256)
    nbp = -(-nb // SUBLANES) * SUBLANES
    if nbp != nb:
        kmean3 = jnp.concatenate([kmean3, jnp.zeros((b, nbp - nb, 256), F32)], axis=1)
    o_dt = _moba_prompt_call(qt, kmean3, mk_tm.reshape(b, t, 256), vt16)
    x_mid = _merge_call(x2d, oab, o_ct, o_dt, lw, tm=tm)
    y, ftail = _ffn_call(x_mid, lw, None, tm=tm, tiles_per_seq=tps, n_seq=b, seg=None)
    hd = HEAD_DIM
    wlen = min(NSA_WINDOW, t)
    state = (nsat.reshape(b, 4, hd, t).transpose(0, 3, 1, 2),
             mobat.reshape(b, 2, MOBA_HEADS, hd, t).transpose(0, 4, 1, 2, 3),
             wint[:, :, t - wlen:].reshape(b, 2, hd, wlen).transpose(0, 3, 1, 2),
             ztail[:, SUBLANES - 2:], ftail[:, SUBLANES - 2:])
    return y, state


def _sample_layer(x2d, lw, tables, b, tq, layer, nsa_cache5, moba_cache5, page_table, win_state5, conv_state,
                  ffn_state):
    rows = b * tq
    hd = HEAD_DIM
    pw = dict(lw, mix=lw['mix_s'], mixb=lw['mixb_s'])
    oab, qt, nsat, wint, mobat, _, gtm, ztail, vn = _proj_call(
        x2d, pw, tables, _halos(conv_state, tq), tm=rows, tiles_per_seq=1, n_seq=b, chunk=rows, seg=tq)
    nsat, wint, mobat = nsat[0], wint[0], mobat[0]
    kcvct = _compress_paged_call(nsa_cache5, layer, page_table, lw)
    q32 = qt[0:256].reshape(NSA_HEADS, hd, b, tq).transpose(2, 0, 3, 1).reshape(b, NSA_HEADS * tq, hd)
    g4 = gtm[:, 0:3 * NSA_HEADS].reshape(b, tq, NSA_HEADS, 3).transpose(0, 2, 1, 3).reshape(b, NSA_HEADS * tq, 3)
    g32 = jnp.concatenate([g4, jnp.zeros((b, NSA_HEADS * tq, LANES - 3), F32)], axis=-1)
    newt = _per_seq_pages(nsat[2 * hd:4 * hd], b, tq).reshape(b, 2, hd, PAGE_SIZE)
    wnewt = _per_seq_pages(wint, b, tq).reshape(b, 2, hd, PAGE_SIZE)
    o32 = _nsa_sample_call(nsa_cache5, layer, page_table, q32, g32, kcvct, newt, win_state5, wnewt)
    o_ct = o32.reshape(b, NSA_HEADS, tq, hd).transpose(1, 3, 0, 2).reshape(256, rows)
    qm = qt[256:512].reshape(MOBA_HEADS, hd, b, tq).transpose(2, 0, 3, 1)
    eye = jnp.eye(MOBA_HEADS, dtype=F32)
    qbd = (qm[:, :, :, None, :] * eye[None, :, None, :, None]).reshape(b, MOBA_HEADS * tq, 256)
    mnewt = _per_seq_pages(mobat, b, tq).reshape(b, 2, 256, PAGE_SIZE)
    o_dt = _moba_sample_call(moba_cache5, layer, page_table, qbd, mnewt).transpose(2, 0, 1).reshape(256, rows)
    x_mid = _merge_call(x2d, oab, o_ct, o_dt, lw, tm=rows)
    y, ftail = _ffn_call(x_mid, lw, _halos(ffn_state, tq), tm=rows, tiles_per_seq=1, n_seq=b, seg=tq)
    wnew = wint.reshape(2, hd, b, tq).transpose(2, 0, 1, 3)
    win_all = jnp.concatenate([win_state5[layer], wnew], axis=-1)
    win_new = win_all[..., win_all.shape[-1] - NSA_WINDOW:].transpose(0, 3, 1, 2)
    state = (nsat.T.reshape(b, tq, 4, hd), mobat.T.reshape(b, tq, 2, MOBA_HEADS, hd), win_new,
             ztail[:, tq - 2:tq], ftail[:, tq - 2:tq], vn.reshape(b, tq, 256))
    return y, state


def kernel(x_prompt, x_sample, cache_nsa_kv, cache_moba_kv, state_nsa_win, state_conv, state_ffn_conv, page_table, norm1, w_in, conv_w, conv_b, gmlp_ln_g, gmlp_ln_b, gmlp_ws, gmlp_bs, nsa_q_gain, nsa_k_gain, nsa_cmp_pos, nsa_cmp_w1, nsa_cmp_w2, moba_q_gain, moba_k_gain, w_branch, w_o, norm2, w_ffn_gate, w_ffn_up, ffn_conv_w, ffn_conv_b, w_ffn_down):
    params = dict(norm1=norm1, w_in=w_in, conv_w=conv_w, conv_b=conv_b, gmlp_ln_g=gmlp_ln_g, gmlp_ln_b=gmlp_ln_b,
                  gmlp_ws=gmlp_ws, gmlp_bs=gmlp_bs, nsa_q_gain=nsa_q_gain, nsa_k_gain=nsa_k_gain,
                  nsa_cmp_pos=nsa_cmp_pos, nsa_cmp_w1=nsa_cmp_w1, nsa_cmp_w2=nsa_cmp_w2, moba_q_gain=moba_q_gain,
                  moba_k_gain=moba_k_gain, w_branch=w_branch, w_o=w_o, norm2=norm2, w_ffn_gate=w_ffn_gate,
                  w_ffn_up=w_ffn_up, ffn_conv_w=ffn_conv_w, ffn_conv_b=ffn_conv_b, w_ffn_down=w_ffn_down)
    bp, tp, _ = x_prompt.shape
    bs, ts, _ = x_sample.shape
    depth = w_in.shape[0]
    n_pool = cache_nsa_kv.shape[1]
    past = page_table.shape[1] * PAGE_SIZE
    assert ts == SUBLANES and state_nsa_win.shape[2] == NSA_WINDOW
    nsa_cache5 = jnp.transpose(cache_nsa_kv, (0, 1, 3, 4, 2))
    moba_cache5 = jnp.transpose(cache_moba_kv, (0, 1, 3, 4, 5, 2)).reshape(depth, n_pool, 2, 256, PAGE_SIZE)
    win_state5 = jnp.transpose(state_nsa_win, (0, 1, 3, 4, 2))
    tab_p = _rope_tables_t(jnp.arange(tp, dtype=jnp.int32))
    tab_s = tuple(jnp.tile(a, (1, bs)) for a in _rope_tables_t(past + jnp.arange(ts, dtype=jnp.int32)))
    yp = x_prompt.reshape(bp * tp, D_MODEL)
    ys = x_sample.reshape(bs * ts, D_MODEL)
    sp, ss = [], []
    for l in range(depth):
        lw = _layer_weights(l, params, ts, bs)
        yp, st_p = _prompt_layer(yp, lw, tab_p, bp, tp)
        ys, st_s = _sample_layer(ys, lw, tab_s, bs, ts, l, nsa_cache5, moba_cache5, page_table, win_state5,
                                 state_conv[l], state_ffn_conv[l])
        sp.append(st_p)
        ss.append(st_s)
    stack = lambda lst, k: jnp.stack([s[k] for s in lst])
    return (yp.reshape(bp, tp, D_MODEL), ys.reshape(bs, ts, D_MODEL),
            stack(sp, 0), stack(ss, 0), stack(sp, 1), stack(ss, 1), stack(sp, 2), stack(ss, 2),
            stack(sp, 3), stack(ss, 3), stack(sp, 4), stack(ss, 4), stack(ss, 5))
```

```python
import functools

import jax
import jax.numpy as jnp
from jax import lax
from jax.experimental import pallas as pl
from jax.experimental.pallas import tpu as pltpu

F32 = jnp.float32
BF16 = jnp.bfloat16

D_MODEL = 1024
HEAD_DIM = 64
ROT_DIM = HEAD_DIM // 4
ROPE_THETA = 500000.0
PAGE_SIZE = 128
BRANCH_CH = 256
N_BRANCH = 4
GMLP_GROUPS = 4
GMLP_CHUNK = 128
NSA_HEADS = 4
NSA_CMP_LEN = 32
NSA_CMP_STRIDE = 16
NSA_CMP_HIDDEN = 256
NSA_SLC_BLOCK = 64
NSA_TOPN = 16
NSA_WINDOW = 512
MOBA_HEADS = 4
MOBA_BLOCK = 256
MOBA_TOPK = 3
D_FF = 2816
EPS = 1e-6
NEG = -1e30
BIG = 1e30

SLC_TILE = 512
LANES = 128
SUBLANES = 8
VMEM_LIMIT = 56 * 1024 * 1024

C_AH, C_AB, C_AC, C_U, C_V, C_G, C_END = 0, 256, 512, 768, 1024, 1280, 1408
R_Q, R_NSA, R_WIN, R_MK, R_MV, R_G, R_END = 0, 512, 768, 896, 1152, 1408, 1424
NORMED_GROUPS = (0, 1, 2, 3, 4, 5, 6, 7, 8, 10, 12, 14, 15, 16, 17)

_NT = (((1,), (1,)), ((), ()))


def _dot(a, b):
    return jnp.dot(a, b, preferred_element_type=F32)


def _dot_nt(a, b):
    return lax.dot_general(a, b, _NT, preferred_element_type=F32)


def _split(a):
    hi = a.astype(BF16)
    lo = (a - hi.astype(F32)).astype(BF16)
    return hi, lo


def _dot_3pass(a, b):
    ah, al = _split(a)
    bh, bl = _split(b)
    return _dot(ah, bh) + _dot(ah, bl) + _dot(al, bh)


def _iota(shape, dim):
    return lax.broadcasted_iota(jnp.int32, shape, dim)


def _lane_tile(a, n):
    return a if n == 1 else jnp.concatenate([a] * n, axis=1)


def _norm_rope_t(xh, gain, cos, sin):
    ss = jnp.sum(xh * xh, axis=0, keepdims=True)
    y = xh * lax.rsqrt(ss * (1.0 / HEAD_DIM) + EPS) * gain
    half = ROT_DIM // 2
    y0, y1 = y[0:half], y[half:ROT_DIM]
    return jnp.concatenate([y0 * cos - y1 * sin, y1 * cos + y0 * sin, y[ROT_DIM:]], axis=0)


def _shift_rows(z, prev1, prev2, seg):
    rows = _iota(z.shape, 0)
    z1 = pltpu.roll(z, 1, 0)
    z2 = pltpu.roll(z, 2, 0)
    if seg is None:
        z1 = jnp.where(rows == 0, prev1, z1)
        z2 = jnp.where(rows == 0, prev2, jnp.where(rows == 1, prev1, z2))
    else:
        t = rows % seg
        z1 = jnp.where(t == 0, prev1, z1)
        z2 = jnp.where(t <= 1, prev2, z2)
    return z1, z2


def _proj_kernel(*refs, tiles_per_seq, chunk, seg):
    it = iter(refs)
    x_ref, n1_ref, wtm_ref, wtr_ref, cw_ref, cb_ref, lng_ref, lnb_ref, mix_ref, mixb_ref = (next(it) for _ in range(10))
    gain_ref, cos_ref, sin_ref = (next(it) for _ in range(3))
    if seg is not None:
        h1_ref, h2_ref = next(it), next(it)
    oab_ref, qt_ref, nsat_ref, wint_ref, mobat_ref, gt_ref, gtm_ref, ztail_ref = (next(it) for _ in range(8))
    if seg is not None:
        vn_ref = next(it)
    else:
        cmp_ref, slc_ref, win_ref, mk_ref, vt_ref, kmean_ref, zprev_ref = (next(it) for _ in range(7))

    x = x_ref[...]
    tm = x.shape[0]
    xn = (x * lax.rsqrt(jnp.mean(x * x, axis=-1, keepdims=True) + EPS) * n1_ref[...]).astype(BF16)
    proj = _dot_nt(xn, wtm_ref[...])
    projt = _dot_nt(wtr_ref[...], xn)

    z = proj[:, C_AC:C_AC + 256] * proj[:, C_AH:C_AH + 256]
    if seg is None:
        @pl.when(pl.program_id(0) % tiles_per_seq == 0)
        def _():
            zprev_ref[...] = jnp.zeros_like(zprev_ref)

        z1, z2 = _shift_rows(z, zprev_ref[SUBLANES - 1:SUBLANES, :], zprev_ref[SUBLANES - 2:SUBLANES - 1, :], None)
        zprev_ref[...] = z[tm - SUBLANES:, :]
        ztail_ref[0] = z[tm - SUBLANES:, :]
    else:
        z1, z2 = _shift_rows(z, h1_ref[...], h2_ref[...], seg)
        ztail_ref[...] = z.reshape(ztail_ref.shape)
    ya = z2 * cw_ref[0:1, :] + cb_ref[...] + z1 * cw_ref[1:2, :] + z * cw_ref[2:3, :]
    oab_ref[:, 0:256] = (proj[:, C_AB:C_AB + 256] * ya).astype(BF16)

    u = jax.nn.gelu(proj[:, C_U:C_U + 256])
    v = jax.nn.gelu(proj[:, C_V:C_V + 256])
    vc = v - jnp.mean(v, axis=-1, keepdims=True)
    vn = vc * lax.rsqrt(jnp.mean(vc * vc, axis=-1, keepdims=True) + EPS) * lng_ref[...] + lnb_ref[...]
    if seg is not None:
        vn_ref[...] = vn
    vnb = vn.astype(BF16)
    grp = _iota((chunk, 256), 1) // HEAD_DIM
    for c in range(tm // chunk):
        r = _dot(mix_ref[...], vnb[c * chunk:(c + 1) * chunk, :])
        s = r[0:chunk]
        for g in range(1, GMLP_GROUPS):
            s = jnp.where(grp == g, r[g * chunk:(g + 1) * chunk], s)
        s = s + mixb_ref[...]
        oab_ref[c * chunk:(c + 1) * chunk, 256:512] = (u[c * chunk:(c + 1) * chunk, :] * s).astype(BF16)

    gtm_ref[...] = jax.nn.sigmoid(proj[:, C_G:C_END])
    gt_ref[...] = jax.nn.sigmoid(projt[R_G:R_END, :])

    cos, sin = cos_ref[...], sin_ref[...]
    rep = tm // LANES
    groups = []
    for g in range(R_MV // HEAD_DIM):
        xh = projt[g * HEAD_DIM:(g + 1) * HEAD_DIM, :]
        if g in NORMED_GROUPS:
            xh = _norm_rope_t(xh, _lane_tile(gain_ref[g * HEAD_DIM:(g + 1) * HEAD_DIM, :], rep), cos, sin)
        groups.append(xh)
    qkv = jnp.concatenate(groups, axis=0)
    qt_ref[...] = qkv[R_Q:R_NSA]
    nsat_ref[0] = qkv[R_NSA:R_WIN]
    wint_ref[0] = qkv[R_WIN:R_MK]
    mobat_ref[0, 0:256, :] = qkv[R_MK:R_MV]
    vmt = projt[R_MV:R_G, :]
    mobat_ref[0, 256:512, :] = vmt

    if seg is None:
        nsa_tm = qkv[R_NSA:R_WIN].T
        cmp_ref[...] = nsa_tm[:, 0:LANES]
        lane = _iota((tm, LANES), 1) - HEAD_DIM
        blk_in_tile = (_iota((tm, LANES), 0) // NSA_SLC_BLOCK) % (SLC_TILE // NSA_SLC_BLOCK)
        slc_ref[...] = jnp.where(lane < 0, nsa_tm[:, LANES:2 * LANES],
                                 jnp.where(lane == blk_in_tile, 1.0, 0.0)).astype(BF16)
        win_ref[...] = qkv[R_WIN:R_MK].T.astype(BF16)
        mk_tm = qkv[R_MK:R_MV].T
        mk_ref[...] = mk_tm.astype(BF16)
        vt_ref[0, 0:256, :] = vmt.astype(BF16)
        vt_ref[0, 256:320, :] = qkv[R_NSA + 192:R_NSA + 256].astype(BF16)
        vt_ref[0, 320:384, :] = qkv[R_WIN + 64:R_WIN + 128].astype(BF16)
        kmean_ref[...] = jnp.zeros_like(kmean_ref)
        for r in range(tm // MOBA_BLOCK):
            kmean_ref[0, r:r + 1, :] = jnp.sum(mk_tm[r * MOBA_BLOCK:(r + 1) * MOBA_BLOCK, :], axis=0,
                                               keepdims=True) * (1.0 / MOBA_BLOCK)


def _proj_call(x2d, lw, tables, halos, *, tm, tiles_per_seq, n_seq, chunk, seg):
    rows = x2d.shape[0]
    nt = rows // tm
    tps = tiles_per_seq
    n_out_seq = n_seq if seg is None else 1
    t_out = rows // n_out_seq
    row = lambda w: pl.BlockSpec((tm, w), lambda i: (i, 0))
    col = lambda r: pl.BlockSpec((r, tm), lambda i: (0, i))
    seq3 = lambda r: pl.BlockSpec((1, r, tm), lambda i: (i // tps, 0, i % tps))
    full = lambda a: pl.BlockSpec(a.shape, lambda i: (0,) * a.ndim)
    n_tab = tables[0].shape[1] // tm
    tab = pl.BlockSpec((ROT_DIM // 2, tm), lambda i: (0, i % n_tab))
    ws = [lw['norm1'], lw['w_tm'], lw['w_tr'], lw['conv_w'], lw['conv_b'], lw['ln_g'], lw['ln_b'], lw['mix'],
          lw['mixb'], lw['gain_t']]
    ins = [x2d] + ws + list(tables)
    in_specs = [row(D_MODEL)] + [full(a) for a in ws] + [tab] * 2
    sds = jax.ShapeDtypeStruct
    out_shape = [sds((rows, 512), BF16), sds((512, rows), F32), sds((n_out_seq, 256, t_out), F32),
                 sds((n_out_seq, 128, t_out), F32), sds((n_out_seq, 512, t_out), F32),
                 sds((2 * SUBLANES, rows), F32), sds((rows, LANES), F32), sds((n_seq, SUBLANES, 256), F32)]
    out_specs = [row(512), col(512), seq3(256), seq3(128), seq3(512), col(2 * SUBLANES), row(LANES)]
    scratch = []
    if seg is None:
        out_specs.append(pl.BlockSpec((1, SUBLANES, 256), lambda i: (i // tps, 0, 0)))
        out_shape += [sds((rows, LANES), F32), sds((rows, LANES), BF16), sds((rows, LANES), BF16),
                      sds((rows, 256), BF16), sds((n_seq, 384, t_out), BF16), sds((nt, SUBLANES, 256), F32)]
        out_specs += [row(LANES), row(LANES), row(LANES), row(256), seq3(384),
                      pl.BlockSpec((1, SUBLANES, 256), lambda i: (i, 0, 0))]
        scratch.append(pltpu.VMEM((SUBLANES, 256), F32))
    else:
        ins += list(halos)
        in_specs += [row(256), row(256)]
        out_specs.append(pl.BlockSpec((n_seq, SUBLANES, 256), lambda i: (0, 0, 0)))
        out_shape.append(sds((rows, 256), F32))
        out_specs.append(row(256))
    return pl.pallas_call(
        functools.partial(_proj_kernel, tiles_per_seq=tps, chunk=chunk, seg=seg),
        grid=(nt,), in_specs=in_specs, out_specs=out_specs, out_shape=out_shape, scratch_shapes=scratch,
        compiler_params=pltpu.CompilerParams(dimension_semantics=("arbitrary",), vmem_limit_bytes=VMEM_LIMIT),
        name="proj_prompt" if seg is None else "proj_sample",
    )(*ins)


def _compress_compute(src_ref, wa_ref, wb_ref, pt_ref, pb_ref, w2_ref, out_ref, outt_ref, groups):
    acc_a = jnp.zeros((groups, 2 * NSA_CMP_HIDDEN), F32)
    acc_b = jnp.zeros((groups, 2 * NSA_CMP_HIDDEN), F32)
    for p in range(NSA_CMP_STRIDE // 2):
        xp = jnp.concatenate([src_ref[pl.ds(2 * p, groups, stride=NSA_CMP_STRIDE), :],
                              src_ref[pl.ds(2 * p + 1, groups, stride=NSA_CMP_STRIDE), :]], axis=1)
        acc_a = acc_a + _dot((xp + pt_ref[p:p + 1, :]).astype(BF16), wa_ref[p])
        acc_b = acc_b + _dot((xp + pb_ref[p:p + 1, :]).astype(BF16), wb_ref[p])
    hdn = jax.nn.gelu(acc_a + pltpu.roll(acc_b, groups - 1, 0))
    out = _dot(hdn.astype(BF16), w2_ref[...])
    if out_ref is not None:
        out_ref[0] = out
    outt_ref[0] = out.T


def _compress_prompt_kernel(src_ref, wa_ref, wb_ref, pt_ref, pb_ref, w2_ref, out_ref, outt_ref, *, groups):
    _compress_compute(src_ref.at[0], wa_ref, wb_ref, pt_ref, pb_ref, w2_ref, out_ref, outt_ref, groups)


def _compress_paged_kernel(ptab_ref, *refs, groups, pps):
    del ptab_ref
    pages = refs[:pps]
    wa_ref, wb_ref, pt_ref, pb_ref, w2_ref, outt_ref, stage_ref = refs[pps:]
    j = pl.program_id(1)
    for k in range(pps):
        pg = pages[k][0, 0].reshape(2 * HEAD_DIM, PAGE_SIZE)
        stage_ref[pl.ds(pl.multiple_of((j * pps + k) * PAGE_SIZE, PAGE_SIZE), PAGE_SIZE), :] = pg.T

    @pl.when(j == pl.num_programs(1) - 1)
    def _():
        _compress_compute(stage_ref, wa_ref, wb_ref, pt_ref, pb_ref, w2_ref, None, outt_ref, groups)


def _compress_prompt_call(cmp_tm3, lw):
    b, t, _ = cmp_tm3.shape
    groups = t // NSA_CMP_STRIDE
    full = lambda a: pl.BlockSpec(a.shape, lambda i: (0,) * a.ndim)
    ws = [lw['cmp_wa'], lw['cmp_wb'], lw['cmp_pt'], lw['cmp_pb'], lw['cmp_w2']]
    return pl.pallas_call(
        functools.partial(_compress_prompt_kernel, groups=groups),
        grid=(b,),
        in_specs=[pl.BlockSpec((1, t, LANES), lambda i: (i, 0, 0))] + [full(a) for a in ws],
        out_specs=[pl.BlockSpec((1, groups, LANES), lambda i: (i, 0, 0)),
                   pl.BlockSpec((1, LANES, groups), lambda i: (i, 0, 0))],
        out_shape=[jax.ShapeDtypeStruct((b, groups, LANES), F32), jax.ShapeDtypeStruct((b, LANES, groups), F32)],
        compiler_params=pltpu.CompilerParams(dimension_semantics=("arbitrary",), vmem_limit_bytes=VMEM_LIMIT),
        name="compress_prompt",
    )(cmp_tm3, *ws)


def _pages_per_step(n_pages, cap):
    pps = cap
    while n_pages % pps:
        pps //= 2
    return pps


def _compress_paged_call(cache5, layer, page_table, lw):
    b, n_pages = page_table.shape
    past = n_pages * PAGE_SIZE
    groups = past // NSA_CMP_STRIDE
    pps = _pages_per_step(n_pages, 64)
    full = lambda a: pl.BlockSpec(a.shape, lambda i, j, pt: (0,) * a.ndim)
    ws = [lw['cmp_wa'], lw['cmp_wb'], lw['cmp_pt'], lw['cmp_pb'], lw['cmp_w2']]
    page_specs = [pl.BlockSpec((1, 1, 2, HEAD_DIM, PAGE_SIZE), functools.partial(
        lambda i, j, pt, k: (layer, pt[i, j * pps + k], 0, 0, 0), k=k)) for k in range(pps)]
    grid_spec = pltpu.PrefetchScalarGridSpec(
        num_scalar_prefetch=1, grid=(b, n_pages // pps),
        in_specs=page_specs + [full(a) for a in ws],
        out_specs=pl.BlockSpec((1, LANES, groups), lambda i, j, pt: (i, 0, 0)),
        scratch_shapes=[pltpu.VMEM((past, LANES), F32)])
    return pl.pallas_call(
        functools.partial(_compress_paged_kernel, groups=groups, pps=pps),
        grid_spec=grid_spec, out_shape=jax.ShapeDtypeStruct((b, LANES, groups), F32),
        compiler_params=pltpu.CompilerParams(dimension_semantics=("arbitrary", "arbitrary"),
                                             vmem_limit_bytes=VMEM_LIMIT),
        name="compress_sample",
    )(page_table, *([cache5] * pps), *ws)


def _topk_mask(score, n_cols, k):
    idx = _iota(score.shape, 1)
    rank = jnp.zeros(score.shape, jnp.int32)
    for j in range(n_cols):
        col = score[:, j:j + 1]
        rank = rank + jnp.where(col > score, 1, jnp.where(col == score, jnp.where(idx > j, 1, 0), 0))
    return rank < k


def _topk_mask_t(score, n_rows, k):
    n_tiles = score.shape[0] // SUBLANES
    tiles = [score[r * SUBLANES:(r + 1) * SUBLANES, :] for r in range(n_tiles)]
    idx = _iota(tiles[0].shape, 0)
    ranks = [jnp.zeros(tiles[0].shape, jnp.int32) for _ in range(n_tiles)]
    for j in range(n_rows):
        row = score[j:j + 1, :]
        for r in range(n_tiles):
            if r * SUBLANES > j:
                ranks[r] = jnp.where(row >= tiles[r], ranks[r] + 1, ranks[r])
            elif (r + 1) * SUBLANES - 1 <= j:
                ranks[r] = jnp.where(row > tiles[r], ranks[r] + 1, ranks[r])
            else:
                ranks[r] = ranks[r] + jnp.where(idx + r * SUBLANES > j, jnp.where(row >= tiles[r], 1, 0),
                                                jnp.where(row > tiles[r], 1, 0))
    return jnp.concatenate(ranks, axis=0) < k


def _softmax_axis(s, ok, axis):
    s = jnp.where(ok, s, NEG)
    m = jnp.max(s, axis=axis, keepdims=True)
    p = jnp.where(ok, jnp.exp(s - m), 0.0)
    l = jnp.sum(p, axis=axis, keepdims=True)
    return p / jnp.where(l > 0.0, l, 1.0)


def _forced_importance(imp, blk, cur):
    forced = (blk == 0) | (blk == cur) | (blk == cur - 1)
    imp = jnp.where(forced, BIG, imp)
    return jnp.where(blk > cur, NEG, imp)


def _nsa_prompt_kernel(qt_ref, gt_ref, kcvc_ref, kcvct_ref, slc_ref, win_ref, vt_ref, o_ref, sel_ref, s_ref, *, t, qb, tk):
    iq = pl.program_id(1)
    s0 = iq * qb
    nq = NSA_HEADS * qb
    qt = qt_ref[...] * (HEAD_DIM ** -0.5)
    qst = jnp.concatenate([qt[h * HEAD_DIM:(h + 1) * HEAD_DIM, :] for h in range(NSA_HEADS)], axis=1)
    qst = jnp.concatenate([qst, jnp.zeros_like(qst)], axis=0).astype(BF16)
    qpos_q = s0 + _iota((1, qb), 1)
    qpos = _lane_tile(qpos_q, NSA_HEADS)

    wlen = NSA_WINDOW + qb
    start = pl.multiple_of(jnp.maximum(s0 - NSA_WINDOW, 0), qb)
    sw = _dot(win_ref[0, pl.ds(start, wlen), :], qst)
    back = (qpos - start) - _iota(sw.shape, 0)
    sw = jnp.where((back & -NSA_WINDOW) == 0, sw, NEG)
    ew = jnp.exp(sw - jnp.max(sw, axis=0, keepdims=True))
    o_win = (_dot(vt_ref[0, HEAD_DIM:2 * HEAD_DIM, pl.ds(start, wlen)], ew.astype(BF16))
             / jnp.sum(ew, axis=0, keepdims=True))

    n_cmp = (t - NSA_CMP_LEN) // NSA_CMP_STRIDE + 1
    n_blk = t // NSA_SLC_BLOCK
    kcvc = kcvc_ref[0].astype(BF16)
    g = kcvc.shape[0]
    s = _dot(kcvc, qst)
    n_idx = _iota(s.shape, 0)
    ok = (n_idx * NSA_CMP_STRIDE + (NSA_CMP_LEN - 1) <= qpos) & (n_idx < n_cmp)
    s = jnp.where(ok, s, NEG)
    m = jnp.max(s, axis=0, keepdims=True)
    e = jnp.exp(s - m)
    p = e * jnp.where(m > 0.5 * NEG, 1.0 / jnp.sum(e, axis=0, keepdims=True), 0.0)
    o_cmp = _dot(kcvct_ref[0, HEAD_DIM:2 * HEAD_DIM, :].astype(BF16), p.astype(BF16))
    psum = p[:, 0:qb]
    for h in range(1, NSA_HEADS):
        psum = psum + p[:, h * qb:(h + 1) * qb]
    cj = _iota((n_blk, g), 0)
    cn = _iota((n_blk, g), 1)
    cover = ((cn * NSA_CMP_STRIDE <= cj * NSA_SLC_BLOCK + (NSA_SLC_BLOCK - 1))
             & (cn * NSA_CMP_STRIDE + (NSA_CMP_LEN - 1) >= cj * NSA_SLC_BLOCK) & (cn < n_cmp)).astype(BF16)
    p_hi, p_lo = _split(psum)
    imp = _dot(cover, p_hi) + _dot(cover, p_lo)
    imp = _forced_importance(imp, _iota(imp.shape, 0), qpos_q // NSA_SLC_BLOCK)
    cur = qpos_q // NSA_SLC_BLOCK
    picked = _topk_mask_t(imp, n_blk, NSA_TOPN) & (_iota(imp.shape, 0) <= cur)
    sel_ref[...] = _lane_tile(jnp.where(picked, 0.0, NEG), NSA_HEADS)

    bpt = tk // NSA_SLC_BLOCK
    q64 = qst[0:HEAD_DIM].astype(F32)
    zpad = jnp.zeros((LANES - HEAD_DIM - bpt, nq), F32)
    kd = s0 // tk

    def scores(kt):
        k0 = pl.multiple_of(kt * tk, tk)
        slab = sel_ref[pl.ds(pl.multiple_of(kt * bpt, bpt), bpt), :]
        w = jnp.concatenate([q64, slab, zpad], axis=0).astype(BF16)
        return _dot(slc_ref[0, pl.ds(k0, tk), :], w)

    def fold(x, op):
        return op(x.reshape(tk // SUBLANES, SUBLANES, nq), axis=0)

    def sweep1(kt, m8):
        sc = scores(kt)
        s_ref[kt] = sc
        return jnp.maximum(m8, fold(sc, jnp.max))

    m8 = lax.fori_loop(0, kd, sweep1, jnp.full((SUBLANES, nq), NEG, F32))
    sc = scores(kd)
    sc = jnp.where(kd * tk + _iota(sc.shape, 0) <= qpos, sc, NEG)
    s_ref[kd] = sc
    m = jnp.max(jnp.maximum(m8, fold(sc, jnp.max)), axis=0, keepdims=True)

    def sweep2(kt, carry):
        l8, acc = carry
        k0 = pl.multiple_of(kt * tk, tk)
        pp = jnp.exp(s_ref[kt] - m)
        return l8 + fold(pp, jnp.sum), acc + _dot(vt_ref[0, 0:HEAD_DIM, pl.ds(k0, tk)], pp.astype(BF16))

    l8, acc = lax.fori_loop(0, kd + 1, sweep2, (jnp.zeros((SUBLANES, nq), F32), jnp.zeros((HEAD_DIM, nq), F32)))
    o_slc = acc / jnp.sum(l8, axis=0, keepdims=True)

    gt = gt_ref[...]
    outs = []
    for h in range(NSA_HEADS):
        c = slice(h * qb, (h + 1) * qb)
        outs.append(gt[3 * h:3 * h + 1, :] * o_cmp[:, c] + gt[3 * h + 1:3 * h + 2, :] * o_slc[:, c]
                    + gt[3 * h + 2:3 * h + 3, :] * o_win[:, c])
    o_ref[...] = jnp.concatenate(outs, axis=0)


def _nsa_prompt_call(qt, gt, kcvc, kcvct, slc_tm3, win_tm3, vt3, *, qb=256, tk=SLC_TILE):
    b, t, _ = slc_tm3.shape
    nq = t // qb
    g = kcvc.shape[1]
    assert t % tk == 0 and tk % qb == 0 and t >= NSA_WINDOW + qb
    return pl.pallas_call(
        functools.partial(_nsa_prompt_kernel, t=t, qb=qb, tk=tk),
        grid=(b, nq),
        in_specs=[pl.BlockSpec((256, qb), lambda i, j: (0, i * nq + j)),
                  pl.BlockSpec((2 * SUBLANES, qb), lambda i, j: (0, i * nq + j)),
                  pl.BlockSpec((1, g, LANES), lambda i, j: (i, 0, 0)),
                  pl.BlockSpec((1, LANES, g), lambda i, j: (i, 0, 0)),
                  pl.BlockSpec((1, t, LANES), lambda i, j: (i, 0, 0)),
                  pl.BlockSpec((1, t, LANES), lambda i, j: (i, 0, 0)),
                  pl.BlockSpec((1, LANES, t), lambda i, j: (i, 2, 0))],
        out_specs=pl.BlockSpec((256, qb), lambda i, j: (0, i * nq + j)),
        out_shape=jax.ShapeDtypeStruct((256, b * t), F32),
        scratch_shapes=[pltpu.VMEM((t // NSA_SLC_BLOCK, NSA_HEADS * qb), F32),
                        pltpu.VMEM((t // tk, tk, NSA_HEADS * qb), F32)],
        compiler_params=pltpu.CompilerParams(dimension_semantics=("arbitrary", "arbitrary"),
                                             vmem_limit_bytes=VMEM_LIMIT),
        name="nsa_prompt",
    )(qt, gt, kcvc, kcvct, slc_tm3, win_tm3, vt3)


def _nsa_sample_kernel(ptab_ref, *refs, past, tq, pps):
    del ptab_ref
    pages = refs[:pps]
    q_ref, g_ref, kcvct_ref, new_ref, wst_ref, wnew_ref, o_ref, kt_ref, vt_ref, e_ref = refs[pps:]
    j = pl.program_id(1)

    @pl.when((pl.program_id(0) == 0) & (j == 0))
    def _():
        e_ref[...] = (_iota(e_ref.shape, 0) == _iota(e_ref.shape, 1) // NSA_SLC_BLOCK).astype(BF16)

    for k in range(pps):
        pg = pages[k][0, 0]
        c0 = pl.multiple_of((j * pps + k) * PAGE_SIZE, PAGE_SIZE)
        kt_ref[:, pl.ds(c0, PAGE_SIZE)] = pg[0].astype(BF16)
        vt_ref[:, pl.ds(c0, PAGE_SIZE)] = pg[1].astype(BF16)

    @pl.when(j == pl.num_programs(1) - 1)
    def _():
        rows = NSA_HEADS * tq
        qs = (q_ref[0] * (HEAD_DIM ** -0.5)).astype(BF16)
        qpos_q = past + _iota((tq, 1), 0)
        qpos = jnp.concatenate([qpos_q] * NSA_HEADS, axis=0)
        l_all = past + tq
        n_cmp = (l_all - NSA_CMP_LEN) // NSA_CMP_STRIDE + 1
        n_blk = -(-l_all // NSA_SLC_BLOCK)
        nb_pad = -(-n_blk // LANES) * LANES

        kcvct = kcvct_ref[0].astype(BF16)
        g = kcvct.shape[1]
        s = _dot(qs, kcvct[0:HEAD_DIM])
        n_idx = _iota(s.shape, 1)
        ok = (n_idx * NSA_CMP_STRIDE + (NSA_CMP_LEN - 1) <= qpos) & (n_idx < n_cmp)
        p = _softmax_axis(s, ok, 1)
        o_cmp = _dot_nt(p.astype(BF16), kcvct[HEAD_DIM:2 * HEAD_DIM])
        psum = p[0:tq]
        for h in range(1, NSA_HEADS):
            psum = psum + p[h * tq:(h + 1) * tq]
        cn = _iota((g, nb_pad), 0)
        cj = _iota((g, nb_pad), 1)
        cover = ((cn * NSA_CMP_STRIDE <= cj * NSA_SLC_BLOCK + (NSA_SLC_BLOCK - 1))
                 & (cn * NSA_CMP_STRIDE + (NSA_CMP_LEN - 1) >= cj * NSA_SLC_BLOCK) & (cn < n_cmp)).astype(BF16)
        p_hi, p_lo = _split(psum)
        imp = _dot(p_hi, cover) + _dot(p_lo, cover)
        imp = _forced_importance(imp, _iota(imp.shape, 1), qpos_q // NSA_SLC_BLOCK)
        picked = _topk_mask(imp, n_blk, NSA_TOPN)

        nbe = e_ref.shape[0]
        bias = jnp.where(picked[:, 0:nbe], 0.0, NEG).astype(BF16)
        bias = jnp.concatenate([bias] * NSA_HEADS, axis=0)
        sc = _dot(qs, kt_ref[...]) + _dot(bias, e_ref[...])
        new_ok = past + _iota((rows, PAGE_SIZE), 1) <= qpos
        sc_new = jnp.where(new_ok, _dot(qs, new_ref[0, 0].astype(BF16)), NEG)
        m = jnp.maximum(jnp.max(sc, axis=-1, keepdims=True), jnp.max(sc_new, axis=-1, keepdims=True))
        pp = jnp.exp(sc - m)
        pp_new = jnp.exp(sc_new - m)
        l = jnp.sum(pp, axis=-1, keepdims=True) + jnp.sum(pp_new, axis=-1, keepdims=True)
        o_slc = (_dot_nt(pp.astype(BF16), vt_ref[...]) + _dot_nt(pp_new.astype(BF16), new_ref[0, 1].astype(BF16))) / l

        kw = jnp.concatenate([wst_ref[0, 0, 0], wnew_ref[0, 0]], axis=1).astype(BF16)
        vw = jnp.concatenate([wst_ref[0, 0, 1], wnew_ref[0, 1]], axis=1).astype(BF16)
        sw = _dot(qs, kw)
        kpos = past - NSA_WINDOW + _iota(sw.shape, 1)
        okw = (kpos <= qpos) & (kpos > qpos - NSA_WINDOW) & (kpos >= 0)
        o_win = _dot_nt(_softmax_axis(sw, okw, 1).astype(BF16), vw)

        gg = g_ref[0]
        o_ref[0] = gg[:, 0:1] * o_cmp + gg[:, 1:2] * o_slc + gg[:, 2:3] * o_win


def _nsa_sample_call(cache5, layer, page_table, q32, g32, kcvct, newt, wstate5, wnewt):
    b, n_pages = page_table.shape
    past = n_pages * PAGE_SIZE
    tq = q32.shape[1] // NSA_HEADS
    pps = _pages_per_step(n_pages, 64)
    nbe = -(-(past // NSA_SLC_BLOCK) // LANES) * LANES
    assert tq <= NSA_SLC_BLOCK and past >= NSA_WINDOW
    assert (past + tq - NSA_CMP_LEN) // NSA_CMP_STRIDE + 1 <= past // NSA_CMP_STRIDE - 1
    rows = NSA_HEADS * tq
    page_specs = [pl.BlockSpec((1, 1, 2, HEAD_DIM, PAGE_SIZE), functools.partial(
        lambda i, j, pt, k: (layer, pt[i, j * pps + k], 1, 0, 0), k=k)) for k in range(pps)]
    per_b = lambda a: pl.BlockSpec((1,) + a.shape[1:], lambda i, j, pt: (i,) + (0,) * (a.ndim - 1))
    grid_spec = pltpu.PrefetchScalarGridSpec(
        num_scalar_prefetch=1, grid=(b, n_pages // pps),
        in_specs=page_specs + [per_b(q32), per_b(g32), per_b(kcvct), per_b(newt),
                               pl.BlockSpec((1, 1, 2, HEAD_DIM, NSA_WINDOW), lambda i, j, pt: (layer, i, 0, 0, 0)),
                               per_b(wnewt)],
        out_specs=pl.BlockSpec((1, rows, HEAD_DIM), lambda i, j, pt: (i, 0, 0)),
        scratch_shapes=[pltpu.VMEM((HEAD_DIM, past), BF16), pltpu.VMEM((HEAD_DIM, past), BF16),
                        pltpu.VMEM((nbe, past), BF16)])
    return pl.pallas_call(
        functools.partial(_nsa_sample_kernel, past=past, tq=tq, pps=pps),
        grid_spec=grid_spec, out_shape=jax.ShapeDtypeStruct((b, rows, HEAD_DIM), F32),
        compiler_params=pltpu.CompilerParams(dimension_semantics=("arbitrary", "arbitrary"),
                                             vmem_limit_bytes=VMEM_LIMIT),
        name="nsa_sample",
    )(page_table, *([cache5] * pps), q32, g32, kcvct, newt, wstate5, wnewt)


def _moba_prompt_kernel(qt_ref, kmean_ref, k_ref, vt_ref, o_ref, sel_ref, acc_ref, s_ref, *, nb):
    i = pl.program_id(1)
    r0 = pl.multiple_of(i * MOBA_BLOCK, MOBA_BLOCK)
    qb = MOBA_BLOCK
    nq = MOBA_HEADS * qb
    qt = qt_ref[...]
    rowgrp = _iota(qt.shape, 0) // HEAD_DIM
    kmean = kmean_ref[0]
    qpad = []
    for h in range(MOBA_HEADS):
        qh = jnp.where(rowgrp == h, qt, 0.0)
        gs = _dot_3pass(kmean, qh)
        n_idx = _iota(gs.shape, 0)
        gs = jnp.where(n_idx < i, gs, NEG)
        sel_ref[:, h * qb:(h + 1) * qb] = jnp.where(_topk_mask_t(gs, nb, MOBA_TOPK) & (n_idx < i), 0.0, NEG)
        qpad.append((qh * (HEAD_DIM ** -0.5)).astype(BF16))
    qcat = jnp.concatenate(qpad, axis=1)

    def block(n, width=1):
        k0 = pl.multiple_of(n * MOBA_BLOCK, width * MOBA_BLOCK)
        return k_ref[0, pl.ds(k0, width * MOBA_BLOCK), :], vt_ref[0, :, pl.ds(k0, width * MOBA_BLOCK)]

    def fold(x, op):
        return op(x.reshape(x.shape[0] // SUBLANES, SUBLANES, nq), axis=0)

    n_pairs = (i + 1) // 2

    def sweep1(n2, m8):
        bias = jnp.stack([sel_ref[pl.ds(2 * n2, 1), :], sel_ref[pl.ds(2 * n2 + 1, 1), :]])
        sc = (_dot(block(2 * n2, 2)[0], qcat).reshape(2, MOBA_BLOCK, nq) + bias).reshape(2 * MOBA_BLOCK, nq)
        s_ref[n2] = sc
        return jnp.maximum(m8, fold(sc, jnp.max))

    m8 = lax.fori_loop(0, n_pairs, sweep1, jnp.full((SUBLANES, nq), NEG, F32))
    own_k, own_v = block(i)
    causal = _iota((MOBA_BLOCK, nq), 0) <= _iota((MOBA_BLOCK, nq), 1) % qb
    own_s = jnp.where(causal, _dot(own_k, qcat), NEG)
    m = jnp.max(jnp.maximum(m8, fold(own_s, jnp.max)), axis=0, keepdims=True)

    acc_ref[...] = jnp.zeros_like(acc_ref)

    def accumulate(p, vb):
        pb = p.astype(BF16)
        for h in range(MOBA_HEADS):
            acc_ref[h] += _dot(vb[h * HEAD_DIM:(h + 1) * HEAD_DIM, :], pb[:, h * qb:(h + 1) * qb])
        return fold(p, jnp.sum)

    def sweep2(n2, l8):
        return l8 + accumulate(jnp.exp(s_ref[n2] - m), block(2 * n2, 2)[1])

    l8 = lax.fori_loop(0, n_pairs, sweep2, jnp.zeros((SUBLANES, nq), F32))
    l = jnp.sum(l8 + accumulate(jnp.exp(own_s - m), own_v), axis=0, keepdims=True)
    o_ref[...] = jnp.concatenate([acc_ref[h] / l[:, h * qb:(h + 1) * qb] for h in range(MOBA_HEADS)], axis=0)


def _moba_prompt_call(qt, kmean3, mk_tm3, vt3):
    b, t, _ = mk_tm3.shape
    nb = t // MOBA_BLOCK
    nbp = kmean3.shape[1]
    return pl.pallas_call(
        functools.partial(_moba_prompt_kernel, nb=nb),
        grid=(b, nb),
        in_specs=[pl.BlockSpec((256, MOBA_BLOCK), lambda i, j: (1, i * nb + j)),
                  pl.BlockSpec((1, nbp, 256), lambda i, j: (i, 0, 0)),
                  pl.BlockSpec((1, t, 256), lambda i, j: (i, 0, 0)),
                  pl.BlockSpec((1, 256, t), lambda i, j: (i, 0, 0))],
        out_specs=pl.BlockSpec((256, MOBA_BLOCK), lambda i, j: (0, i * nb + j)),
        out_shape=jax.ShapeDtypeStruct((256, b * t), F32),
        scratch_shapes=[pltpu.VMEM((nbp, MOBA_HEADS * MOBA_BLOCK), F32),
                        pltpu.VMEM((MOBA_HEADS, HEAD_DIM, MOBA_BLOCK), F32),
                        pltpu.VMEM((nb // 2, 2 * MOBA_BLOCK, MOBA_HEADS * MOBA_BLOCK), F32)],
        compiler_params=pltpu.CompilerParams(dimension_semantics=("arbitrary", "arbitrary"),
                                             vmem_limit_bytes=VMEM_LIMIT),
        name="moba_prompt",
    )(qt, kmean3, mk_tm3, vt3)


def _moba_sample_kernel(ptab_ref, *refs, past, tq, n_pages):
    del ptab_ref
    pages = refs[:n_pages]
    q_ref, new_ref, o_ref, s_ref, e_ref = refs[n_pages:]
    ppb = MOBA_BLOCK // PAGE_SIZE
    n_past = past // MOBA_BLOCK

    @pl.when(pl.program_id(0) == 0)
    def _():
        e_ref[...] = (_iota(e_ref.shape, 0) == _iota(e_ref.shape, 1) // MOBA_BLOCK).astype(BF16)

    q = q_ref[0]
    rows = q.shape[0]
    qb = (q * (HEAD_DIM ** -0.5)).astype(BF16)

    def block(n, which):
        return jnp.concatenate([pages[n * ppb + c][0, 0, which] for c in range(ppb)], axis=1)

    lane = _iota((256, LANES), 1)
    kmean = jnp.zeros((256, LANES), F32)
    for n in range(n_past):
        kb = block(n, 0)
        kmean = jnp.where(lane == n, jnp.sum(kb, axis=1, keepdims=True) * (1.0 / MOBA_BLOCK), kmean)
        s_ref[:, n * MOBA_BLOCK:(n + 1) * MOBA_BLOCK] = _dot(qb, kb.astype(BF16))
    gs = _dot_3pass(q, kmean)
    n_idx = _iota(gs.shape, 1)
    gs = jnp.where(n_idx < n_past, gs, NEG)
    bias = jnp.where(_topk_mask(gs, n_past, MOBA_TOPK) & (n_idx < n_past), 0.0, NEG).astype(BF16)
    s = s_ref[...] + _dot(bias, e_ref[...])
    own_ok = _iota((rows, PAGE_SIZE), 1) <= _iota((rows, PAGE_SIZE), 0) % tq
    s_own = jnp.where(own_ok, _dot(qb, new_ref[0, 0].astype(BF16)), NEG)
    m = jnp.maximum(jnp.max(s, axis=-1, keepdims=True), jnp.max(s_own, axis=-1, keepdims=True))
    p = jnp.exp(s - m)
    p_own = jnp.exp(s_own - m)
    l = jnp.sum(p, axis=-1, keepdims=True) + jnp.sum(p_own, axis=-1, keepdims=True)
    pb = p.astype(BF16)
    acc = _dot_nt(p_own.astype(BF16), new_ref[0, 1].astype(BF16))
    for n in range(n_past):
        acc = acc + _dot_nt(pb[:, n * MOBA_BLOCK:(n + 1) * MOBA_BLOCK], block(n, 1).astype(BF16))
    oh = acc / l
    grp = _iota((tq, 256), 1) // HEAD_DIM
    out = oh[0:tq]
    for h in range(1, MOBA_HEADS):
        out = jnp.where(grp == h, oh[h * tq:(h + 1) * tq], out)
    o_ref[0] = out


def _moba_sample_call(cache5, layer, page_table, qbd, newt):
    b, n_pages = page_table.shape
    past = n_pages * PAGE_SIZE
    rows = qbd.shape[1]
    tq = rows // MOBA_HEADS
    assert past % MOBA_BLOCK == 0 and tq <= PAGE_SIZE and past // MOBA_BLOCK <= LANES
    page_specs = [pl.BlockSpec((1, 1, 2, 256, PAGE_SIZE), functools.partial(
        lambda i, pt, k: (layer, pt[i, k], 0, 0, 0), k=k)) for k in range(n_pages)]
    per_b = lambda a: pl.BlockSpec((1,) + a.shape[1:], lambda i, pt: (i,) + (0,) * (a.ndim - 1))
    grid_spec = pltpu.PrefetchScalarGridSpec(
        num_scalar_prefetch=1, grid=(b,),
        in_specs=page_specs + [per_b(qbd), per_b(newt)],
        out_specs=pl.BlockSpec((1, tq, 256), lambda i, pt: (i, 0, 0)),
        scratch_shapes=[pltpu.VMEM((rows, past), F32), pltpu.VMEM((LANES, past), BF16)])
    return pl.pallas_call(
        functools.partial(_moba_sample_kernel, past=past, tq=tq, n_pages=n_pages),
        grid_spec=grid_spec, out_shape=jax.ShapeDtypeStruct((b, tq, 256), F32),
        compiler_params=pltpu.CompilerParams(dimension_semantics=("arbitrary",), vmem_limit_bytes=VMEM_LIMIT),
        name="moba_sample",
    )(page_table, *([cache5] * n_pages), qbd, newt)


def _merge_kernel(x_ref, oab_ref, oct_ref, odt_ref, n1_ref, wgt_ref, wbr_ref, wo_ref, y_ref):
    x = x_ref[...]
    xn = (x * lax.rsqrt(jnp.mean(x * x, axis=-1, keepdims=True) + EPS) * n1_ref[...]).astype(BF16)
    branches = [oab_ref[:, 0:256], oab_ref[:, 256:512], oct_ref[...].T.astype(BF16), odt_ref[...].T.astype(BF16)]
    merged = jnp.zeros(x.shape, F32)
    for n in range(N_BRANCH):
        g = jax.nn.sigmoid(_dot_nt(xn, wgt_ref[n * D_MODEL:(n + 1) * D_MODEL, :]))
        merged = merged + g * _dot(branches[n], wbr_ref[n])
    y_ref[...] = x + _dot(merged.astype(BF16), wo_ref[...])


def _merge_call(x2d, oab, oct, odt, lw, *, tm):
    rows = x2d.shape[0]
    row = lambda w: pl.BlockSpec((tm, w), lambda i: (i, 0))
    col = lambda r: pl.BlockSpec((r, tm), lambda i: (0, i))
    full = lambda a: pl.BlockSpec(a.shape, lambda i: (0,) * a.ndim)
    ws = [lw['norm1'], lw['wg_t'], lw['w_branch'], lw['w_o']]
    return pl.pallas_call(
        _merge_kernel, grid=(rows // tm,),
        in_specs=[row(D_MODEL), row(512), col(256), col(256)] + [full(a) for a in ws],
        out_specs=row(D_MODEL), out_shape=jax.ShapeDtypeStruct((rows, D_MODEL), F32),
        compiler_params=pltpu.CompilerParams(dimension_semantics=("arbitrary",), vmem_limit_bytes=VMEM_LIMIT),
        name="merge",
    )(x2d, oab, oct, odt, *ws)


def _ffn_kernel(*refs, tiles_per_seq, seg):
    it = iter(refs)
    x_ref, n2_ref, wgate_ref, wup_ref, cw_ref, cb_ref, wdown_ref = (next(it) for _ in range(7))
    if seg is not None:
        h1_ref, h2_ref = next(it), next(it)
    y_ref, tail_ref = next(it), next(it)
    if seg is None:
        prev_ref = next(it)

    x = x_ref[...]
    tm = x.shape[0]
    h2 = (x * lax.rsqrt(jnp.mean(x * x, axis=-1, keepdims=True) + EPS) * n2_ref[...]).astype(BF16)
    gate = _dot(h2, wgate_ref[...])
    if seg is None:
        @pl.when(pl.program_id(0) % tiles_per_seq == 0)
        def _():
            prev_ref[...] = jnp.zeros_like(prev_ref)

        g1, g2 = _shift_rows(gate, prev_ref[SUBLANES - 1:SUBLANES, :], prev_ref[SUBLANES - 2:SUBLANES - 1, :], None)
        prev_ref[...] = gate[tm - SUBLANES:, :]
        tail_ref[0] = gate[tm - SUBLANES:, :]
    else:
        g1, g2 = _shift_rows(gate, h1_ref[...], h2_ref[...], seg)
        tail_ref[...] = gate.reshape(tail_ref.shape)
    ac = g2 * cw_ref[0:1, :] + cb_ref[...] + g1 * cw_ref[1:2, :] + gate * cw_ref[2:3, :]
    act = jax.nn.silu(ac) * _dot(h2, wup_ref[...])
    y_ref[...] = x + _dot(act.astype(BF16), wdown_ref[...])


def _ffn_call(x2d, lw, halos, *, tm, tiles_per_seq, n_seq, seg):
    rows = x2d.shape[0]
    row = lambda w: pl.BlockSpec((tm, w), lambda i: (i, 0))
    full = lambda a: pl.BlockSpec(a.shape, lambda i: (0,) * a.ndim)
    ws = [lw['norm2'], lw['w_gate'], lw['w_up'], lw['ffn_conv_w'], lw['ffn_conv_b'], lw['w_down']]
    ins = [x2d] + ws
    in_specs = [row(D_MODEL)] + [full(a) for a in ws]
    scratch = []
    if seg is None:
        tail_spec = pl.BlockSpec((1, SUBLANES, D_FF), lambda i: (i // tiles_per_seq, 0, 0))
        scratch.append(pltpu.VMEM((SUBLANES, D_FF), F32))
    else:
        ins += list(halos)
        in_specs += [row(D_FF), row(D_FF)]
        tail_spec = pl.BlockSpec((n_seq, SUBLANES, D_FF), lambda i: (0, 0, 0))
    return pl.pallas_call(
        functools.partial(_ffn_kernel, tiles_per_seq=tiles_per_seq, seg=seg),
        grid=(rows // tm,), in_specs=in_specs, out_specs=[row(D_MODEL), tail_spec],
        out_shape=[jax.ShapeDtypeStruct((rows, D_MODEL), F32), jax.ShapeDtypeStruct((n_seq, SUBLANES, D_FF), F32)],
        scratch_shapes=scratch,
        compiler_params=pltpu.CompilerParams(dimension_semantics=("arbitrary",), vmem_limit_bytes=VMEM_LIMIT),
        name="ffn_prompt" if seg is None else "ffn_sample",
    )(*ins)


def _rope_tables_t(pos):
    half = ROT_DIM // 2
    inv = jnp.power(jnp.float32(ROPE_THETA), -jnp.arange(half, dtype=F32) / half)
    ang = inv[:, None] * pos.astype(F32)[None, :]
    return jnp.cos(ang), jnp.sin(ang)


def _layer_weights(l, p, sample_tq, dec_batch):
    wt = jnp.transpose(p['w_in'], (2, 0, 1))[:, l, :]
    o_ck, o_cv, o_g = 1536, 1728, 1920
    o_dq = o_g + 3 * NSA_HEADS
    o_br = o_dq + 3 * 256
    hd = HEAD_DIM
    w_tm = jnp.concatenate([wt[0:1280], wt[o_g:o_g + 12], jnp.zeros((LANES - 12, D_MODEL), F32)], axis=0)
    w_tr = jnp.concatenate([
        wt[1280:1536], wt[o_dq:o_dq + 256],
        wt[o_ck:o_ck + hd], wt[o_cv:o_cv + hd], wt[o_ck + hd:o_ck + 2 * hd], wt[o_cv + hd:o_cv + 2 * hd],
        wt[o_ck + 2 * hd:o_ck + 3 * hd], wt[o_cv + 2 * hd:o_cv + 3 * hd],
        wt[o_dq + 256:o_dq + 768], wt[o_g:o_g + 12], jnp.zeros((2 * SUBLANES - 12, D_MODEL), F32)], axis=0)
    lw = {'w_tm': w_tm.astype(BF16), 'w_tr': w_tr.astype(BF16), 'wg_t': wt[o_br:].astype(BF16)}
    row = lambda a: a.reshape(1, -1)
    lw['norm1'] = row(p['norm1'][l])
    lw['norm2'] = row(p['norm2'][l])
    lw['conv_w'] = p['conv_w'][l]
    lw['conv_b'] = row(p['conv_b'][l])
    lw['ln_g'] = row(p['gmlp_ln_g'][l])
    lw['ln_b'] = row(p['gmlp_ln_b'][l])
    tril = jnp.tril(jnp.ones((GMLP_CHUNK, GMLP_CHUNK), bool))
    ws = jnp.where(tril[None], p['gmlp_ws'][l], 0.0)
    bs = p['gmlp_bs'][l]
    lw['mix_p'] = ws.reshape(GMLP_GROUPS * GMLP_CHUNK, GMLP_CHUNK).astype(BF16)
    lw['mixb_p'] = jnp.repeat(bs.T, HEAD_DIM, axis=1)
    eye = jnp.eye(dec_batch, dtype=F32)
    lw['mix_s'] = jnp.concatenate([jnp.kron(eye, ws[g, :sample_tq, :sample_tq]) for g in range(GMLP_GROUPS)],
                                  axis=0).astype(BF16)
    lw['mixb_s'] = jnp.tile(jnp.repeat(bs.T[:sample_tq], HEAD_DIM, axis=1), (dec_batch, 1))
    kg = p['nsa_k_gain'][l]
    one = jnp.ones((hd,), F32)
    gains = jnp.concatenate([jnp.tile(p['nsa_q_gain'][l], NSA_HEADS), jnp.tile(p['moba_q_gain'][l], MOBA_HEADS),
                             kg[0], one, kg[1], one, kg[2], one, jnp.tile(p['moba_k_gain'][l], MOBA_HEADS)])
    lw['gain_t'] = jnp.broadcast_to(gains[:, None], (R_MV, LANES))
    w1c = p['nsa_cmp_w1'][l].reshape(2, NSA_CMP_LEN, HEAD_DIM, NSA_CMP_HIDDEN)
    pos = p['nsa_cmp_pos'][l]
    zero = jnp.zeros((HEAD_DIM, NSA_CMP_HIDDEN), F32)

    def pair_w(r_off):
        mats = []
        for pp in range(NSA_CMP_STRIDE // 2):
            blocks = []
            for r in (2 * pp, 2 * pp + 1):
                blocks.append(jnp.concatenate([w1c[0, r + r_off], zero], axis=1))
                blocks.append(jnp.concatenate([zero, w1c[1, r + r_off]], axis=1))
            mats.append(jnp.concatenate(blocks, axis=0))
        return jnp.stack(mats).astype(BF16)

    def pair_pos(r_off):
        return jnp.stack([jnp.concatenate([pos[0, 2 * pp + r_off], pos[1, 2 * pp + r_off],
                                           pos[0, 2 * pp + 1 + r_off], pos[1, 2 * pp + 1 + r_off]])
                          for pp in range(NSA_CMP_STRIDE // 2)])

    lw['cmp_wa'], lw['cmp_wb'] = pair_w(0), pair_w(NSA_CMP_STRIDE)
    lw['cmp_pt'], lw['cmp_pb'] = pair_pos(0), pair_pos(NSA_CMP_STRIDE)
    w2 = p['nsa_cmp_w2'][l]
    z2 = jnp.zeros((NSA_CMP_HIDDEN, HEAD_DIM), F32)
    lw['cmp_w2'] = jnp.concatenate([jnp.concatenate([w2[0], z2], axis=1),
                                    jnp.concatenate([z2, w2[1]], axis=1)], axis=0).astype(BF16)
    lw['w_branch'] = p['w_branch'][l].astype(BF16)
    lw['w_o'] = p['w_o'][l].astype(BF16)
    lw['w_gate'] = p['w_ffn_gate'][l].astype(BF16)
    lw['w_up'] = p['w_ffn_up'][l].astype(BF16)
    lw['w_down'] = p['w_ffn_down'][l].astype(BF16)
    lw['ffn_conv_w'] = p['ffn_conv_w'][l]
    lw['ffn_conv_b'] = row(p['ffn_conv_b'][l])
    return lw


def _halos(buf, tq):
    b, _, c = buf.shape
    z = jnp.zeros((b, tq - 1, c), F32)
    h1 = jnp.concatenate([buf[:, 1:2], z], axis=1)
    h2 = jnp.concatenate([buf[:, 0:2], z[:, 1:]], axis=1)
    return h1.reshape(b * tq, c), h2.reshape(b * tq, c)


def _row_tile(rows, cap):
    tm = cap
    while rows % tm:
        tm //= 2
    return tm


def _per_seq_pages(xt, b, tq):
    c = xt.shape[0]
    x = xt.reshape(c, b, tq).transpose(1, 0, 2)
    return jnp.concatenate([x, jnp.zeros((b, c, PAGE_SIZE - tq), F32)], axis=-1)


def _prompt_layer(x2d, lw, tables, b, t):
    tm = _row_tile(t, 512)
    assert tm % MOBA_BLOCK == 0 and tm % SLC_TILE == 0
    tps = t // tm
    pw = dict(lw, mix=lw['mix_p'], mixb=lw['mixb_p'])
    (oab, qt, nsat, wint, mobat, gt, _, ztail, cmp_tm, slc_tm, win_tm, mk_tm, vt16, kmean) = _proj_call(
        x2d, pw, tables, None, tm=tm, tiles_per_seq=tps, n_seq=b, chunk=GMLP_CHUNK, seg=None)
    kcvc, kcvct = _compress_prompt_call(cmp_tm.reshape(b, t, LANES), lw)
    o_ct = _nsa_prompt_call(qt, gt, kcvc, kcvct, slc_tm.reshape(b, t, LANES), win_tm.reshape(b, t, LANES), vt16)
    nb = t // MOBA_BLOCK
    kmean3 = kmean[:, 0:tm // MOBA_BLOCK].reshape(b, nb, 256)
    nbp = -(-nb // SUBLANES) * SUBLANES
    if nbp != nb:
        kmean3 = jnp.concatenate([kmean3, jnp.zeros((b, nbp - nb, 256), F32)], axis=1)
    o_dt = _moba_prompt_call(qt, kmean3, mk_tm.reshape(b, t, 256), vt16)
    x_mid = _merge_call(x2d, oab, o_ct, o_dt, lw, tm=tm)
    y, ftail = _ffn_call(x_mid, lw, None, tm=tm, tiles_per_seq=tps, n_seq=b, seg=None)
    hd = HEAD_DIM
    wlen = min(NSA_WINDOW, t)
    state = (nsat.reshape(b, 4, hd, t).transpose(0, 3, 1, 2),
             mobat.reshape(b, 2, MOBA_HEADS, hd, t).transpose(0, 4, 1, 2, 3),
             wint[:, :, t - wlen:].reshape(b, 2, hd, wlen).transpose(0, 3, 1, 2),
             ztail[:, SUBLANES - 2:], ftail[:, SUBLANES - 2:])
    return y, state


def _sample_layer(x2d, lw, tables, b, tq, layer, nsa_cache5, moba_cache5, page_table, win_state5, conv_state,
                  ffn_state):
    rows = b * tq
    hd = HEAD_DIM
    pw = dict(lw, mix=lw['mix_s'], mixb=lw['mixb_s'])
    oab, qt, nsat, wint, mobat, _, gtm, ztail, vn = _proj_call(
        x2d, pw, tables, _halos(conv_state, tq), tm=rows, tiles_per_seq=1, n_seq=b, chunk=rows, seg=tq)
    nsat, wint, mobat = nsat[0], wint[0], mobat[0]
    kcvct = _compress_paged_call(nsa_cache5, layer, page_table, lw)
    q32 = qt[0:256].reshape(NSA_HEADS, hd, b, tq).transpose(2, 0, 3, 1).reshape(b, NSA_HEADS * tq, hd)
    g4 = gtm[:, 0:3 * NSA_HEADS].reshape(b, tq, NSA_HEADS, 3).transpose(0, 2, 1, 3).reshape(b, NSA_HEADS * tq, 3)
    g32 = jnp.concatenate([g4, jnp.zeros((b, NSA_HEADS * tq, LANES - 3), F32)], axis=-1)
    newt = _per_seq_pages(nsat[2 * hd:4 * hd], b, tq).reshape(b, 2, hd, PAGE_SIZE)
    wnewt = _per_seq_pages(wint, b, tq).reshape(b, 2, hd, PAGE_SIZE)
    o32 = _nsa_sample_call(nsa_cache5, layer, page_table, q32, g32, kcvct, newt, win_state5, wnewt)
    o_ct = o32.reshape(b, NSA_HEADS, tq, hd).transpose(1, 3, 0, 2).reshape(256, rows)
    qm = qt[256:512].reshape(MOBA_HEADS, hd, b, tq).transpose(2, 0, 3, 1)
    eye = jnp.eye(MOBA_HEADS, dtype=F32)
    qbd = (qm[:, :, :, None, :] * eye[None, :, None, :, None]).reshape(b, MOBA_HEADS * tq, 256)
    mnewt = _per_seq_pages(mobat, b, tq).reshape(b, 2, 256, PAGE_SIZE)
    o_dt = _moba_sample_call(moba_cache5, layer, page_table, qbd, mnewt).transpose(2, 0, 1).reshape(256, rows)
    x_mid = _merge_call(x2d, oab, o_ct, o_dt, lw, tm=rows)
    y, ftail = _ffn_call(x_mid, lw, _halos(ffn_state, tq), tm=rows, tiles_per_seq=1, n_seq=b, seg=tq)
    wnew = wint.reshape(2, hd, b, tq).transpose(2, 0, 1, 3)
    win_all = jnp.concatenate([win_state5[layer], wnew], axis=-1)
    win_new = win_all[..., win_all.shape[-1] - NSA_WINDOW:].transpose(0, 3, 1, 2)
    state = (nsat.T.reshape(b, tq, 4, hd), mobat.T.reshape(b, tq, 2, MOBA_HEADS, hd), win_new,
             ztail[:, tq - 2:tq], ftail[:, tq - 2:tq], vn.reshape(b, tq, 256))
    return y, state


def kernel(x_prompt, x_sample, cache_nsa_kv, cache_moba_kv, state_nsa_win, state_conv, state_ffn_conv, page_table, norm1, w_in, conv_w, conv_b, gmlp_ln_g, gmlp_ln_b, gmlp_ws, gmlp_bs, nsa_q_gain, nsa_k_gain, nsa_cmp_pos, nsa_cmp_w1, nsa_cmp_w2, moba_q_gain, moba_k_gain, w_branch, w_o, norm2, w_ffn_gate, w_ffn_up, ffn_conv_w, ffn_conv_b, w_ffn_down):
    params = dict(norm1=norm1, w_in=w_in, conv_w=conv_w, conv_b=conv_b, gmlp_ln_g=gmlp_ln_g, gmlp_ln_b=gmlp_ln_b,
                  gmlp_ws=gmlp_ws, gmlp_bs=gmlp_bs, nsa_q_gain=nsa_q_gain, nsa_k_gain=nsa_k_gain,
                  nsa_cmp_pos=nsa_cmp_pos, nsa_cmp_w1=nsa_cmp_w1, nsa_cmp_w2=nsa_cmp_w2, moba_q_gain=moba_q_gain,
                  moba_k_gain=moba_k_gain, w_branch=w_branch, w_o=w_o, norm2=norm2, w_ffn_gate=w_ffn_gate,
                  w_ffn_up=w_ffn_up, ffn_conv_w=ffn_conv_w, ffn_conv_b=ffn_conv_b, w_ffn_down=w_ffn_down)
    bp, tp, _ = x_prompt.shape
    bs, ts, _ = x_sample.shape
    depth = w_in.shape[0]
    n_pool = cache_nsa_kv.shape[1]
    past = page_table.shape[1] * PAGE_SIZE
    assert ts == SUBLANES and state_nsa_win.shape[2] == NSA_WINDOW
    nsa_cache5 = jnp.transpose(cache_nsa_kv, (0, 1, 3, 4, 2))
    moba_cache5 = jnp.transpose(cache_moba_kv, (0, 1, 3, 4, 5, 2)).reshape(depth, n_pool, 2, 256, PAGE_SIZE)
    win_state5 = jnp.transpose(state_nsa_win, (0, 1, 3, 4, 2))
    tab_p = _rope_tables_t(jnp.arange(tp, dtype=jnp.int32))
    tab_s = tuple(jnp.tile(a, (1, bs)) for a in _rope_tables_t(past + jnp.arange(ts, dtype=jnp.int32)))
    yp = x_prompt.reshape(bp * tp, D_MODEL)
    ys = x_sample.reshape(bs * ts, D_MODEL)
    sp, ss = [], []
    for l in range(depth):
        lw = _layer_weights(l, params, ts, bs)
        yp, st_p = _prompt_layer(yp, lw, tab_p, bp, tp)
        ys, st_s = _sample_layer(ys, lw, tab_s, bs, ts, l, nsa_cache5, moba_cache5, page_table, win_state5,
                                 state_conv[l], state_ffn_conv[l])
        sp.append(st_p)
        ss.append(st_s)
    stack = lambda lst, k: jnp.stack([s[k] for s in lst])
    return (yp.reshape(bp, tp, D_MODEL), ys.reshape(bs, ts, D_MODEL),
            stack(sp, 0), stack(ss, 0), stack(sp, 1), stack(ss, 1), stack(sp, 2), stack(ss, 2),
            stack(sp, 3), stack(ss, 3), stack(sp, 4), stack(ss, 4), stack(ss, 5))
```

```python
import functools

import jax
import jax.numpy as jnp
from jax import lax
from jax.experimental import pallas as pl
from jax.experimental.pallas import tpu as pltpu

F32 = jnp.float32
BF16 = jnp.bfloat16

D_MODEL = 1024
HEAD_DIM = 64
ROT_DIM = HEAD_DIM // 4
ROPE_THETA = 500000.0
PAGE_SIZE = 128
BRANCH_CH = 256
N_BRANCH = 4
GMLP_GROUPS = 4
GMLP_CHUNK = 128
NSA_HEADS = 4
NSA_CMP_LEN = 32
NSA_CMP_STRIDE = 16
NSA_CMP_HIDDEN = 256
NSA_SLC_BLOCK = 64
NSA_TOPN = 16
NSA_WINDOW = 512
MOBA_HEADS = 4
MOBA_BLOCK = 256
MOBA_TOPK = 3
D_FF = 2816
EPS = 1e-6
NEG = -1e30
BIG = 1e30

SLC_TILE = 512
LANES = 128
SUBLANES = 8
VMEM_LIMIT = 56 * 1024 * 1024

C_AH, C_AB, C_AC, C_U, C_V, C_G, C_END = 0, 256, 512, 768, 1024, 1280, 1408
R_Q, R_NSA, R_WIN, R_MK, R_MV, R_G, R_END = 0, 512, 768, 896, 1152, 1408, 1424
NORMED_GROUPS = (0, 1, 2, 3, 4, 5, 6, 7, 8, 10, 12, 14, 15, 16, 17)

_NT = (((1,), (1,)), ((), ()))


def _dot(a, b):
    return jnp.dot(a, b, preferred_element_type=F32)


def _dot_nt(a, b):
    return lax.dot_general(a, b, _NT, preferred_element_type=F32)


def _split(a):
    hi = a.astype(BF16)
    lo = (a - hi.astype(F32)).astype(BF16)
    return hi, lo


def _dot_3pass(a, b):
    ah, al = _split(a)
    bh, bl = _split(b)
    return _dot(ah, bh) + _dot(ah, bl) + _dot(al, bh)


def _iota(shape, dim):
    return lax.broadcasted_iota(jnp.int32, shape, dim)


def _lane_tile(a, n):
    return a if n == 1 else jnp.concatenate([a] * n, axis=1)


def _norm_rope_t(xh, gain, cos, sin):
    ss = jnp.sum(xh * xh, axis=0, keepdims=True)
    y = xh * lax.rsqrt(ss * (1.0 / HEAD_DIM) + EPS) * gain
    half = ROT_DIM // 2
    y0, y1 = y[0:half], y[half:ROT_DIM]
    return jnp.concatenate([y0 * cos - y1 * sin, y1 * cos + y0 * sin, y[ROT_DIM:]], axis=0)


def _shift_rows(z, prev1, prev2, seg):
    rows = _iota(z.shape, 0)
    z1 = pltpu.roll(z, 1, 0)
    z2 = pltpu.roll(z, 2, 0)
    if seg is None:
        z1 = jnp.where(rows == 0, prev1, z1)
        z2 = jnp.where(rows == 0, prev2, jnp.where(rows == 1, prev1, z2))
    else:
        t = rows % seg
        z1 = jnp.where(t == 0, prev1, z1)
        z2 = jnp.where(t <= 1, prev2, z2)
    return z1, z2


def _proj_kernel(*refs, tiles_per_seq, chunk, seg, subs):
    it = iter(refs)
    x_ref, n1_ref, wtm_ref, wtr_ref, cw_ref, cb_ref, lng_ref, lnb_ref, mix_ref, mixb_ref = (next(it) for _ in range(10))
    gain_ref, cos_ref, sin_ref = (next(it) for _ in range(3))
    if seg is not None:
        h1_ref, h2_ref = next(it), next(it)
    oab_ref, qt_ref, nsat_ref, wint_ref, mobat_ref, gt_ref, gtm_ref, ztail_ref = (next(it) for _ in range(8))
    if seg is not None:
        vn_ref = next(it)
    else:
        cmp_ref, slc_ref, win_ref, mk_ref, vt_ref, kmean_ref, zprev_ref = (next(it) for _ in range(7))

    tm = x_ref.shape[0]
    sm = tm // subs
    if seg is None:
        @pl.when(pl.program_id(0) % tiles_per_seq == 0)
        def _():
            zprev_ref[...] = jnp.zeros_like(zprev_ref)

        prev = zprev_ref[...]
        kmean_ref[...] = jnp.zeros_like(kmean_ref)

    for si in range(subs):
        r0 = si * sm
        rs = slice(r0, r0 + sm)
        x = x_ref[rs, :]
        xn = (x * lax.rsqrt(jnp.mean(x * x, axis=-1, keepdims=True) + EPS) * n1_ref[...]).astype(BF16)
        proj = _dot_nt(xn, wtm_ref[...])
        projt = _dot_nt(wtr_ref[...], xn)

        z = proj[:, C_AC:C_AC + 256] * proj[:, C_AH:C_AH + 256]
        if seg is None:
            z1, z2 = _shift_rows(z, prev[SUBLANES - 1:SUBLANES, :], prev[SUBLANES - 2:SUBLANES - 1, :], None)
            prev = z[sm - SUBLANES:, :]
        else:
            z1, z2 = _shift_rows(z, h1_ref[...], h2_ref[...], seg)
            ztail_ref[...] = z.reshape(ztail_ref.shape)
        ya = z2 * cw_ref[0:1, :] + cb_ref[...] + z1 * cw_ref[1:2, :] + z * cw_ref[2:3, :]
        oab_ref[rs, 0:256] = (proj[:, C_AB:C_AB + 256] * ya).astype(BF16)

        u = jax.nn.gelu(proj[:, C_U:C_U + 256])
        v = jax.nn.gelu(proj[:, C_V:C_V + 256])
        vc = v - jnp.mean(v, axis=-1, keepdims=True)
        vn = vc * lax.rsqrt(jnp.mean(vc * vc, axis=-1, keepdims=True) + EPS) * lng_ref[...] + lnb_ref[...]
        if seg is not None:
            vn_ref[...] = vn
        vnb = vn.astype(BF16)
        grp = _iota((chunk, 256), 1) // HEAD_DIM
        for c in range(sm // chunk):
            r = _dot(mix_ref[...], vnb[c * chunk:(c + 1) * chunk, :])
            s = r[0:chunk]
            for g in range(1, GMLP_GROUPS):
                s = jnp.where(grp == g, r[g * chunk:(g + 1) * chunk], s)
            s = s + mixb_ref[...]
            oab_ref[r0 + c * chunk:r0 + (c + 1) * chunk, 256:512] = (u[c * chunk:(c + 1) * chunk, :] * s).astype(BF16)

        gtm_ref[rs, :] = jax.nn.sigmoid(proj[:, C_G:C_END])
        gt_ref[:, rs] = jax.nn.sigmoid(projt[R_G:R_END, :])

        cos, sin = cos_ref[:, rs], sin_ref[:, rs]
        rep = sm // LANES
        groups = []
        for g in range(R_MV // HEAD_DIM):
            xh = projt[g * HEAD_DIM:(g + 1) * HEAD_DIM, :]
            if g in NORMED_GROUPS:
                xh = _norm_rope_t(xh, _lane_tile(gain_ref[g * HEAD_DIM:(g + 1) * HEAD_DIM, :], rep), cos, sin)
            groups.append(xh)
        qkv = jnp.concatenate(groups, axis=0)
        qt_ref[:, rs] = qkv[R_Q:R_NSA]
        nsat_ref[0, :, rs] = qkv[R_NSA:R_WIN]
        wint_ref[0, :, rs] = qkv[R_WIN:R_MK]
        mobat_ref[0, 0:256, rs] = qkv[R_MK:R_MV]
        vmt = projt[R_MV:R_G, :]
        mobat_ref[0, 256:512, rs] = vmt

        if seg is None:
            nsa_tm = qkv[R_NSA:R_WIN].T
            cmp_ref[rs, :] = nsa_tm[:, 0:LANES]
            lane = _iota((sm, LANES), 1) - HEAD_DIM
            blk_in_tile = ((r0 + _iota((sm, LANES), 0)) // NSA_SLC_BLOCK) % (SLC_TILE // NSA_SLC_BLOCK)
            slc_ref[rs, :] = jnp.where(lane < 0, nsa_tm[:, LANES:2 * LANES],
                                       jnp.where(lane == blk_in_tile, 1.0, 0.0)).astype(BF16)
            win_ref[rs, :] = qkv[R_WIN:R_MK].T.astype(BF16)
            mk_tm = qkv[R_MK:R_MV].T
            mk_ref[rs, :] = mk_tm.astype(BF16)
            vt_ref[0, 0:256, rs] = vmt.astype(BF16)
            vt_ref[0, 256:320, rs] = qkv[R_NSA + 192:R_NSA + 256].astype(BF16)
            vt_ref[0, 320:384, rs] = qkv[R_WIN + 64:R_WIN + 128].astype(BF16)
            for r in range(sm // MOBA_BLOCK):
                b0 = r0 // MOBA_BLOCK + r
                kmean_ref[0, b0:b0 + 1, :] = jnp.sum(mk_tm[r * MOBA_BLOCK:(r + 1) * MOBA_BLOCK, :], axis=0,
                                                     keepdims=True) * (1.0 / MOBA_BLOCK)

    if seg is None:
        zprev_ref[...] = prev
        ztail_ref[0] = prev


def _proj_call(x2d, lw, tables, halos, *, tm, tiles_per_seq, n_seq, chunk, seg):
    rows = x2d.shape[0]
    nt = rows // tm
    tps = tiles_per_seq
    n_out_seq = n_seq if seg is None else 1
    t_out = rows // n_out_seq
    row = lambda w: pl.BlockSpec((tm, w), lambda i: (i, 0))
    col = lambda r: pl.BlockSpec((r, tm), lambda i: (0, i))
    seq3 = lambda r: pl.BlockSpec((1, r, tm), lambda i: (i // tps, 0, i % tps))
    full = lambda a: pl.BlockSpec(a.shape, lambda i: (0,) * a.ndim)
    n_tab = tables[0].shape[1] // tm
    tab = pl.BlockSpec((ROT_DIM // 2, tm), lambda i: (0, i % n_tab))
    ws = [lw['norm1'], lw['w_tm'], lw['w_tr'], lw['conv_w'], lw['conv_b'], lw['ln_g'], lw['ln_b'], lw['mix'],
          lw['mixb'], lw['gain_t']]
    ins = [x2d] + ws + list(tables)
    in_specs = [row(D_MODEL)] + [full(a) for a in ws] + [tab] * 2
    sds = jax.ShapeDtypeStruct
    out_shape = [sds((rows, 512), BF16), sds((512, rows), F32), sds((n_out_seq, 256, t_out), F32),
                 sds((n_out_seq, 128, t_out), F32), sds((n_out_seq, 512, t_out), F32),
                 sds((2 * SUBLANES, rows), F32), sds((rows, LANES), F32), sds((n_seq, SUBLANES, 256), F32)]
    out_specs = [row(512), col(512), seq3(256), seq3(128), seq3(512), col(2 * SUBLANES), row(LANES)]
    scratch = []
    if seg is None:
        out_specs.append(pl.BlockSpec((1, SUBLANES, 256), lambda i: (i // tps, 0, 0)))
        out_shape += [sds((rows, LANES), F32), sds((rows, LANES), BF16), sds((rows, LANES), BF16),
                      sds((rows, 256), BF16), sds((n_seq, 384, t_out), BF16), sds((nt, SUBLANES, 256), F32)]
        out_specs += [row(LANES), row(LANES), row(LANES), row(256), seq3(384),
                      pl.BlockSpec((1, SUBLANES, 256), lambda i: (i, 0, 0))]
        scratch.append(pltpu.VMEM((SUBLANES, 256), F32))
    else:
        ins += list(halos)
        in_specs += [row(256), row(256)]
        out_specs.append(pl.BlockSpec((n_seq, SUBLANES, 256), lambda i: (0, 0, 0)))
        out_shape.append(sds((rows, 256), F32))
        out_specs.append(row(256))
    return pl.pallas_call(
        functools.partial(_proj_kernel, tiles_per_seq=tps, chunk=chunk, seg=seg,
                          subs=tm // MOBA_BLOCK if seg is None else 1),
        grid=(nt,), in_specs=in_specs, out_specs=out_specs, out_shape=out_shape, scratch_shapes=scratch,
        compiler_params=pltpu.CompilerParams(dimension_semantics=("arbitrary",), vmem_limit_bytes=VMEM_LIMIT),
        name="proj_prompt" if seg is None else "proj_sample",
    )(*ins)


def _compress_compute(src_ref, wa_ref, wb_ref, pt_ref, pb_ref, w2_ref, out_ref, outt_ref, groups):
    acc_a = jnp.zeros((groups, 2 * NSA_CMP_HIDDEN), F32)
    acc_b = jnp.zeros((groups, 2 * NSA_CMP_HIDDEN), F32)
    for p in range(NSA_CMP_STRIDE // 2):
        xp = jnp.concatenate([src_ref[pl.ds(2 * p, groups, stride=NSA_CMP_STRIDE), :],
                              src_ref[pl.ds(2 * p + 1, groups, stride=NSA_CMP_STRIDE), :]], axis=1)
        acc_a = acc_a + _dot((xp + pt_ref[p:p + 1, :]).astype(BF16), wa_ref[p])
        acc_b = acc_b + _dot((xp + pb_ref[p:p + 1, :]).astype(BF16), wb_ref[p])
    hdn = jax.nn.gelu(acc_a + pltpu.roll(acc_b, groups - 1, 0))
    out = _dot(hdn.astype(BF16), w2_ref[...])
    if out_ref is not None:
        out_ref[0] = out
    outt_ref[0] = out.T


def _compress_prompt_kernel(src_ref, wa_ref, wb_ref, pt_ref, pb_ref, w2_ref, out_ref, outt_ref, *, groups):
    _compress_compute(src_ref.at[0], wa_ref, wb_ref, pt_ref, pb_ref, w2_ref, out_ref, outt_ref, groups)


def _compress_paged_kernel(ptab_ref, *refs, groups, pps):
    del ptab_ref
    pages = refs[:pps]
    wa_ref, wb_ref, pt_ref, pb_ref, w2_ref, outt_ref, stage_ref = refs[pps:]
    j = pl.program_id(1)
    for k in range(pps):
        pg = pages[k][0, 0].reshape(2 * HEAD_DIM, PAGE_SIZE)
        stage_ref[pl.ds(pl.multiple_of((j * pps + k) * PAGE_SIZE, PAGE_SIZE), PAGE_SIZE), :] = pg.T

    @pl.when(j == pl.num_programs(1) - 1)
    def _():
        _compress_compute(stage_ref, wa_ref, wb_ref, pt_ref, pb_ref, w2_ref, None, outt_ref, groups)


def _compress_prompt_call(cmp_tm3, lw):
    b, t, _ = cmp_tm3.shape
    groups = t // NSA_CMP_STRIDE
    full = lambda a: pl.BlockSpec(a.shape, lambda i: (0,) * a.ndim)
    ws = [lw['cmp_wa'], lw['cmp_wb'], lw['cmp_pt'], lw['cmp_pb'], lw['cmp_w2']]
    return pl.pallas_call(
        functools.partial(_compress_prompt_kernel, groups=groups),
        grid=(b,),
        in_specs=[pl.BlockSpec((1, t, LANES), lambda i: (i, 0, 0))] + [full(a) for a in ws],
        out_specs=[pl.BlockSpec((1, groups, LANES), lambda i: (i, 0, 0)),
                   pl.BlockSpec((1, LANES, groups), lambda i: (i, 0, 0))],
        out_shape=[jax.ShapeDtypeStruct((b, groups, LANES), F32), jax.ShapeDtypeStruct((b, LANES, groups), F32)],
        compiler_params=pltpu.CompilerParams(dimension_semantics=("arbitrary",), vmem_limit_bytes=VMEM_LIMIT),
        name="compress_prompt",
    )(cmp_tm3, *ws)


def _pages_per_step(n_pages, cap):
    pps = cap
    while n_pages % pps:
        pps //= 2
    return pps


def _compress_paged_call(cache5, layer, page_table, lw):
    b, n_pages = page_table.shape
    past = n_pages * PAGE_SIZE
    groups = past // NSA_CMP_STRIDE
    pps = _pages_per_step(n_pages, 64)
    full = lambda a: pl.BlockSpec(a.shape, lambda i, j, pt: (0,) * a.ndim)
    ws = [lw['cmp_wa'], lw['cmp_wb'], lw['cmp_pt'], lw['cmp_pb'], lw['cmp_w2']]
    page_specs = [pl.BlockSpec((1, 1, 2, HEAD_DIM, PAGE_SIZE), functools.partial(
        lambda i, j, pt, k: (layer, pt[i, j * pps + k], 0, 0, 0), k=k)) for k in range(pps)]
    grid_spec = pltpu.PrefetchScalarGridSpec(
        num_scalar_prefetch=1, grid=(b, n_pages // pps),
        in_specs=page_specs + [full(a) for a in ws],
        out_specs=pl.BlockSpec((1, LANES, groups), lambda i, j, pt: (i, 0, 0)),
        scratch_shapes=[pltpu.VMEM((past, LANES), F32)])
    return pl.pallas_call(
        functools.partial(_compress_paged_kernel, groups=groups, pps=pps),
        grid_spec=grid_spec, out_shape=jax.ShapeDtypeStruct((b, LANES, groups), F32),
        compiler_params=pltpu.CompilerParams(dimension_semantics=("arbitrary", "arbitrary"),
                                             vmem_limit_bytes=VMEM_LIMIT),
        name="compress_sample",
    )(page_table, *([cache5] * pps), *ws)


def _topk_mask(score, n_cols, k):
    idx = _iota(score.shape, 1)
    rank = jnp.zeros(score.shape, jnp.int32)
    for j in range(n_cols):
        col = score[:, j:j + 1]
        rank = rank + jnp.where(col > score, 1, jnp.where(col == score, jnp.where(idx > j, 1, 0), 0))
    return rank < k


def _topk_mask_t(score, n_rows, k):
    n_tiles = score.shape[0] // SUBLANES
    tiles = [score[r * SUBLANES:(r + 1) * SUBLANES, :] for r in range(n_tiles)]
    idx = _iota(tiles[0].shape, 0)
    ranks = [jnp.zeros(tiles[0].shape, jnp.int32) for _ in range(n_tiles)]
    for j in range(n_rows):
        row = score[j:j + 1, :]
        for r in range(n_tiles):
            if r * SUBLANES > j:
                ranks[r] = jnp.where(row >= tiles[r], ranks[r] + 1, ranks[r])
            elif (r + 1) * SUBLANES - 1 <= j:
                ranks[r] = jnp.where(row > tiles[r], ranks[r] + 1, ranks[r])
            else:
                ranks[r] = ranks[r] + jnp.where(idx + r * SUBLANES > j, jnp.where(row >= tiles[r], 1, 0),
                                                jnp.where(row > tiles[r], 1, 0))
    return jnp.concatenate(ranks, axis=0) < k


def _softmax_axis(s, ok, axis):
    s = jnp.where(ok, s, NEG)
    m = jnp.max(s, axis=axis, keepdims=True)
    p = jnp.where(ok, jnp.exp(s - m), 0.0)
    l = jnp.sum(p, axis=axis, keepdims=True)
    return p / jnp.where(l > 0.0, l, 1.0)


def _forced_importance(imp, blk, cur):
    forced = (blk == 0) | (blk == cur) | (blk == cur - 1)
    imp = jnp.where(forced, BIG, imp)
    return jnp.where(blk > cur, NEG, imp)


def _nsa_prompt_kernel(qt_ref, gt_ref, kcvc_ref, kcvct_ref, slc_ref, win_ref, vt_ref, o_ref, sel_ref, s_ref, *, t, qb, tk):
    iq = pl.program_id(1)
    s0 = iq * qb
    nq = NSA_HEADS * qb
    qt = qt_ref[...] * (HEAD_DIM ** -0.5)
    qst = jnp.concatenate([qt[h * HEAD_DIM:(h + 1) * HEAD_DIM, :] for h in range(NSA_HEADS)], axis=1)
    qst = jnp.concatenate([qst, jnp.zeros_like(qst)], axis=0).astype(BF16)
    qpos_q = s0 + _iota((1, qb), 1)
    qpos = _lane_tile(qpos_q, NSA_HEADS)

    wlen = NSA_WINDOW + qb
    start = pl.multiple_of(jnp.maximum(s0 - NSA_WINDOW, 0), qb)
    sw = _dot(win_ref[0, pl.ds(start, wlen), :], qst)
    back = (qpos - start) - _iota(sw.shape, 0)
    sw = jnp.where((back & -NSA_WINDOW) == 0, sw, NEG)
    ew = jnp.exp(sw - jnp.max(sw, axis=0, keepdims=True))
    o_win = (_dot(vt_ref[0, HEAD_DIM:2 * HEAD_DIM, pl.ds(start, wlen)], ew.astype(BF16))
             / jnp.sum(ew, axis=0, keepdims=True))

    n_cmp = (t - NSA_CMP_LEN) // NSA_CMP_STRIDE + 1
    n_blk = t // NSA_SLC_BLOCK
    kcvc = kcvc_ref[0].astype(BF16)
    g = kcvc.shape[0]
    s = _dot(kcvc, qst)
    n_idx = _iota(s.shape, 0)
    ok = (n_idx * NSA_CMP_STRIDE + (NSA_CMP_LEN - 1) <= qpos) & (n_idx < n_cmp)
    s = jnp.where(ok, s, NEG)
    m = jnp.max(s, axis=0, keepdims=True)
    e = jnp.exp(s - m)
    p = e * jnp.where(m > 0.5 * NEG, 1.0 / jnp.sum(e, axis=0, keepdims=True), 0.0)
    o_cmp = _dot(kcvct_ref[0, HEAD_DIM:2 * HEAD_DIM, :].astype(BF16), p.astype(BF16))
    psum = p[:, 0:qb]
    for h in range(1, NSA_HEADS):
        psum = psum + p[:, h * qb:(h + 1) * qb]
    cj = _iota((n_blk, g), 0)
    cn = _iota((n_blk, g), 1)
    cover = ((cn * NSA_CMP_STRIDE <= cj * NSA_SLC_BLOCK + (NSA_SLC_BLOCK - 1))
             & (cn * NSA_CMP_STRIDE + (NSA_CMP_LEN - 1) >= cj * NSA_SLC_BLOCK) & (cn < n_cmp)).astype(BF16)
    p_hi, p_lo = _split(psum)
    imp = _dot(cover, p_hi) + _dot(cover, p_lo)
    imp = _forced_importance(imp, _iota(imp.shape, 0), qpos_q // NSA_SLC_BLOCK)
    cur = qpos_q // NSA_SLC_BLOCK
    picked = _topk_mask_t(imp, n_blk, NSA_TOPN) & (_iota(imp.shape, 0) <= cur)
    sel_ref[...] = _lane_tile(jnp.where(picked, 0.0, NEG), NSA_HEADS)

    bpt = tk // NSA_SLC_BLOCK
    q64 = qst[0:HEAD_DIM].astype(F32)
    zpad = jnp.zeros((LANES - HEAD_DIM - bpt, nq), F32)
    kd = s0 // tk

    def scores(kt):
        k0 = pl.multiple_of(kt * tk, tk)
        slab = sel_ref[pl.ds(pl.multiple_of(kt * bpt, bpt), bpt), :]
        w = jnp.concatenate([q64, slab, zpad], axis=0).astype(BF16)
        return _dot(slc_ref[0, pl.ds(k0, tk), :], w)

    def fold(x, op):
        return op(x.reshape(tk // SUBLANES, SUBLANES, nq), axis=0)

    def sweep1(kt, m8):
        sc = scores(kt)
        s_ref[kt] = sc
        return jnp.maximum(m8, fold(sc, jnp.max))

    m8 = lax.fori_loop(0, kd, sweep1, jnp.full((SUBLANES, nq), NEG, F32))
    sc = scores(kd)
    sc = jnp.where(kd * tk + _iota(sc.shape, 0) <= qpos, sc, NEG)
    s_ref[kd] = sc
    m = jnp.max(jnp.maximum(m8, fold(sc, jnp.max)), axis=0, keepdims=True)

    def sweep2(kt, carry):
        l8, acc = carry
        k0 = pl.multiple_of(kt * tk, tk)
        pp = jnp.exp(s_ref[kt] - m)
        return l8 + fold(pp, jnp.sum), acc + _dot(vt_ref[0, 0:HEAD_DIM, pl.ds(k0, tk)], pp.astype(BF16))

    l8, acc = lax.fori_loop(0, kd + 1, sweep2, (jnp.zeros((SUBLANES, nq), F32), jnp.zeros((HEAD_DIM, nq), F32)))
    o_slc = acc / jnp.sum(l8, axis=0, keepdims=True)

    gt = gt_ref[...]
    outs = []
    for h in range(NSA_HEADS):
        c = slice(h * qb, (h + 1) * qb)
        outs.append(gt[3 * h:3 * h + 1, :] * o_cmp[:, c] + gt[3 * h + 1:3 * h + 2, :] * o_slc[:, c]
                    + gt[3 * h + 2:3 * h + 3, :] * o_win[:, c])
    o_ref[...] = jnp.concatenate(outs, axis=0)


def _nsa_prompt_call(qt, gt, kcvc, kcvct, slc_tm3, win_tm3, vt3, *, qb=256, tk=SLC_TILE):
    b, t, _ = slc_tm3.shape
    nq = t // qb
    g = kcvc.shape[1]
    assert t % tk == 0 and tk % qb == 0 and t >= NSA_WINDOW + qb
    return pl.pallas_call(
        functools.partial(_nsa_prompt_kernel, t=t, qb=qb, tk=tk),
        grid=(b, nq),
        in_specs=[pl.BlockSpec((256, qb), lambda i, j: (0, i * nq + j)),
                  pl.BlockSpec((2 * SUBLANES, qb), lambda i, j: (0, i * nq + j)),
                  pl.BlockSpec((1, g, LANES), lambda i, j: (i, 0, 0)),
                  pl.BlockSpec((1, LANES, g), lambda i, j: (i, 0, 0)),
                  pl.BlockSpec((1, t, LANES), lambda i, j: (i, 0, 0)),
                  pl.BlockSpec((1, t, LANES), lambda i, j: (i, 0, 0)),
                  pl.BlockSpec((1, LANES, t), lambda i, j: (i, 2, 0))],
        out_specs=pl.BlockSpec((256, qb), lambda i, j: (0, i * nq + j)),
        out_shape=jax.ShapeDtypeStruct((256, b * t), F32),
        scratch_shapes=[pltpu.VMEM((t // NSA_SLC_BLOCK, NSA_HEADS * qb), F32),
                        pltpu.VMEM((t // tk, tk, NSA_HEADS * qb), F32)],
        compiler_params=pltpu.CompilerParams(dimension_semantics=("arbitrary", "arbitrary"),
                                             vmem_limit_bytes=VMEM_LIMIT),
        name="nsa_prompt",
    )(qt, gt, kcvc, kcvct, slc_tm3, win_tm3, vt3)


def _nsa_sample_kernel(ptab_ref, *refs, past, tq, pps):
    del ptab_ref
    pages = refs[:pps]
    q_ref, g_ref, kcvct_ref, new_ref, wst_ref, wnew_ref, o_ref, kt_ref, vt_ref, e_ref = refs[pps:]
    j = pl.program_id(1)

    @pl.when((pl.program_id(0) == 0) & (j == 0))
    def _():
        e_ref[...] = (_iota(e_ref.shape, 0) == _iota(e_ref.shape, 1) // NSA_SLC_BLOCK).astype(BF16)

    for k in range(pps):
        pg = pages[k][0, 0]
        c0 = pl.multiple_of((j * pps + k) * PAGE_SIZE, PAGE_SIZE)
        kt_ref[:, pl.ds(c0, PAGE_SIZE)] = pg[0].astype(BF16)
        vt_ref[:, pl.ds(c0, PAGE_SIZE)] = pg[1].astype(BF16)

    @pl.when(j == pl.num_programs(1) - 1)
    def _():
        rows = NSA_HEADS * tq
        qs = (q_ref[0] * (HEAD_DIM ** -0.5)).astype(BF16)
        qpos_q = past + _iota((tq, 1), 0)
        qpos = jnp.concatenate([qpos_q] * NSA_HEADS, axis=0)
        l_all = past + tq
        n_cmp = (l_all - NSA_CMP_LEN) // NSA_CMP_STRIDE + 1
        n_blk = -(-l_all // NSA_SLC_BLOCK)
        nb_pad = -(-n_blk // LANES) * LANES

        kcvct = kcvct_ref[0].astype(BF16)
        g = kcvct.shape[1]
        s = _dot(qs, kcvct[0:HEAD_DIM])
        n_idx = _iota(s.shape, 1)
        ok = (n_idx * NSA_CMP_STRIDE + (NSA_CMP_LEN - 1) <= qpos) & (n_idx < n_cmp)
        p = _softmax_axis(s, ok, 1)
        o_cmp = _dot_nt(p.astype(BF16), kcvct[HEAD_DIM:2 * HEAD_DIM])
        psum = p[0:tq]
        for h in range(1, NSA_HEADS):
            psum = psum + p[h * tq:(h + 1) * tq]
        cn = _iota((g, nb_pad), 0)
        cj = _iota((g, nb_pad), 1)
        cover = ((cn * NSA_CMP_STRIDE <= cj * NSA_SLC_BLOCK + (NSA_SLC_BLOCK - 1))
                 & (cn * NSA_CMP_STRIDE + (NSA_CMP_LEN - 1) >= cj * NSA_SLC_BLOCK) & (cn < n_cmp)).astype(BF16)
        p_hi, p_lo = _split(psum)
        imp = _dot(p_hi, cover) + _dot(p_lo, cover)
        imp = _forced_importance(imp, _iota(imp.shape, 1), qpos_q // NSA_SLC_BLOCK)
        picked = _topk_mask(imp, n_blk, NSA_TOPN)

        nbe = e_ref.shape[0]
        bias = jnp.where(picked[:, 0:nbe], 0.0, NEG).astype(BF16)
        bias = jnp.concatenate([bias] * NSA_HEADS, axis=0)
        sc = _dot(qs, kt_ref[...]) + _dot(bias, e_ref[...])
        new_ok = past + _iota((rows, PAGE_SIZE), 1) <= qpos
        sc_new = jnp.where(new_ok, _dot(qs, new_ref[0, 0].astype(BF16)), NEG)
        m = jnp.maximum(jnp.max(sc, axis=-1, keepdims=True), jnp.max(sc_new, axis=-1, keepdims=True))
        pp = jnp.exp(sc - m)
        pp_new = jnp.exp(sc_new - m)
        l = jnp.sum(pp, axis=-1, keepdims=True) + jnp.sum(pp_new, axis=-1, keepdims=True)
        o_slc = (_dot_nt(pp.astype(BF16), vt_ref[...]) + _dot_nt(pp_new.astype(BF16), new_ref[0, 1].astype(BF16))) / l

        kw = jnp.concatenate([wst_ref[0, 0, 0], wnew_ref[0, 0]], axis=1).astype(BF16)
        vw = jnp.concatenate([wst_ref[0, 0, 1], wnew_ref[0, 1]], axis=1).astype(BF16)
        sw = _dot(qs, kw)
        kpos = past - NSA_WINDOW + _iota(sw.shape, 1)
        okw = (kpos <= qpos) & (kpos > qpos - NSA_WINDOW) & (kpos >= 0)
        o_win = _dot_nt(_softmax_axis(sw, okw, 1).astype(BF16), vw)

        gg = g_ref[0]
        o_ref[0] = gg[:, 0:1] * o_cmp + gg[:, 1:2] * o_slc + gg[:, 2:3] * o_win


def _nsa_sample_call(cache5, layer, page_table, q32, g32, kcvct, newt, wstate5, wnewt):
    b, n_pages = page_table.shape
    past = n_pages * PAGE_SIZE
    tq = q32.shape[1] // NSA_HEADS
    pps = _pages_per_step(n_pages, 64)
    nbe = -(-(past // NSA_SLC_BLOCK) // LANES) * LANES
    assert tq <= NSA_SLC_BLOCK and past >= NSA_WINDOW
    assert (past + tq - NSA_CMP_LEN) // NSA_CMP_STRIDE + 1 <= past // NSA_CMP_STRIDE - 1
    rows = NSA_HEADS * tq
    page_specs = [pl.BlockSpec((1, 1, 2, HEAD_DIM, PAGE_SIZE), functools.partial(
        lambda i, j, pt, k: (layer, pt[i, j * pps + k], 1, 0, 0), k=k)) for k in range(pps)]
    per_b = lambda a: pl.BlockSpec((1,) + a.shape[1:], lambda i, j, pt: (i,) + (0,) * (a.ndim - 1))
    grid_spec = pltpu.PrefetchScalarGridSpec(
        num_scalar_prefetch=1, grid=(b, n_pages // pps),
        in_specs=page_specs + [per_b(q32), per_b(g32), per_b(kcvct), per_b(newt),
                               pl.BlockSpec((1, 1, 2, HEAD_DIM, NSA_WINDOW), lambda i, j, pt: (layer, i, 0, 0, 0)),
                               per_b(wnewt)],
        out_specs=pl.BlockSpec((1, rows, HEAD_DIM), lambda i, j, pt: (i, 0, 0)),
        scratch_shapes=[pltpu.VMEM((HEAD_DIM, past), BF16), pltpu.VMEM((HEAD_DIM, past), BF16),
                        pltpu.VMEM((nbe, past), BF16)])
    return pl.pallas_call(
        functools.partial(_nsa_sample_kernel, past=past, tq=tq, pps=pps),
        grid_spec=grid_spec, out_shape=jax.ShapeDtypeStruct((b, rows, HEAD_DIM), F32),
        compiler_params=pltpu.CompilerParams(dimension_semantics=("arbitrary", "arbitrary"),
                                             vmem_limit_bytes=VMEM_LIMIT),
        name="nsa_sample",
    )(page_table, *([cache5] * pps), q32, g32, kcvct, newt, wstate5, wnewt)


def _moba_prompt_kernel(qt_ref, kmean_ref, k_ref, vt_ref, o_ref, sel_ref, acc_ref, s_ref, *, nb):
    i = pl.program_id(1)
    r0 = pl.multiple_of(i * MOBA_BLOCK, MOBA_BLOCK)
    qb = MOBA_BLOCK
    nq = MOBA_HEADS * qb
    qt = qt_ref[...]
    rowgrp = _iota(qt.shape, 0) // HEAD_DIM
    kmean = kmean_ref[0]
    qpad = []
    for h in range(MOBA_HEADS):
        qh = jnp.where(rowgrp == h, qt, 0.0)
        gs = _dot_3pass(kmean, qh)
        n_idx = _iota(gs.shape, 0)
        gs = jnp.where(n_idx < i, gs, NEG)
        sel_ref[:, h * qb:(h + 1) * qb] = jnp.where(_topk_mask_t(gs, nb, MOBA_TOPK) & (n_idx < i), 0.0, NEG)
        qpad.append((qh * (HEAD_DIM ** -0.5)).astype(BF16))
    qcat = jnp.concatenate(qpad, axis=1)

    def block(n, width=1):
        k0 = pl.multiple_of(n * MOBA_BLOCK, width * MOBA_BLOCK)
        return k_ref[0, pl.ds(k0, width * MOBA_BLOCK), :], vt_ref[0, :, pl.ds(k0, width * MOBA_BLOCK)]

    def fold(x, op):
        return op(x.reshape(x.shape[0] // SUBLANES, SUBLANES, nq), axis=0)

    n_pairs = (i + 1) // 2

    def sweep1(n2, m8):
        bias = jnp.stack([sel_ref[pl.ds(2 * n2, 1), :], sel_ref[pl.ds(2 * n2 + 1, 1), :]])
        sc = (_dot(block(2 * n2, 2)[0], qcat).reshape(2, MOBA_BLOCK, nq) + bias).reshape(2 * MOBA_BLOCK, nq)
        s_ref[n2] = sc
        return jnp.maximum(m8, fold(sc, jnp.max))

    m8 = lax.fori_loop(0, n_pairs, sweep1, jnp.full((SUBLANES, nq), NEG, F32))
    own_k, own_v = block(i)
    causal = _iota((MOBA_BLOCK, nq), 0) <= _iota((MOBA_BLOCK, nq), 1) % qb
    own_s = jnp.where(causal, _dot(own_k, qcat), NEG)
    m = jnp.max(jnp.maximum(m8, fold(own_s, jnp.max)), axis=0, keepdims=True)

    acc_ref[...] = jnp.zeros_like(acc_ref)

    def accumulate(p, vb):
        pb = p.astype(BF16)
        for h in range(MOBA_HEADS):
            acc_ref[h] += _dot(vb[h * HEAD_DIM:(h + 1) * HEAD_DIM, :], pb[:, h * qb:(h + 1) * qb])
        return fold(p, jnp.sum)

    def sweep2(n2, l8):
        return l8 + accumulate(jnp.exp(s_ref[n2] - m), block(2 * n2, 2)[1])

    l8 = lax.fori_loop(0, n_pairs, sweep2, jnp.zeros((SUBLANES, nq), F32))
    l = jnp.sum(l8 + accumulate(jnp.exp(own_s - m), own_v), axis=0, keepdims=True)
    o_ref[...] = jnp.concatenate([acc_ref[h] / l[:, h * qb:(h + 1) * qb] for h in range(MOBA_HEADS)], axis=0)


def _moba_prompt_call(qt, kmean3, mk_tm3, vt3):
    b, t, _ = mk_tm3.shape
    nb = t // MOBA_BLOCK
    nbp = kmean3.shape[1]
    return pl.pallas_call(
        functools.partial(_moba_prompt_kernel, nb=nb),
        grid=(b, nb),
        in_specs=[pl.BlockSpec((256, MOBA_BLOCK), lambda i, j: (1, i * nb + j)),
                  pl.BlockSpec((1, nbp, 256), lambda i, j: (i, 0, 0)),
                  pl.BlockSpec((1, t, 256), lambda i, j: (i, 0, 0)),
                  pl.BlockSpec((1, 256, t), lambda i, j: (i, 0, 0))],
        out_specs=pl.BlockSpec((256, MOBA_BLOCK), lambda i, j: (0, i * nb + j)),
        out_shape=jax.ShapeDtypeStruct((256, b * t), F32),
        scratch_shapes=[pltpu.VMEM((nbp, MOBA_HEADS * MOBA_BLOCK), F32),
                        pltpu.VMEM((MOBA_HEADS, HEAD_DIM, MOBA_BLOCK), F32),
                        pltpu.VMEM((nb // 2, 2 * MOBA_BLOCK, MOBA_HEADS * MOBA_BLOCK), F32)],
        compiler_params=pltpu.CompilerParams(dimension_semantics=("arbitrary", "arbitrary"),
                                             vmem_limit_bytes=VMEM_LIMIT),
        name="moba_prompt",
    )(qt, kmean3, mk_tm3, vt3)


def _moba_sample_kernel(ptab_ref, *refs, past, tq, n_pages):
    del ptab_ref
    pages = refs[:n_pages]
    q_ref, new_ref, o_ref, s_ref, e_ref = refs[n_pages:]
    ppb = MOBA_BLOCK // PAGE_SIZE
    n_past = past // MOBA_BLOCK

    @pl.when(pl.program_id(0) == 0)
    def _():
        e_ref[...] = (_iota(e_ref.shape, 0) == _iota(e_ref.shape, 1) // MOBA_BLOCK).astype(BF16)

    q = q_ref[0]
    rows = q.shape[0]
    qb = (q * (HEAD_DIM ** -0.5)).astype(BF16)

    def block(n, which):
        return jnp.concatenate([pages[n * ppb + c][0, 0, which] for c in range(ppb)], axis=1)

    lane = _iota((256, LANES), 1)
    kmean = jnp.zeros((256, LANES), F32)
    for n in range(n_past):
        kb = block(n, 0)
        kmean = jnp.where(lane == n, jnp.sum(kb, axis=1, keepdims=True) * (1.0 / MOBA_BLOCK), kmean)
        s_ref[:, n * MOBA_BLOCK:(n + 1) * MOBA_BLOCK] = _dot(qb, kb.astype(BF16))
    gs = _dot_3pass(q, kmean)
    n_idx = _iota(gs.shape, 1)
    gs = jnp.where(n_idx < n_past, gs, NEG)
    bias = jnp.where(_topk_mask(gs, n_past, MOBA_TOPK) & (n_idx < n_past), 0.0, NEG).astype(BF16)
    s = s_ref[...] + _dot(bias, e_ref[...])
    own_ok = _iota((rows, PAGE_SIZE), 1) <= _iota((rows, PAGE_SIZE), 0) % tq
    s_own = jnp.where(own_ok, _dot(qb, new_ref[0, 0].astype(BF16)), NEG)
    m = jnp.maximum(jnp.max(s, axis=-1, keepdims=True), jnp.max(s_own, axis=-1, keepdims=True))
    p = jnp.exp(s - m)
    p_own = jnp.exp(s_own - m)
    l = jnp.sum(p, axis=-1, keepdims=True) + jnp.sum(p_own, axis=-1, keepdims=True)
    pb = p.astype(BF16)
    acc = _dot_nt(p_own.astype(BF16), new_ref[0, 1].astype(BF16))
    for n in range(n_past):
        acc = acc + _dot_nt(pb[:, n * MOBA_BLOCK:(n + 1) * MOBA_BLOCK], block(n, 1).astype(BF16))
    oh = acc / l
    grp = _iota((tq, 256), 1) // HEAD_DIM
    out = oh[0:tq]
    for h in range(1, MOBA_HEADS):
        out = jnp.where(grp == h, oh[h * tq:(h + 1) * tq], out)
    o_ref[0] = out


def _moba_sample_call(cache5, layer, page_table, qbd, newt):
    b, n_pages = page_table.shape
    past = n_pages * PAGE_SIZE
    rows = qbd.shape[1]
    tq = rows // MOBA_HEADS
    assert past % MOBA_BLOCK == 0 and tq <= PAGE_SIZE and past // MOBA_BLOCK <= LANES
    page_specs = [pl.BlockSpec((1, 1, 2, 256, PAGE_SIZE), functools.partial(
        lambda i, pt, k: (layer, pt[i, k], 0, 0, 0), k=k)) for k in range(n_pages)]
    per_b = lambda a: pl.BlockSpec((1,) + a.shape[1:], lambda i, pt: (i,) + (0,) * (a.ndim - 1))
    grid_spec = pltpu.PrefetchScalarGridSpec(
        num_scalar_prefetch=1, grid=(b,),
        in_specs=page_specs + [per_b(qbd), per_b(newt)],
        out_specs=pl.BlockSpec((1, tq, 256), lambda i, pt: (i, 0, 0)),
        scratch_shapes=[pltpu.VMEM((rows, past), F32), pltpu.VMEM((LANES, past), BF16)])
    return pl.pallas_call(
        functools.partial(_moba_sample_kernel, past=past, tq=tq, n_pages=n_pages),
        grid_spec=grid_spec, out_shape=jax.ShapeDtypeStruct((b, tq, 256), F32),
        compiler_params=pltpu.CompilerParams(dimension_semantics=("arbitrary",), vmem_limit_bytes=VMEM_LIMIT),
        name="moba_sample",
    )(page_table, *([cache5] * n_pages), qbd, newt)


def _merge_kernel(x_ref, oab_ref, oct_ref, odt_ref, n1_ref, wgt_ref, wbr_ref, wo_ref, y_ref, *, subs):
    sm = x_ref.shape[0] // subs
    for si in range(subs):
        rs = slice(si * sm, (si + 1) * sm)
        x = x_ref[rs, :]
        xn = (x * lax.rsqrt(jnp.mean(x * x, axis=-1, keepdims=True) + EPS) * n1_ref[...]).astype(BF16)
        branches = [oab_ref[rs, 0:256], oab_ref[rs, 256:512], oct_ref[:, rs].T.astype(BF16),
                    odt_ref[:, rs].T.astype(BF16)]
        merged = jnp.zeros(x.shape, F32)
        for n in range(N_BRANCH):
            g = jax.nn.sigmoid(_dot_nt(xn, wgt_ref[n * D_MODEL:(n + 1) * D_MODEL, :]))
            merged = merged + g * _dot(branches[n], wbr_ref[n])
        y_ref[rs, :] = x + _dot(merged.astype(BF16), wo_ref[...])


def _merge_call(x2d, oab, oct, odt, lw, *, tm):
    rows = x2d.shape[0]
    row = lambda w: pl.BlockSpec((tm, w), lambda i: (i, 0))
    col = lambda r: pl.BlockSpec((r, tm), lambda i: (0, i))
    full = lambda a: pl.BlockSpec(a.shape, lambda i: (0,) * a.ndim)
    ws = [lw['norm1'], lw['wg_t'], lw['w_branch'], lw['w_o']]
    return pl.pallas_call(
        functools.partial(_merge_kernel, subs=2 if tm % (2 * LANES) == 0 else 1), grid=(rows // tm,),
        in_specs=[row(D_MODEL), row(512), col(256), col(256)] + [full(a) for a in ws],
        out_specs=row(D_MODEL), out_shape=jax.ShapeDtypeStruct((rows, D_MODEL), F32),
        compiler_params=pltpu.CompilerParams(dimension_semantics=("arbitrary",), vmem_limit_bytes=VMEM_LIMIT),
        name="merge",
    )(x2d, oab, oct, odt, *ws)


def _ffn_kernel(*refs, tiles_per_seq, seg, subs):
    it = iter(refs)
    x_ref, n2_ref, wgate_ref, wup_ref, cw_ref, cb_ref, wdown_ref = (next(it) for _ in range(7))
    if seg is not None:
        h1_ref, h2_ref = next(it), next(it)
    y_ref, tail_ref = next(it), next(it)
    if seg is None:
        prev_ref = next(it)

    tm = x_ref.shape[0]
    sm = tm // subs
    if seg is None:
        @pl.when(pl.program_id(0) % tiles_per_seq == 0)
        def _():
            prev_ref[...] = jnp.zeros_like(prev_ref)

        prev = prev_ref[...]

    for si in range(subs):
        rs = slice(si * sm, (si + 1) * sm)
        x = x_ref[rs, :]
        h2 = (x * lax.rsqrt(jnp.mean(x * x, axis=-1, keepdims=True) + EPS) * n2_ref[...]).astype(BF16)
        gate = _dot(h2, wgate_ref[...])
        if seg is None:
            g1, g2 = _shift_rows(gate, prev[SUBLANES - 1:SUBLANES, :], prev[SUBLANES - 2:SUBLANES - 1, :], None)
            prev = gate[sm - SUBLANES:, :]
        else:
            g1, g2 = _shift_rows(gate, h1_ref[...], h2_ref[...], seg)
            tail_ref[...] = gate.reshape(tail_ref.shape)
        ac = g2 * cw_ref[0:1, :] + cb_ref[...] + g1 * cw_ref[1:2, :] + gate * cw_ref[2:3, :]
        act = jax.nn.silu(ac) * _dot(h2, wup_ref[...])
        y_ref[rs, :] = x + _dot(act.astype(BF16), wdown_ref[...])

    if seg is None:
        prev_ref[...] = prev
        tail_ref[0] = prev


def _ffn_call(x2d, lw, halos, *, tm, tiles_per_seq, n_seq, seg):
    rows = x2d.shape[0]
    row = lambda w: pl.BlockSpec((tm, w), lambda i: (i, 0))
    full = lambda a: pl.BlockSpec(a.shape, lambda i: (0,) * a.ndim)
    ws = [lw['norm2'], lw['w_gate'], lw['w_up'], lw['ffn_conv_w'], lw['ffn_conv_b'], lw['w_down']]
    ins = [x2d] + ws
    in_specs = [row(D_MODEL)] + [full(a) for a in ws]
    scratch = []
    if seg is None:
        tail_spec = pl.BlockSpec((1, SUBLANES, D_FF), lambda i: (i // tiles_per_seq, 0, 0))
        scratch.append(pltpu.VMEM((SUBLANES, D_FF), F32))
    else:
        ins += list(halos)
        in_specs += [row(D_FF), row(D_FF)]
        tail_spec = pl.BlockSpec((n_seq, SUBLANES, D_FF), lambda i: (0, 0, 0))
    return pl.pallas_call(
        functools.partial(_ffn_kernel, tiles_per_seq=tiles_per_seq, seg=seg, subs=2 if seg is None else 1),
        grid=(rows // tm,), in_specs=in_specs, out_specs=[row(D_MODEL), tail_spec],
        out_shape=[jax.ShapeDtypeStruct((rows, D_MODEL), F32), jax.ShapeDtypeStruct((n_seq, SUBLANES, D_FF), F32)],
        scratch_shapes=scratch,
        compiler_params=pltpu.CompilerParams(dimension_semantics=("arbitrary",), vmem_limit_bytes=VMEM_LIMIT),
        name="ffn_prompt" if seg is None else "ffn_sample",
    )(*ins)


def _rope_tables_t(pos):
    half = ROT_DIM // 2
    inv = jnp.power(jnp.float32(ROPE_THETA), -jnp.arange(half, dtype=F32) / half)
    ang = inv[:, None] * pos.astype(F32)[None, :]
    return jnp.cos(ang), jnp.sin(ang)


def _layer_weights(l, p, sample_tq, dec_batch):
    wt = jnp.transpose(p['w_in'], (2, 0, 1))[:, l, :]
    o_ck, o_cv, o_g = 1536, 1728, 1920
    o_dq = o_g + 3 * NSA_HEADS
    o_br = o_dq + 3 * 256
    hd = HEAD_DIM
    w_tm = jnp.concatenate([wt[0:1280], wt[o_g:o_g + 12], jnp.zeros((LANES - 12, D_MODEL), F32)], axis=0)
    w_tr = jnp.concatenate([
        wt[1280:1536], wt[o_dq:o_dq + 256],
        wt[o_ck:o_ck + hd], wt[o_cv:o_cv + hd], wt[o_ck + hd:o_ck + 2 * hd], wt[o_cv + hd:o_cv + 2 * hd],
        wt[o_ck + 2 * hd:o_ck + 3 * hd], wt[o_cv + 2 * hd:o_cv + 3 * hd],
        wt[o_dq + 256:o_dq + 768], wt[o_g:o_g + 12], jnp.zeros((2 * SUBLANES - 12, D_MODEL), F32)], axis=0)
    lw = {'w_tm': w_tm.astype(BF16), 'w_tr': w_tr.astype(BF16), 'wg_t': wt[o_br:].astype(BF16)}
    row = lambda a: a.reshape(1, -1)
    lw['norm1'] = row(p['norm1'][l])
    lw['norm2'] = row(p['norm2'][l])
    lw['conv_w'] = p['conv_w'][l]
    lw['conv_b'] = row(p['conv_b'][l])
    lw['ln_g'] = row(p['gmlp_ln_g'][l])
    lw['ln_b'] = row(p['gmlp_ln_b'][l])
    tril = jnp.tril(jnp.ones((GMLP_CHUNK, GMLP_CHUNK), bool))
    ws = jnp.where(tril[None], p['gmlp_ws'][l], 0.0)
    bs = p['gmlp_bs'][l]
    lw['mix_p'] = ws.reshape(GMLP_GROUPS * GMLP_CHUNK, GMLP_CHUNK).astype(BF16)
    lw['mixb_p'] = jnp.repeat(bs.T, HEAD_DIM, axis=1)
    eye = jnp.eye(dec_batch, dtype=F32)
    lw['mix_s'] = jnp.concatenate([jnp.kron(eye, ws[g, :sample_tq, :sample_tq]) for g in range(GMLP_GROUPS)],
                                  axis=0).astype(BF16)
    lw['mixb_s'] = jnp.tile(jnp.repeat(bs.T[:sample_tq], HEAD_DIM, axis=1), (dec_batch, 1))
    kg = p['nsa_k_gain'][l]
    one = jnp.ones((hd,), F32)
    gains = jnp.concatenate([jnp.tile(p['nsa_q_gain'][l], NSA_HEADS), jnp.tile(p['moba_q_gain'][l], MOBA_HEADS),
                             kg[0], one, kg[1], one, kg[2], one, jnp.tile(p['moba_k_gain'][l], MOBA_HEADS)])
    lw['gain_t'] = jnp.broadcast_to(gains[:, None], (R_MV, LANES))
    w1c = p['nsa_cmp_w1'][l].reshape(2, NSA_CMP_LEN, HEAD_DIM, NSA_CMP_HIDDEN)
    pos = p['nsa_cmp_pos'][l]
    zero = jnp.zeros((HEAD_DIM, NSA_CMP_HIDDEN), F32)

    def pair_w(r_off):
        mats = []
        for pp in range(NSA_CMP_STRIDE // 2):
            blocks = []
            for r in (2 * pp, 2 * pp + 1):
                blocks.append(jnp.concatenate([w1c[0, r + r_off], zero], axis=1))
                blocks.append(jnp.concatenate([zero, w1c[1, r + r_off]], axis=1))
            mats.append(jnp.concatenate(blocks, axis=0))
        return jnp.stack(mats).astype(BF16)

    def pair_pos(r_off):
        return jnp.stack([jnp.concatenate([pos[0, 2 * pp + r_off], pos[1, 2 * pp + r_off],
                                           pos[0, 2 * pp + 1 + r_off], pos[1, 2 * pp + 1 + r_off]])
                          for pp in range(NSA_CMP_STRIDE // 2)])

    lw['cmp_wa'], lw['cmp_wb'] = pair_w(0), pair_w(NSA_CMP_STRIDE)
    lw['cmp_pt'], lw['cmp_pb'] = pair_pos(0), pair_pos(NSA_CMP_STRIDE)
    w2 = p['nsa_cmp_w2'][l]
    z2 = jnp.zeros((NSA_CMP_HIDDEN, HEAD_DIM), F32)
    lw['cmp_w2'] = jnp.concatenate([jnp.concatenate([w2[0], z2], axis=1),
                                    jnp.concatenate([z2, w2[1]], axis=1)], axis=0).astype(BF16)
    lw['w_branch'] = p['w_branch'][l].astype(BF16)
    lw['w_o'] = p['w_o'][l].astype(BF16)
    lw['w_gate'] = p['w_ffn_gate'][l].astype(BF16)
    lw['w_up'] = p['w_ffn_up'][l].astype(BF16)
    lw['w_down'] = p['w_ffn_down'][l].astype(BF16)
    lw['ffn_conv_w'] = p['ffn_conv_w'][l]
    lw['ffn_conv_b'] = row(p['ffn_conv_b'][l])
    return lw


def _halos(buf, tq):
    b, _, c = buf.shape
    z = jnp.zeros((b, tq - 1, c), F32)
    h1 = jnp.concatenate([buf[:, 1:2], z], axis=1)
    h2 = jnp.concatenate([buf[:, 0:2], z[:, 1:]], axis=1)
    return h1.reshape(b * tq, c), h2.reshape(b * tq, c)


def _row_tile(rows, cap):
    tm = cap
    while rows % tm:
        tm //= 2
    return tm


def _per_seq_pages(xt, b, tq):
    c = xt.shape[0]
    x = xt.reshape(c, b, tq).transpose(1, 0, 2)
    return jnp.concatenate([x, jnp.zeros((b, c, PAGE_SIZE - tq), F32)], axis=-1)


def _prompt_layer(x2d, lw, tables, b, t):
    tm = _row_tile(t, 512)
    assert tm % MOBA_BLOCK == 0 and tm % SLC_TILE == 0
    tps = t // tm
    pw = dict(lw, mix=lw['mix_p'], mixb=lw['mixb_p'])
    (oab, qt, nsat, wint, mobat, gt, _, ztail, cmp_tm, slc_tm, win_tm, mk_tm, vt16, kmean) = _proj_call(
        x2d, pw, tables, None, tm=tm, tiles_per_seq=tps, n_seq=b, chunk=GMLP_CHUNK, seg=None)
    kcvc, kcvct = _compress_prompt_call(cmp_tm.reshape(b, t, LANES), lw)
    o_ct = _nsa_prompt_call(qt, gt, kcvc, kcvct, slc_tm.reshape(b, t, LANES), win_tm.reshape(b, t, LANES), vt16)
    nb = t // MOBA_BLOCK
    kmean3 = kmean[:, 0:tm // MOBA_BLOCK].reshape(b, nb, 256)
    nbp = -(-nb // SUBLANES) * SUBLANES
    if nbp != nb:
        kmean3 = jnp.concatenate([kmean3, jnp.zeros((b, nbp - nb, 256), F32)], axis=1)
    o_dt = _moba_prompt_call(qt, kmean3, mk_tm.reshape(b, t, 256), vt16)
    x_mid = _merge_call(x2d, oab, o_ct, o_dt, lw, tm=tm)
    y, ftail = _ffn_call(x_mid, lw, None, tm=tm, tiles_per_seq=tps, n_seq=b, seg=None)
    hd = HEAD_DIM
    wlen = min(NSA_WINDOW, t)
    state = (nsat.reshape(b, 4, hd, t).transpose(0, 3, 1, 2),
             mobat.reshape(b, 2, MOBA_HEADS, hd, t).transpose(0, 4, 1, 2, 3),
             wint[:, :, t - wlen:].reshape(b, 2, hd, wlen).transpose(0, 3, 1, 2),
             ztail[:, SUBLANES - 2:], ftail[:, SUBLANES - 2:])
    return y, state


def _sample_layer(x2d, lw, tables, b, tq, layer, nsa_cache5, moba_cache5, page_table, win_state5, conv_state,
                  ffn_state):
    rows = b * tq
    hd = HEAD_DIM
    pw = dict(lw, mix=lw['mix_s'], mixb=lw['mixb_s'])
    oab, qt, nsat, wint, mobat, _, gtm, ztail, vn = _proj_call(
        x2d, pw, tables, _halos(conv_state, tq), tm=rows, tiles_per_seq=1, n_seq=b, chunk=rows, seg=tq)
    nsat, wint, mobat = nsat[0], wint[0], mobat[0]
    kcvct = _compress_paged_call(nsa_cache5, layer, page_table, lw)
    q32 = qt[0:256].reshape(NSA_HEADS, hd, b, tq).transpose(2, 0, 3, 1).reshape(b, NSA_HEADS * tq, hd)
    g4 = gtm[:, 0:3 * NSA_HEADS].reshape(b, tq, NSA_HEADS, 3).transpose(0, 2, 1, 3).reshape(b, NSA_HEADS * tq, 3)
    g32 = jnp.concatenate([g4, jnp.zeros((b, NSA_HEADS * tq, LANES - 3), F32)], axis=-1)
    newt = _per_seq_pages(nsat[2 * hd:4 * hd], b, tq).reshape(b, 2, hd, PAGE_SIZE)
    wnewt = _per_seq_pages(wint, b, tq).reshape(b, 2, hd, PAGE_SIZE)
    o32 = _nsa_sample_call(nsa_cache5, layer, page_table, q32, g32, kcvct, newt, win_state5, wnewt)
    o_ct = o32.reshape(b, NSA_HEADS, tq, hd).transpose(1, 3, 0, 2).reshape(256, rows)
    qm = qt[256:512].reshape(MOBA_HEADS, hd, b, tq).transpose(2, 0, 3, 1)
    eye = jnp.eye(MOBA_HEADS, dtype=F32)
    qbd = (qm[:, :, :, None, :] * eye[None, :, None, :, None]).reshape(b, MOBA_HEADS * tq, 256)
    mnewt = _per_seq_pages(mobat, b, tq).reshape(b, 2, 256, PAGE_SIZE)
    o_dt = _moba_sample_call(moba_cache5, layer, page_table, qbd, mnewt).transpose(2, 0, 1).reshape(256, rows)
    x_mid = _merge_call(x2d, oab, o_ct, o_dt, lw, tm=rows)
    y, ftail = _ffn_call(x_mid, lw, _halos(ffn_state, tq), tm=rows, tiles_per_seq=1, n_seq=b, seg=tq)
    wnew = wint.reshape(2, hd, b, tq).transpose(2, 0, 1, 3)
    win_all = jnp.concatenate([win_state5[layer], wnew], axis=-1)
    win_new = win_all[..., win_all.shape[-1] - NSA_WINDOW:].transpose(0, 3, 1, 2)
    state = (nsat.T.reshape(b, tq, 4, hd), mobat.T.reshape(b, tq, 2, MOBA_HEADS, hd), win_new,
             ztail[:, tq - 2:tq], ftail[:, tq - 2:tq], vn.reshape(b, tq, 256))
    return y, state


def kernel(x_prompt, x_sample, cache_nsa_kv, cache_moba_kv, state_nsa_win, state_conv, state_ffn_conv, page_table, norm1, w_in, conv_w, conv_b, gmlp_ln_g, gmlp_ln_b, gmlp_ws, gmlp_bs, nsa_q_gain, nsa_k_gain, nsa_cmp_pos, nsa_cmp_w1, nsa_cmp_w2, moba_q_gain, moba_k_gain, w_branch, w_o, norm2, w_ffn_gate, w_ffn_up, ffn_conv_w, ffn_conv_b, w_ffn_down):
    params = dict(norm1=norm1, w_in=w_in, conv_w=conv_w, conv_b=conv_b, gmlp_ln_g=gmlp_ln_g, gmlp_ln_b=gmlp_ln_b,
                  gmlp_ws=gmlp_ws, gmlp_bs=gmlp_bs, nsa_q_gain=nsa_q_gain, nsa_k_gain=nsa_k_gain,
                  nsa_cmp_pos=nsa_cmp_pos, nsa_cmp_w1=nsa_cmp_w1, nsa_cmp_w2=nsa_cmp_w2, moba_q_gain=moba_q_gain,
                  moba_k_gain=moba_k_gain, w_branch=w_branch, w_o=w_o, norm2=norm2, w_ffn_gate=w_ffn_gate,
                  w_ffn_up=w_ffn_up, ffn_conv_w=ffn_conv_w, ffn_conv_b=ffn_conv_b, w_ffn_down=w_ffn_down)
    bp, tp, _ = x_prompt.shape
    bs, ts, _ = x_sample.shape
    depth = w_in.shape[0]
    n_pool = cache_nsa_kv.shape[1]
    past = page_table.shape[1] * PAGE_SIZE
    assert ts == SUBLANES and state_nsa_win.shape[2] == NSA_WINDOW
    nsa_cache5 = jnp.transpose(cache_nsa_kv, (0, 1, 3, 4, 2))
    moba_cache5 = jnp.transpose(cache_moba_kv, (0, 1, 3, 4, 5, 2)).reshape(depth, n_pool, 2, 256, PAGE_SIZE)
    win_state5 = jnp.transpose(state_nsa_win, (0, 1, 3, 4, 2))
    tab_p = _rope_tables_t(jnp.arange(tp, dtype=jnp.int32))
    tab_s = tuple(jnp.tile(a, (1, bs)) for a in _rope_tables_t(past + jnp.arange(ts, dtype=jnp.int32)))
    yp = x_prompt.reshape(bp * tp, D_MODEL)
    ys = x_sample.reshape(bs * ts, D_MODEL)
    sp, ss = [], []
    for l in range(depth):
        lw = _layer_weights(l, params, ts, bs)
        yp, st_p = _prompt_layer(yp, lw, tab_p, bp, tp)
        ys, st_s = _sample_layer(ys, lw, tab_s, bs, ts, l, nsa_cache5, moba_cache5, page_table, win_state5,
                                 state_conv[l], state_ffn_conv[l])
        sp.append(st_p)
        ss.append(st_s)
    stack = lambda lst, k: jnp.stack([s[k] for s in lst])
    return (yp.reshape(bp, tp, D_MODEL), ys.reshape(bs, ts, D_MODEL),
            stack(sp, 0), stack(ss, 0), stack(sp, 1), stack(ss, 1), stack(sp, 2), stack(ss, 2),
            stack(sp, 3), stack(ss, 3), stack(sp, 4), stack(ss, 4), stack(ss, 5))
```

```python
import functools

import jax
import jax.numpy as jnp
from jax import lax
from jax.experimental import pallas as pl
from jax.experimental.pallas import tpu as pltpu

F32 = jnp.float32
BF16 = jnp.bfloat16

D_MODEL = 1024
HEAD_DIM = 64
ROT_DIM = HEAD_DIM // 4
ROPE_THETA = 500000.0
PAGE_SIZE = 128
BRANCH_CH = 256
N_BRANCH = 4
GMLP_GROUPS = 4
GMLP_CHUNK = 128
NSA_HEADS = 4
NSA_CMP_LEN = 32
NSA_CMP_STRIDE = 16
NSA_CMP_HIDDEN = 256
NSA_SLC_BLOCK = 64
NSA_TOPN = 16
NSA_WINDOW = 512
MOBA_HEADS = 4
MOBA_BLOCK = 256
MOBA_TOPK = 3
D_FF = 2816
EPS = 1e-6
NEG = -1e30
BIG = 1e30

SLC_TILE = 512
LANES = 128
SUBLANES = 8
VMEM_LIMIT = 56 * 1024 * 1024

C_AH, C_AB, C_AC, C_U, C_V, C_G, C_END = 0, 256, 512, 768, 1024, 1280, 1408
R_Q, R_NSA, R_WIN, R_MK, R_MV, R_G, R_END = 0, 512, 768, 896, 1152, 1408, 1424
NORMED_GROUPS = (0, 1, 2, 3, 4, 5, 6, 7, 8, 10, 12, 14, 15, 16, 17)

_NT = (((1,), (1,)), ((), ()))


def _dot(a, b):
    return jnp.dot(a, b, preferred_element_type=F32)


def _dot_nt(a, b):
    return lax.dot_general(a, b, _NT, preferred_element_type=F32)


def _split(a):
    hi = a.astype(BF16)
    lo = (a - hi.astype(F32)).astype(BF16)
    return hi, lo


def _dot_3pass(a, b):
    ah, al = _split(a)
    bh, bl = _split(b)
    return _dot(ah, bh) + _dot(ah, bl) + _dot(al, bh)


def _iota(shape, dim):
    return lax.broadcasted_iota(jnp.int32, shape, dim)


def _lane_tile(a, n):
    return a if n == 1 else jnp.concatenate([a] * n, axis=1)


def _norm_rope_t(xh, gain, cos, sin):
    ss = jnp.sum(xh * xh, axis=0, keepdims=True)
    y = xh * lax.rsqrt(ss * (1.0 / HEAD_DIM) + EPS) * gain
    half = ROT_DIM // 2
    y0, y1 = y[0:half], y[half:ROT_DIM]
    return jnp.concatenate([y0 * cos - y1 * sin, y1 * cos + y0 * sin, y[ROT_DIM:]], axis=0)


def _shift_rows(z, prev1, prev2, seg):
    rows = _iota(z.shape, 0)
    z1 = pltpu.roll(z, 1, 0)
    z2 = pltpu.roll(z, 2, 0)
    if seg is None:
        z1 = jnp.where(rows == 0, prev1, z1)
        z2 = jnp.where(rows == 0, prev2, jnp.where(rows == 1, prev1, z2))
    else:
        t = rows % seg
        z1 = jnp.where(t == 0, prev1, z1)
        z2 = jnp.where(t <= 1, prev2, z2)
    return z1, z2


def _proj_kernel(*refs, tiles_per_seq, chunk, seg, subs):
    it = iter(refs)
    x_ref, n1_ref, wtm_ref, wtr_ref, cw_ref, cb_ref, lng_ref, lnb_ref, mix_ref, mixb_ref = (next(it) for _ in range(10))
    gain_ref, cos_ref, sin_ref = (next(it) for _ in range(3))
    if seg is not None:
        h1_ref, h2_ref = next(it), next(it)
    oab_ref, qt_ref, nsat_ref, wint_ref, mobat_ref, gt_ref, gtm_ref, ztail_ref = (next(it) for _ in range(8))
    if seg is not None:
        vn_ref = next(it)
    else:
        cmp_ref, slc_ref, win_ref, mk_ref, vt_ref, kmean_ref, zprev_ref = (next(it) for _ in range(7))

    tm = x_ref.shape[0]
    sm = tm // subs
    if seg is None:
        @pl.when(pl.program_id(0) % tiles_per_seq == 0)
        def _():
            zprev_ref[...] = jnp.zeros_like(zprev_ref)

        prev = zprev_ref[...]
        kmean_ref[...] = jnp.zeros_like(kmean_ref)

    for si in range(subs):
        r0 = si * sm
        rs = slice(r0, r0 + sm)
        x = x_ref[rs, :]
        xn = (x * lax.rsqrt(jnp.mean(x * x, axis=-1, keepdims=True) + EPS) * n1_ref[...]).astype(BF16)
        proj = _dot_nt(xn, wtm_ref[...])
        projt = _dot_nt(wtr_ref[...], xn)

        z = proj[:, C_AC:C_AC + 256] * proj[:, C_AH:C_AH + 256]
        if seg is None:
            z1, z2 = _shift_rows(z, prev[SUBLANES - 1:SUBLANES, :], prev[SUBLANES - 2:SUBLANES - 1, :], None)
            prev = z[sm - SUBLANES:, :]
        else:
            z1, z2 = _shift_rows(z, h1_ref[...], h2_ref[...], seg)
            ztail_ref[...] = z.reshape(ztail_ref.shape)
        ya = z2 * cw_ref[0:1, :] + cb_ref[...] + z1 * cw_ref[1:2, :] + z * cw_ref[2:3, :]
        oab_ref[rs, 0:256] = (proj[:, C_AB:C_AB + 256] * ya).astype(BF16)

        u = jax.nn.gelu(proj[:, C_U:C_U + 256])
        v = jax.nn.gelu(proj[:, C_V:C_V + 256])
        vc = v - jnp.mean(v, axis=-1, keepdims=True)
        vn = vc * lax.rsqrt(jnp.mean(vc * vc, axis=-1, keepdims=True) + EPS) * lng_ref[...] + lnb_ref[...]
        if seg is not None:
            vn_ref[...] = vn
        vnb = vn.astype(BF16)
        grp = _iota((chunk, 256), 1) // HEAD_DIM
        for c in range(sm // chunk):
            r = _dot(mix_ref[...], vnb[c * chunk:(c + 1) * chunk, :])
            s = r[0:chunk]
            for g in range(1, GMLP_GROUPS):
                s = jnp.where(grp == g, r[g * chunk:(g + 1) * chunk], s)
            s = s + mixb_ref[...]
            oab_ref[r0 + c * chunk:r0 + (c + 1) * chunk, 256:512] = (u[c * chunk:(c + 1) * chunk, :] * s).astype(BF16)

        gtm_ref[rs, :] = jax.nn.sigmoid(proj[:, C_G:C_END])
        gt_ref[:, rs] = jax.nn.sigmoid(projt[R_G:R_END, :])

        cos, sin = cos_ref[:, rs], sin_ref[:, rs]
        rep = sm // LANES
        groups = []
        for g in range(R_MV // HEAD_DIM):
            xh = projt[g * HEAD_DIM:(g + 1) * HEAD_DIM, :]
            if g in NORMED_GROUPS:
                xh = _norm_rope_t(xh, _lane_tile(gain_ref[g * HEAD_DIM:(g + 1) * HEAD_DIM, :], rep), cos, sin)
            groups.append(xh)
        qkv = jnp.concatenate(groups, axis=0)
        qt_ref[:, rs] = qkv[R_Q:R_NSA]
        nsat_ref[0, :, rs] = qkv[R_NSA:R_WIN]
        wint_ref[0, :, rs] = qkv[R_WIN:R_MK]
        mobat_ref[0, 0:256, rs] = qkv[R_MK:R_MV]
        vmt = projt[R_MV:R_G, :]
        mobat_ref[0, 256:512, rs] = vmt

        if seg is None:
            nsa_tm = qkv[R_NSA:R_WIN].T
            cmp_ref[rs, :] = nsa_tm[:, 0:LANES]
            lane = _iota((sm, LANES), 1) - HEAD_DIM
            blk_in_tile = ((r0 + _iota((sm, LANES), 0)) // NSA_SLC_BLOCK) % (SLC_TILE // NSA_SLC_BLOCK)
            slc_ref[rs, :] = jnp.where(lane < 0, nsa_tm[:, LANES:2 * LANES],
                                       jnp.where(lane == blk_in_tile, 1.0, 0.0)).astype(BF16)
            win_ref[rs, :] = qkv[R_WIN:R_MK].T.astype(BF16)
            mk_tm = qkv[R_MK:R_MV].T
            mk_ref[rs, :] = mk_tm.astype(BF16)
            vt_ref[0, 0:256, rs] = vmt.astype(BF16)
            vt_ref[0, 256:320, rs] = qkv[R_NSA + 192:R_NSA + 256].astype(BF16)
            vt_ref[0, 320:384, rs] = qkv[R_WIN + 64:R_WIN + 128].astype(BF16)
            for r in range(sm // MOBA_BLOCK):
                b0 = r0 // MOBA_BLOCK + r
                kmean_ref[0, b0:b0 + 1, :] = jnp.sum(mk_tm[r * MOBA_BLOCK:(r + 1) * MOBA_BLOCK, :], axis=0,
                                                     keepdims=True) * (1.0 / MOBA_BLOCK)

    if seg is None:
        zprev_ref[...] = prev
        ztail_ref[0] = prev


def _proj_call(x2d, lw, tables, halos, *, tm, tiles_per_seq, n_seq, chunk, seg):
    rows = x2d.shape[0]
    nt = rows // tm
    tps = tiles_per_seq
    n_out_seq = n_seq if seg is None else 1
    t_out = rows // n_out_seq
    row = lambda w: pl.BlockSpec((tm, w), lambda i: (i, 0))
    col = lambda r: pl.BlockSpec((r, tm), lambda i: (0, i))
    seq3 = lambda r: pl.BlockSpec((1, r, tm), lambda i: (i // tps, 0, i % tps))
    full = lambda a: pl.BlockSpec(a.shape, lambda i: (0,) * a.ndim)
    n_tab = tables[0].shape[1] // tm
    tab = pl.BlockSpec((ROT_DIM // 2, tm), lambda i: (0, i % n_tab))
    ws = [lw['norm1'], lw['w_tm'], lw['w_tr'], lw['conv_w'], lw['conv_b'], lw['ln_g'], lw['ln_b'], lw['mix'],
          lw['mixb'], lw['gain_t']]
    ins = [x2d] + ws + list(tables)
    in_specs = [row(D_MODEL)] + [full(a) for a in ws] + [tab] * 2
    sds = jax.ShapeDtypeStruct
    out_shape = [sds((rows, 512), BF16), sds((512, rows), F32), sds((n_out_seq, 256, t_out), F32),
                 sds((n_out_seq, 128, t_out), F32), sds((n_out_seq, 512, t_out), F32),
                 sds((2 * SUBLANES, rows), F32), sds((rows, LANES), F32), sds((n_seq, SUBLANES, 256), F32)]
    out_specs = [row(512), col(512), seq3(256), seq3(128), seq3(512), col(2 * SUBLANES), row(LANES)]
    scratch = []
    if seg is None:
        out_specs.append(pl.BlockSpec((1, SUBLANES, 256), lambda i: (i // tps, 0, 0)))
        out_shape += [sds((rows, LANES), F32), sds((rows, LANES), BF16), sds((rows, LANES), BF16),
                      sds((rows, 256), BF16), sds((n_seq, 384, t_out), BF16), sds((nt, SUBLANES, 256), F32)]
        out_specs += [row(LANES), row(LANES), row(LANES), row(256), seq3(384),
                      pl.BlockSpec((1, SUBLANES, 256), lambda i: (i, 0, 0))]
        scratch.append(pltpu.VMEM((SUBLANES, 256), F32))
    else:
        ins += list(halos)
        in_specs += [row(256), row(256)]
        out_specs.append(pl.BlockSpec((n_seq, SUBLANES, 256), lambda i: (0, 0, 0)))
        out_shape.append(sds((rows, 256), F32))
        out_specs.append(row(256))
    return pl.pallas_call(
        functools.partial(_proj_kernel, tiles_per_seq=tps, chunk=chunk, seg=seg,
                          subs=tm // MOBA_BLOCK if seg is None else 1),
        grid=(nt,), in_specs=in_specs, out_specs=out_specs, out_shape=out_shape, scratch_shapes=scratch,
        compiler_params=pltpu.CompilerParams(dimension_semantics=("arbitrary",), vmem_limit_bytes=VMEM_LIMIT),
        name="proj_prompt" if seg is None else "proj_sample",
    )(*ins)


def _compress_compute(src_ref, wa_ref, wb_ref, pt_ref, pb_ref, w2_ref, groups):
    acc_a = jnp.zeros((groups, 2 * NSA_CMP_HIDDEN), F32)
    acc_b = jnp.zeros((groups, 2 * NSA_CMP_HIDDEN), F32)
    for p in range(NSA_CMP_STRIDE // 2):
        xp = jnp.concatenate([src_ref[pl.ds(2 * p, groups, stride=NSA_CMP_STRIDE), :],
                              src_ref[pl.ds(2 * p + 1, groups, stride=NSA_CMP_STRIDE), :]], axis=1)
        acc_a = acc_a + _dot((xp + pt_ref[p:p + 1, :]).astype(BF16), wa_ref[p])
        acc_b = acc_b + _dot((xp + pb_ref[p:p + 1, :]).astype(BF16), wb_ref[p])
    hdn = jax.nn.gelu(acc_a + pltpu.roll(acc_b, groups - 1, 0))
    return _dot(hdn.astype(BF16), w2_ref[...])


def _compress_prompt_kernel(src_ref, wa_ref, wb_ref, pt_ref, pb_ref, w2_ref, out_ref, outt_ref, *, groups):
    out = _compress_compute(src_ref.at[0], wa_ref, wb_ref, pt_ref, pb_ref, w2_ref, groups)
    out_ref[0] = out
    outt_ref[0] = out.T


def _compress_paged_kernel(ptab_ref, *refs, groups, n_seq):
    del ptab_ref
    n_in = len(refs) - 7
    pages = refs[:n_in]
    wa_ref, wb_ref, pt_ref, pb_ref, w2_ref, outt_ref, stage_ref = refs[n_in:]
    for k in range(n_in):
        pg = pages[k][0, 0].reshape(2 * HEAD_DIM, PAGE_SIZE)
        stage_ref[k * PAGE_SIZE:(k + 1) * PAGE_SIZE, :] = pg.T
    out = _compress_compute(stage_ref, wa_ref, wb_ref, pt_ref, pb_ref, w2_ref, n_seq * groups)
    for s in range(n_seq):
        outt_ref[s] = out[s * groups:(s + 1) * groups].T


def _compress_prompt_call(cmp_tm3, lw):
    b, t, _ = cmp_tm3.shape
    groups = t // NSA_CMP_STRIDE
    full = lambda a: pl.BlockSpec(a.shape, lambda i: (0,) * a.ndim)
    ws = [lw['cmp_wa'], lw['cmp_wb'], lw['cmp_pt'], lw['cmp_pb'], lw['cmp_w2']]
    return pl.pallas_call(
        functools.partial(_compress_prompt_kernel, groups=groups),
        grid=(b,),
        in_specs=[pl.BlockSpec((1, t, LANES), lambda i: (i, 0, 0))] + [full(a) for a in ws],
        out_specs=[pl.BlockSpec((1, groups, LANES), lambda i: (i, 0, 0)),
                   pl.BlockSpec((1, LANES, groups), lambda i: (i, 0, 0))],
        out_shape=[jax.ShapeDtypeStruct((b, groups, LANES), F32), jax.ShapeDtypeStruct((b, LANES, groups), F32)],
        compiler_params=pltpu.CompilerParams(dimension_semantics=("arbitrary",), vmem_limit_bytes=VMEM_LIMIT),
        name="compress_prompt",
    )(cmp_tm3, *ws)


def _compress_paged_call(cache5, layer, page_table, lw):
    b, n_pages = page_table.shape
    past = n_pages * PAGE_SIZE
    groups = past // NSA_CMP_STRIDE
    n_seq = 1
    full = lambda a: pl.BlockSpec(a.shape, lambda i, pt: (0,) * a.ndim)
    ws = [lw['cmp_wa'], lw['cmp_wb'], lw['cmp_pt'], lw['cmp_pb'], lw['cmp_w2']]
    page_specs = [pl.BlockSpec((1, 1, 2, HEAD_DIM, PAGE_SIZE), functools.partial(
        lambda i, pt, s, k: (layer, pt[i * n_seq + s, k], 0, 0, 0), s=s, k=k))
        for s in range(n_seq) for k in range(n_pages)]
    grid_spec = pltpu.PrefetchScalarGridSpec(
        num_scalar_prefetch=1, grid=(b // n_seq,),
        in_specs=page_specs + [full(a) for a in ws],
        out_specs=pl.BlockSpec((n_seq, LANES, groups), lambda i, pt: (i, 0, 0)),
        scratch_shapes=[pltpu.VMEM((n_seq * past, LANES), F32)])
    return pl.pallas_call(
        functools.partial(_compress_paged_kernel, groups=groups, n_seq=n_seq),
        grid_spec=grid_spec, out_shape=jax.ShapeDtypeStruct((b, LANES, groups), F32),
        compiler_params=pltpu.CompilerParams(dimension_semantics=("arbitrary",), vmem_limit_bytes=VMEM_LIMIT),
        name="compress_sample",
    )(page_table, *([cache5] * (n_seq * n_pages)), *ws)


def _topk_mask(score, n_cols, k):
    idx = _iota(score.shape, 1)
    rank = jnp.zeros(score.shape, jnp.int32)
    for j in range(n_cols):
        col = score[:, j:j + 1]
        rank = rank + jnp.where(col > score, 1, jnp.where(col == score, jnp.where(idx > j, 1, 0), 0))
    return rank < k


def _topk_mask_t(score, n_rows, k):
    n_tiles = score.shape[0] // SUBLANES
    tiles = [score[r * SUBLANES:(r + 1) * SUBLANES, :] for r in range(n_tiles)]
    idx = _iota(tiles[0].shape, 0)
    ranks = [jnp.zeros(tiles[0].shape, jnp.int32) for _ in range(n_tiles)]
    for j in range(n_rows):
        row = score[j:j + 1, :]
        for r in range(n_tiles):
            if r * SUBLANES > j:
                ranks[r] = jnp.where(row >= tiles[r], ranks[r] + 1, ranks[r])
            elif (r + 1) * SUBLANES - 1 <= j:
                ranks[r] = jnp.where(row > tiles[r], ranks[r] + 1, ranks[r])
            else:
                ranks[r] = ranks[r] + jnp.where(idx + r * SUBLANES > j, jnp.where(row >= tiles[r], 1, 0),
                                                jnp.where(row > tiles[r], 1, 0))
    return jnp.concatenate(ranks, axis=0) < k


def _softmax_axis(s, ok, axis):
    s = jnp.where(ok, s, NEG)
    m = jnp.max(s, axis=axis, keepdims=True)
    p = jnp.where(ok, jnp.exp(s - m), 0.0)
    l = jnp.sum(p, axis=axis, keepdims=True)
    return p / jnp.where(l > 0.0, l, 1.0)


def _forced_importance(imp, blk, cur):
    forced = (blk == 0) | (blk == cur) | (blk == cur - 1)
    imp = jnp.where(forced, BIG, imp)
    return jnp.where(blk > cur, NEG, imp)


def _nsa_prompt_kernel(qt_ref, gt_ref, kcvc_ref, kcvct_ref, slc_ref, win_ref, vt_ref, o_ref, sel_ref, s_ref, *, t, qb, tk):
    iq = pl.program_id(1)
    s0 = iq * qb
    nq = NSA_HEADS * qb
    qt = qt_ref[...] * (HEAD_DIM ** -0.5)
    qst = jnp.concatenate([qt[h * HEAD_DIM:(h + 1) * HEAD_DIM, :] for h in range(NSA_HEADS)], axis=1)
    qst = jnp.concatenate([qst, jnp.zeros_like(qst)], axis=0).astype(BF16)
    qpos_q = s0 + _iota((1, qb), 1)
    qpos = _lane_tile(qpos_q, NSA_HEADS)

    wlen = NSA_WINDOW + qb
    start = pl.multiple_of(jnp.maximum(s0 - NSA_WINDOW, 0), qb)
    sw = _dot(win_ref[0, pl.ds(start, wlen), :], qst)
    back = (qpos - start) - _iota(sw.shape, 0)
    sw = jnp.where((back & -NSA_WINDOW) == 0, sw, NEG)
    ew = jnp.exp(sw - jnp.max(sw, axis=0, keepdims=True))
    o_win = (_dot(vt_ref[0, HEAD_DIM:2 * HEAD_DIM, pl.ds(start, wlen)], ew.astype(BF16))
             / jnp.sum(ew, axis=0, keepdims=True))

    n_cmp = (t - NSA_CMP_LEN) // NSA_CMP_STRIDE + 1
    n_blk = t // NSA_SLC_BLOCK
    kcvc = kcvc_ref[0].astype(BF16)
    g = kcvc.shape[0]
    s = _dot(kcvc, qst)
    n_idx = _iota(s.shape, 0)
    ok = (n_idx * NSA_CMP_STRIDE + (NSA_CMP_LEN - 1) <= qpos) & (n_idx < n_cmp)
    s = jnp.where(ok, s, NEG)
    m = jnp.max(s, axis=0, keepdims=True)
    e = jnp.exp(s - m)
    p = e * jnp.where(m > 0.5 * NEG, 1.0 / jnp.sum(e, axis=0, keepdims=True), 0.0)
    o_cmp = _dot(kcvct_ref[0, HEAD_DIM:2 * HEAD_DIM, :].astype(BF16), p.astype(BF16))
    psum = p[:, 0:qb]
    for h in range(1, NSA_HEADS):
        psum = psum + p[:, h * qb:(h + 1) * qb]
    cj = _iota((n_blk, g), 0)
    cn = _iota((n_blk, g), 1)
    cover = ((cn * NSA_CMP_STRIDE <= cj * NSA_SLC_BLOCK + (NSA_SLC_BLOCK - 1))
             & (cn * NSA_CMP_STRIDE + (NSA_CMP_LEN - 1) >= cj * NSA_SLC_BLOCK) & (cn < n_cmp)).astype(BF16)
    p_hi, p_lo = _split(psum)
    imp = _dot(cover, p_hi) + _dot(cover, p_lo)
    imp = _forced_importance(imp, _iota(imp.shape, 0), qpos_q // NSA_SLC_BLOCK)
    cur = qpos_q // NSA_SLC_BLOCK
    picked = _topk_mask_t(imp, n_blk, NSA_TOPN) & (_iota(imp.shape, 0) <= cur)
    sel_ref[...] = _lane_tile(jnp.where(picked, 0.0, NEG), NSA_HEADS)

    bpt = tk // NSA_SLC_BLOCK
    q64 = qst[0:HEAD_DIM].astype(F32)
    zpad = jnp.zeros((LANES - HEAD_DIM - bpt, nq), F32)
    kd = s0 // tk

    def scores(kt):
        k0 = pl.multiple_of(kt * tk, tk)
        slab = sel_ref[pl.ds(pl.multiple_of(kt * bpt, bpt), bpt), :]
        w = jnp.concatenate([q64, slab, zpad], axis=0).astype(BF16)
        return _dot(slc_ref[0, pl.ds(k0, tk), :], w)

    def fold(x, op):
        return op(x.reshape(tk // SUBLANES, SUBLANES, nq), axis=0)

    def sweep1(kt, m8):
        sc = scores(kt)
        s_ref[kt] = sc
        return jnp.maximum(m8, fold(sc, jnp.max))

    m8 = lax.fori_loop(0, kd, sweep1, jnp.full((SUBLANES, nq), NEG, F32))
    sc = scores(kd)
    sc = jnp.where(kd * tk + _iota(sc.shape, 0) <= qpos, sc, NEG)
    s_ref[kd] = sc
    m = jnp.max(jnp.maximum(m8, fold(sc, jnp.max)), axis=0, keepdims=True)

    def sweep2(kt, carry):
        l8, acc = carry
        k0 = pl.multiple_of(kt * tk, tk)
        pp = jnp.exp(s_ref[kt] - m)
        return l8 + fold(pp, jnp.sum), acc + _dot(vt_ref[0, 0:HEAD_DIM, pl.ds(k0, tk)], pp.astype(BF16))

    l8, acc = lax.fori_loop(0, kd + 1, sweep2, (jnp.zeros((SUBLANES, nq), F32), jnp.zeros((HEAD_DIM, nq), F32)))
    o_slc = acc / jnp.sum(l8, axis=0, keepdims=True)

    gt = gt_ref[...]
    outs = []
    for h in range(NSA_HEADS):
        c = slice(h * qb, (h + 1) * qb)
        outs.append(gt[3 * h:3 * h + 1, :] * o_cmp[:, c] + gt[3 * h + 1:3 * h + 2, :] * o_slc[:, c]
                    + gt[3 * h + 2:3 * h + 3, :] * o_win[:, c])
    o_ref[...] = jnp.concatenate(outs, axis=0)


def _nsa_prompt_call(qt, gt, kcvc, kcvct, slc_tm3, win_tm3, vt3, *, qb=256, tk=SLC_TILE):
    b, t, _ = slc_tm3.shape
    nq = t // qb
    g = kcvc.shape[1]
    assert t % tk == 0 and tk % qb == 0 and t >= NSA_WINDOW + qb
    return pl.pallas_call(
        functools.partial(_nsa_prompt_kernel, t=t, qb=qb, tk=tk),
        grid=(b, nq),
        in_specs=[pl.BlockSpec((256, qb), lambda i, j: (0, i * nq + j)),
                  pl.BlockSpec((2 * SUBLANES, qb), lambda i, j: (0, i * nq + j)),
                  pl.BlockSpec((1, g, LANES), lambda i, j: (i, 0, 0)),
                  pl.BlockSpec((1, LANES, g), lambda i, j: (i, 0, 0)),
                  pl.BlockSpec((1, t, LANES), lambda i, j: (i, 0, 0)),
                  pl.BlockSpec((1, t, LANES), lambda i, j: (i, 0, 0)),
                  pl.BlockSpec((1, LANES, t), lambda i, j: (i, 2, 0))],
        out_specs=pl.BlockSpec((256, qb), lambda i, j: (0, i * nq + j)),
        out_shape=jax.ShapeDtypeStruct((256, b * t), F32),
        scratch_shapes=[pltpu.VMEM((t // NSA_SLC_BLOCK, NSA_HEADS * qb), F32),
                        pltpu.VMEM((t // tk, tk, NSA_HEADS * qb), F32)],
        compiler_params=pltpu.CompilerParams(dimension_semantics=("arbitrary", "arbitrary"),
                                             vmem_limit_bytes=VMEM_LIMIT),
        name="nsa_prompt",
    )(qt, gt, kcvc, kcvct, slc_tm3, win_tm3, vt3)


def _nsa_sample_kernel(ptab_ref, *refs, past, tq, n_seq):
    del ptab_ref
    n_in = len(refs) - 10
    n_pages = n_in // n_seq
    pages = refs[:n_in]
    q_ref, g_ref, kcvct_ref, new_ref, wst_ref, wnew_ref, o_ref, kt_ref, vt_ref, e_ref = refs[n_in:]

    @pl.when(pl.program_id(0) == 0)
    def _():
        e_ref[...] = (_iota(e_ref.shape, 0) == _iota(e_ref.shape, 1) // NSA_SLC_BLOCK).astype(BF16)

    for sq in range(n_seq):
        for k in range(n_pages):
            pg = pages[sq * n_pages + k][0, 0]
            kt_ref[sq, :, k * PAGE_SIZE:(k + 1) * PAGE_SIZE] = pg[0].astype(BF16)
            vt_ref[sq, :, k * PAGE_SIZE:(k + 1) * PAGE_SIZE] = pg[1].astype(BF16)

    for sq in range(n_seq):
        rows = NSA_HEADS * tq
        qs = (q_ref[sq] * (HEAD_DIM ** -0.5)).astype(BF16)
        qpos_q = past + _iota((tq, 1), 0)
        qpos = jnp.concatenate([qpos_q] * NSA_HEADS, axis=0)
        l_all = past + tq
        n_cmp = (l_all - NSA_CMP_LEN) // NSA_CMP_STRIDE + 1
        n_blk = -(-l_all // NSA_SLC_BLOCK)
        nb_pad = -(-n_blk // LANES) * LANES

        kcvct = kcvct_ref[sq].astype(BF16)
        g = kcvct.shape[1]
        s = _dot(qs, kcvct[0:HEAD_DIM])
        n_idx = _iota(s.shape, 1)
        ok = (n_idx * NSA_CMP_STRIDE + (NSA_CMP_LEN - 1) <= qpos) & (n_idx < n_cmp)
        p = _softmax_axis(s, ok, 1)
        o_cmp = _dot_nt(p.astype(BF16), kcvct[HEAD_DIM:2 * HEAD_DIM])
        psum = p[0:tq]
        for h in range(1, NSA_HEADS):
            psum = psum + p[h * tq:(h + 1) * tq]
        cn = _iota((g, nb_pad), 0)
        cj = _iota((g, nb_pad), 1)
        cover = ((cn * NSA_CMP_STRIDE <= cj * NSA_SLC_BLOCK + (NSA_SLC_BLOCK - 1))
                 & (cn * NSA_CMP_STRIDE + (NSA_CMP_LEN - 1) >= cj * NSA_SLC_BLOCK) & (cn < n_cmp)).astype(BF16)
        p_hi, p_lo = _split(psum)
        imp = _dot(p_hi, cover) + _dot(p_lo, cover)
        imp = _forced_importance(imp, _iota(imp.shape, 1), qpos_q // NSA_SLC_BLOCK)
        picked = _topk_mask(imp, n_blk, NSA_TOPN)

        nbe = e_ref.shape[0]
        bias = jnp.where(picked[:, 0:nbe], 0.0, NEG).astype(BF16)
        bias = jnp.concatenate([bias] * NSA_HEADS, axis=0)
        sc = _dot(qs, kt_ref[sq]) + _dot(bias, e_ref[...])
        new_ok = past + _iota((rows, PAGE_SIZE), 1) <= qpos
        sc_new = jnp.where(new_ok, _dot(qs, new_ref[sq, 0].astype(BF16)), NEG)
        m = jnp.maximum(jnp.max(sc, axis=-1, keepdims=True), jnp.max(sc_new, axis=-1, keepdims=True))
        pp = jnp.exp(sc - m)
        pp_new = jnp.exp(sc_new - m)
        l = jnp.sum(pp, axis=-1, keepdims=True) + jnp.sum(pp_new, axis=-1, keepdims=True)
        o_slc = (_dot_nt(pp.astype(BF16), vt_ref[sq])
                 + _dot_nt(pp_new.astype(BF16), new_ref[sq, 1].astype(BF16))) / l

        kw = jnp.concatenate([wst_ref[0, sq, 0], wnew_ref[sq, 0]], axis=1).astype(BF16)
        vw = jnp.concatenate([wst_ref[0, sq, 1], wnew_ref[sq, 1]], axis=1).astype(BF16)
        sw = _dot(qs, kw)
        kpos = past - NSA_WINDOW + _iota(sw.shape, 1)
        okw = (kpos <= qpos) & (kpos > qpos - NSA_WINDOW) & (kpos >= 0)
        o_win = _dot_nt(_softmax_axis(sw, okw, 1).astype(BF16), vw)

        gg = g_ref[sq]
        o_ref[sq] = gg[:, 0:1] * o_cmp + gg[:, 1:2] * o_slc + gg[:, 2:3] * o_win


def _nsa_sample_call(cache5, layer, page_table, q32, g32, kcvct, newt, wstate5, wnewt):
    b, n_pages = page_table.shape
    past = n_pages * PAGE_SIZE
    tq = q32.shape[1] // NSA_HEADS
    n_seq = 2 if b % 2 == 0 else 1
    nbe = -(-(past // NSA_SLC_BLOCK) // LANES) * LANES
    assert tq <= NSA_SLC_BLOCK and past >= NSA_WINDOW
    assert (past + tq - NSA_CMP_LEN) // NSA_CMP_STRIDE + 1 <= past // NSA_CMP_STRIDE - 1
    rows = NSA_HEADS * tq
    page_specs = [pl.BlockSpec((1, 1, 2, HEAD_DIM, PAGE_SIZE), functools.partial(
        lambda i, pt, s, k: (layer, pt[i * n_seq + s, k], 1, 0, 0), s=s, k=k))
        for s in range(n_seq) for k in range(n_pages)]
    per_b = lambda a: pl.BlockSpec((n_seq,) + a.shape[1:], lambda i, pt: (i,) + (0,) * (a.ndim - 1))
    grid_spec = pltpu.PrefetchScalarGridSpec(
        num_scalar_prefetch=1, grid=(b // n_seq,),
        in_specs=page_specs + [per_b(q32), per_b(g32), per_b(kcvct), per_b(newt),
                               pl.BlockSpec((1, n_seq, 2, HEAD_DIM, NSA_WINDOW), lambda i, pt: (layer, i, 0, 0, 0)),
                               per_b(wnewt)],
        out_specs=pl.BlockSpec((n_seq, rows, HEAD_DIM), lambda i, pt: (i, 0, 0)),
        scratch_shapes=[pltpu.VMEM((n_seq, HEAD_DIM, past), BF16), pltpu.VMEM((n_seq, HEAD_DIM, past), BF16),
                        pltpu.VMEM((nbe, past), BF16)])
    return pl.pallas_call(
        functools.partial(_nsa_sample_kernel, past=past, tq=tq, n_seq=n_seq),
        grid_spec=grid_spec, out_shape=jax.ShapeDtypeStruct((b, rows, HEAD_DIM), F32),
        compiler_params=pltpu.CompilerParams(dimension_semantics=("arbitrary",), vmem_limit_bytes=VMEM_LIMIT),
        name="nsa_sample",
    )(page_table, *([cache5] * (n_seq * n_pages)), q32, g32, kcvct, newt, wstate5, wnewt)


def _moba_prompt_kernel(qt_ref, kmean_ref, k_ref, vt_ref, o_ref, sel_ref, acc_ref, s_ref, *, nb):
    i = pl.program_id(1)
    r0 = pl.multiple_of(i * MOBA_BLOCK, MOBA_BLOCK)
    qb = MOBA_BLOCK
    nq = MOBA_HEADS * qb
    qt = qt_ref[...]
    rowgrp = _iota(qt.shape, 0) // HEAD_DIM
    kmean = kmean_ref[0]
    qpad = []
    for h in range(MOBA_HEADS):
        qh = jnp.where(rowgrp == h, qt, 0.0)
        gs = _dot_3pass(kmean, qh)
        n_idx = _iota(gs.shape, 0)
        gs = jnp.where(n_idx < i, gs, NEG)
        sel_ref[:, h * qb:(h + 1) * qb] = jnp.where(_topk_mask_t(gs, nb, MOBA_TOPK) & (n_idx < i), 0.0, NEG)
        qpad.append((qh * (HEAD_DIM ** -0.5)).astype(BF16))
    qcat = jnp.concatenate(qpad, axis=1)

    def block(n, width=1):
        k0 = pl.multiple_of(n * MOBA_BLOCK, width * MOBA_BLOCK)
        return k_ref[0, pl.ds(k0, width * MOBA_BLOCK), :], vt_ref[0, :, pl.ds(k0, width * MOBA_BLOCK)]

    def fold(x, op):
        return op(x.reshape(x.shape[0] // SUBLANES, SUBLANES, nq), axis=0)

    n_pairs = (i + 1) // 2

    def sweep1(n2, m8):
        bias = jnp.stack([sel_ref[pl.ds(2 * n2, 1), :], sel_ref[pl.ds(2 * n2 + 1, 1), :]])
        sc = (_dot(block(2 * n2, 2)[0], qcat).reshape(2, MOBA_BLOCK, nq) + bias).reshape(2 * MOBA_BLOCK, nq)
        s_ref[n2] = sc
        return jnp.maximum(m8, fold(sc, jnp.max))

    m8 = lax.fori_loop(0, n_pairs, sweep1, jnp.full((SUBLANES, nq), NEG, F32))
    own_k, own_v = block(i)
    causal = _iota((MOBA_BLOCK, nq), 0) <= _iota((MOBA_BLOCK, nq), 1) % qb
    own_s = jnp.where(causal, _dot(own_k, qcat), NEG)
    m = jnp.max(jnp.maximum(m8, fold(own_s, jnp.max)), axis=0, keepdims=True)

    acc_ref[...] = jnp.zeros_like(acc_ref)

    def accumulate(p, vb):
        pb = p.astype(BF16)
        for h in range(MOBA_HEADS):
            acc_ref[h] += _dot(vb[h * HEAD_DIM:(h + 1) * HEAD_DIM, :], pb[:, h * qb:(h + 1) * qb])
        return fold(p, jnp.sum)

    def sweep2(n2, l8):
        return l8 + accumulate(jnp.exp(s_ref[n2] - m), block(2 * n2, 2)[1])

    l8 = lax.fori_loop(0, n_pairs, sweep2, jnp.zeros((SUBLANES, nq), F32))
    l = jnp.sum(l8 + accumulate(jnp.exp(own_s - m), own_v), axis=0, keepdims=True)
    o_ref[...] = jnp.concatenate([acc_ref[h] / l[:, h * qb:(h + 1) * qb] for h in range(MOBA_HEADS)], axis=0)


def _moba_prompt_call(qt, kmean3, mk_tm3, vt3):
    b, t, _ = mk_tm3.shape
    nb = t // MOBA_BLOCK
    nbp = kmean3.shape[1]
    return pl.pallas_call(
        functools.partial(_moba_prompt_kernel, nb=nb),
        grid=(b, nb),
        in_specs=[pl.BlockSpec((256, MOBA_BLOCK), lambda i, j: (1, i * nb + j)),
                  pl.BlockSpec((1, nbp, 256), lambda i, j: (i, 0, 0)),
                  pl.BlockSpec((1, t, 256), lambda i, j: (i, 0, 0)),
                  pl.BlockSpec((1, 256, t), lambda i, j: (i, 0, 0))],
        out_specs=pl.BlockSpec((256, MOBA_BLOCK), lambda i, j: (0, i * nb + j)),
        out_shape=jax.ShapeDtypeStruct((256, b * t), F32),
        scratch_shapes=[pltpu.VMEM((nbp, MOBA_HEADS * MOBA_BLOCK), F32),
                        pltpu.VMEM((MOBA_HEADS, HEAD_DIM, MOBA_BLOCK), F32),
                        pltpu.VMEM((nb // 2, 2 * MOBA_BLOCK, MOBA_HEADS * MOBA_BLOCK), F32)],
        compiler_params=pltpu.CompilerParams(dimension_semantics=("arbitrary", "arbitrary"),
                                             vmem_limit_bytes=VMEM_LIMIT),
        name="moba_prompt",
    )(qt, kmean3, mk_tm3, vt3)


def _moba_sample_kernel(ptab_ref, *refs, past, tq, n_pages):
    del ptab_ref
    pages = refs[:n_pages]
    q_ref, new_ref, o_ref, s_ref, e_ref = refs[n_pages:]
    ppb = MOBA_BLOCK // PAGE_SIZE
    n_past = past // MOBA_BLOCK

    @pl.when(pl.program_id(0) == 0)
    def _():
        e_ref[...] = (_iota(e_ref.shape, 0) == _iota(e_ref.shape, 1) // MOBA_BLOCK).astype(BF16)

    q = q_ref[0]
    rows = q.shape[0]
    qb = (q * (HEAD_DIM ** -0.5)).astype(BF16)

    def block(n, which):
        return jnp.concatenate([pages[n * ppb + c][0, 0, which] for c in range(ppb)], axis=1)

    lane = _iota((256, LANES), 1)
    kmean = jnp.zeros((256, LANES), F32)
    for n in range(n_past):
        kb = block(n, 0)
        kmean = jnp.where(lane == n, jnp.sum(kb, axis=1, keepdims=True) * (1.0 / MOBA_BLOCK), kmean)
        s_ref[:, n * MOBA_BLOCK:(n + 1) * MOBA_BLOCK] = _dot(qb, kb.astype(BF16))
    gs = _dot_3pass(q, kmean)
    n_idx = _iota(gs.shape, 1)
    gs = jnp.where(n_idx < n_past, gs, NEG)
    bias = jnp.where(_topk_mask(gs, n_past, MOBA_TOPK) & (n_idx < n_past), 0.0, NEG).astype(BF16)
    s = s_ref[...] + _dot(bias, e_ref[...])
    own_ok = _iota((rows, PAGE_SIZE), 1) <= _iota((rows, PAGE_SIZE), 0) % tq
    s_own = jnp.where(own_ok, _dot(qb, new_ref[0, 0].astype(BF16)), NEG)
    m = jnp.maximum(jnp.max(s, axis=-1, keepdims=True), jnp.max(s_own, axis=-1, keepdims=True))
    p = jnp.exp(s - m)
    p_own = jnp.exp(s_own - m)
    l = jnp.sum(p, axis=-1, keepdims=True) + jnp.sum(p_own, axis=-1, keepdims=True)
    pb = p.astype(BF16)
    acc = _dot_nt(p_own.astype(BF16), new_ref[0, 1].astype(BF16))
    for n in range(n_past):
        acc = acc + _dot_nt(pb[:, n * MOBA_BLOCK:(n + 1) * MOBA_BLOCK], block(n, 1).astype(BF16))
    oh = acc / l
    grp = _iota((tq, 256), 1) // HEAD_DIM
    out = oh[0:tq]
    for h in range(1, MOBA_HEADS):
        out = jnp.where(grp == h, oh[h * tq:(h + 1) * tq], out)
    o_ref[0] = out


def _moba_sample_call(cache5, layer, page_table, qbd, newt):
    b, n_pages = page_table.shape
    past = n_pages * PAGE_SIZE
    rows = qbd.shape[1]
    tq = rows // MOBA_HEADS
    assert past % MOBA_BLOCK == 0 and tq <= PAGE_SIZE and past // MOBA_BLOCK <= LANES
    page_specs = [pl.BlockSpec((1, 1, 2, 256, PAGE_SIZE), functools.partial(
        lambda i, pt, k: (layer, pt[i, k], 0, 0, 0), k=k)) for k in range(n_pages)]
    per_b = lambda a: pl.BlockSpec((1,) + a.shape[1:], lambda i, pt: (i,) + (0,) * (a.ndim - 1))
    grid_spec = pltpu.PrefetchScalarGridSpec(
        num_scalar_prefetch=1, grid=(b,),
        in_specs=page_specs + [per_b(qbd), per_b(newt)],
        out_specs=pl.BlockSpec((1, tq, 256), lambda i, pt: (i, 0, 0)),
        scratch_shapes=[pltpu.VMEM((rows, past), F32), pltpu.VMEM((LANES, past), BF16)])
    return pl.pallas_call(
        functools.partial(_moba_sample_kernel, past=past, tq=tq, n_pages=n_pages),
        grid_spec=grid_spec, out_shape=jax.ShapeDtypeStruct((b, tq, 256), F32),
        compiler_params=pltpu.CompilerParams(dimension_semantics=("arbitrary",), vmem_limit_bytes=VMEM_LIMIT),
        name="moba_sample",
    )(page_table, *([cache5] * n_pages), qbd, newt)


def _merge_kernel(x_ref, oab_ref, oct_ref, odt_ref, n1_ref, wgt_ref, wbr_ref, wo_ref, y_ref, *, subs):
    sm = x_ref.shape[0] // subs
    for si in range(subs):
        rs = slice(si * sm, (si + 1) * sm)
        x = x_ref[rs, :]
        xn = (x * lax.rsqrt(jnp.mean(x * x, axis=-1, keepdims=True) + EPS) * n1_ref[...]).astype(BF16)
        branches = [oab_ref[rs, 0:256], oab_ref[rs, 256:512], oct_ref[:, rs].T.astype(BF16),
                    odt_ref[:, rs].T.astype(BF16)]
        merged = jnp.zeros(x.shape, F32)
        for n in range(N_BRANCH):
            g = jax.nn.sigmoid(_dot_nt(xn, wgt_ref[n * D_MODEL:(n + 1) * D_MODEL, :]))
            merged = merged + g * _dot(branches[n], wbr_ref[n])
        y_ref[rs, :] = x + _dot(merged.astype(BF16), wo_ref[...])


def _merge_call(x2d, oab, oct, odt, lw, *, tm):
    rows = x2d.shape[0]
    row = lambda w: pl.BlockSpec((tm, w), lambda i: (i, 0))
    col = lambda r: pl.BlockSpec((r, tm), lambda i: (0, i))
    full = lambda a: pl.BlockSpec(a.shape, lambda i: (0,) * a.ndim)
    ws = [lw['norm1'], lw['wg_t'], lw['w_branch'], lw['w_o']]
    return pl.pallas_call(
        functools.partial(_merge_kernel, subs=2 if tm % (2 * LANES) == 0 else 1), grid=(rows // tm,),
        in_specs=[row(D_MODEL), row(512), col(256), col(256)] + [full(a) for a in ws],
        out_specs=row(D_MODEL), out_shape=jax.ShapeDtypeStruct((rows, D_MODEL), F32),
        compiler_params=pltpu.CompilerParams(dimension_semantics=("arbitrary",), vmem_limit_bytes=VMEM_LIMIT),
        name="merge",
    )(x2d, oab, oct, odt, *ws)


def _ffn_kernel(*refs, tiles_per_seq, seg, subs):
    it = iter(refs)
    x_ref, n2_ref, wgate_ref, wup_ref, cw_ref, cb_ref, wdown_ref = (next(it) for _ in range(7))
    if seg is not None:
        h1_ref, h2_ref = next(it), next(it)
    y_ref, tail_ref = next(it), next(it)
    if seg is None:
        prev_ref = next(it)

    tm = x_ref.shape[0]
    sm = tm // subs
    if seg is None:
        @pl.when(pl.program_id(0) % tiles_per_seq == 0)
        def _():
            prev_ref[...] = jnp.zeros_like(prev_ref)

        prev = prev_ref[...]

    for si in range(subs):
        rs = slice(si * sm, (si + 1) * sm)
        x = x_ref[rs, :]
        h2 = (x * lax.rsqrt(jnp.mean(x * x, axis=-1, keepdims=True) + EPS) * n2_ref[...]).astype(BF16)
        gate = _dot(h2, wgate_ref[...])
        if seg is None:
            g1, g2 = _shift_rows(gate, prev[SUBLANES - 1:SUBLANES, :], prev[SUBLANES - 2:SUBLANES - 1, :], None)
            prev = gate[sm - SUBLANES:, :]
        else:
            g1, g2 = _shift_rows(gate, h1_ref[...], h2_ref[...], seg)
            tail_ref[...] = gate.reshape(tail_ref.shape)
        ac = g2 * cw_ref[0:1, :] + cb_ref[...] + g1 * cw_ref[1:2, :] + gate * cw_ref[2:3, :]
        act = jax.nn.silu(ac) * _dot(h2, wup_ref[...])
        y_ref[rs, :] = x + _dot(act.astype(BF16), wdown_ref[...])

    if seg is None:
        prev_ref[...] = prev
        tail_ref[0] = prev


def _ffn_call(x2d, lw, halos, *, tm, tiles_per_seq, n_seq, seg):
    rows = x2d.shape[0]
    row = lambda w: pl.BlockSpec((tm, w), lambda i: (i, 0))
    full = lambda a: pl.BlockSpec(a.shape, lambda i: (0,) * a.ndim)
    ws = [lw['norm2'], lw['w_gate'], lw['w_up'], lw['ffn_conv_w'], lw['ffn_conv_b'], lw['w_down']]
    ins = [x2d] + ws
    in_specs = [row(D_MODEL)] + [full(a) for a in ws]
    scratch = []
    if seg is None:
        tail_spec = pl.BlockSpec((1, SUBLANES, D_FF), lambda i: (i // tiles_per_seq, 0, 0))
        scratch.append(pltpu.VMEM((SUBLANES, D_FF), F32))
    else:
        ins += list(halos)
        in_specs += [row(D_FF), row(D_FF)]
        tail_spec = pl.BlockSpec((n_seq, SUBLANES, D_FF), lambda i: (0, 0, 0))
    return pl.pallas_call(
        functools.partial(_ffn_kernel, tiles_per_seq=tiles_per_seq, seg=seg, subs=2 if seg is None else 1),
        grid=(rows // tm,), in_specs=in_specs, out_specs=[row(D_MODEL), tail_spec],
        out_shape=[jax.ShapeDtypeStruct((rows, D_MODEL), F32), jax.ShapeDtypeStruct((n_seq, SUBLANES, D_FF), F32)],
        scratch_shapes=scratch,
        compiler_params=pltpu.CompilerParams(dimension_semantics=("arbitrary",), vmem_limit_bytes=VMEM_LIMIT),
        name="ffn_prompt" if seg is None else "ffn_sample",
    )(*ins)


def _rope_tables_t(pos):
    half = ROT_DIM // 2
    inv = jnp.power(jnp.float32(ROPE_THETA), -jnp.arange(half, dtype=F32) / half)
    ang = inv[:, None] * pos.astype(F32)[None, :]
    return jnp.cos(ang), jnp.sin(ang)


def _layer_weights(l, p, sample_tq, dec_batch):
    wt = jnp.transpose(p['w_in'], (2, 0, 1))[:, l, :]
    o_ck, o_cv, o_g = 1536, 1728, 1920
    o_dq = o_g + 3 * NSA_HEADS
    o_br = o_dq + 3 * 256
    hd = HEAD_DIM
    w_tm = jnp.concatenate([wt[0:1280], wt[o_g:o_g + 12], jnp.zeros((LANES - 12, D_MODEL), F32)], axis=0)
    w_tr = jnp.concatenate([
        wt[1280:1536], wt[o_dq:o_dq + 256],
        wt[o_ck:o_ck + hd], wt[o_cv:o_cv + hd], wt[o_ck + hd:o_ck + 2 * hd], wt[o_cv + hd:o_cv + 2 * hd],
        wt[o_ck + 2 * hd:o_ck + 3 * hd], wt[o_cv + 2 * hd:o_cv + 3 * hd],
        wt[o_dq + 256:o_dq + 768], wt[o_g:o_g + 12], jnp.zeros((2 * SUBLANES - 12, D_MODEL), F32)], axis=0)
    lw = {'w_tm': w_tm.astype(BF16), 'w_tr': w_tr.astype(BF16), 'wg_t': wt[o_br:].astype(BF16)}
    row = lambda a: a.reshape(1, -1)
    lw['norm1'] = row(p['norm1'][l])
    lw['norm2'] = row(p['norm2'][l])
    lw['conv_w'] = p['conv_w'][l]
    lw['conv_b'] = row(p['conv_b'][l])
    lw['ln_g'] = row(p['gmlp_ln_g'][l])
    lw['ln_b'] = row(p['gmlp_ln_b'][l])
    tril = jnp.tril(jnp.ones((GMLP_CHUNK, GMLP_CHUNK), bool))
    ws = jnp.where(tril[None], p['gmlp_ws'][l], 0.0)
    bs = p['gmlp_bs'][l]
    lw['mix_p'] = ws.reshape(GMLP_GROUPS * GMLP_CHUNK, GMLP_CHUNK).astype(BF16)
    lw['mixb_p'] = jnp.repeat(bs.T, HEAD_DIM, axis=1)
    eye = jnp.eye(dec_batch, dtype=F32)
    lw['mix_s'] = jnp.concatenate([jnp.kron(eye, ws[g, :sample_tq, :sample_tq]) for g in range(GMLP_GROUPS)],
                                  axis=0).astype(BF16)
    lw['mixb_s'] = jnp.tile(jnp.repeat(bs.T[:sample_tq], HEAD_DIM, axis=1), (dec_batch, 1))
    kg = p['nsa_k_gain'][l]
    one = jnp.ones((hd,), F32)
    gains = jnp.concatenate([jnp.tile(p['nsa_q_gain'][l], NSA_HEADS), jnp.tile(p['moba_q_gain'][l], MOBA_HEADS),
                             kg[0], one, kg[1], one, kg[2], one, jnp.tile(p['moba_k_gain'][l], MOBA_HEADS)])
    lw['gain_t'] = jnp.broadcast_to(gains[:, None], (R_MV, LANES))
    w1c = p['nsa_cmp_w1'][l].reshape(2, NSA_CMP_LEN, HEAD_DIM, NSA_CMP_HIDDEN)
    pos = p['nsa_cmp_pos'][l]
    zero = jnp.zeros((HEAD_DIM, NSA_CMP_HIDDEN), F32)

    def pair_w(r_off):
        mats = []
        for pp in range(NSA_CMP_STRIDE // 2):
            blocks = []
            for r in (2 * pp, 2 * pp + 1):
                blocks.append(jnp.concatenate([w1c[0, r + r_off], zero], axis=1))
                blocks.append(jnp.concatenate([zero, w1c[1, r + r_off]], axis=1))
            mats.append(jnp.concatenate(blocks, axis=0))
        return jnp.stack(mats).astype(BF16)

    def pair_pos(r_off):
        return jnp.stack([jnp.concatenate([pos[0, 2 * pp + r_off], pos[1, 2 * pp + r_off],
                                           pos[0, 2 * pp + 1 + r_off], pos[1, 2 * pp + 1 + r_off]])
                          for pp in range(NSA_CMP_STRIDE // 2)])

    lw['cmp_wa'], lw['cmp_wb'] = pair_w(0), pair_w(NSA_CMP_STRIDE)
    lw['cmp_pt'], lw['cmp_pb'] = pair_pos(0), pair_pos(NSA_CMP_STRIDE)
    w2 = p['nsa_cmp_w2'][l]
    z2 = jnp.zeros((NSA_CMP_HIDDEN, HEAD_DIM), F32)
    lw['cmp_w2'] = jnp.concatenate([jnp.concatenate([w2[0], z2], axis=1),
                                    jnp.concatenate([z2, w2[1]], axis=1)], axis=0).astype(BF16)
    lw['w_branch'] = p['w_branch'][l].astype(BF16)
    lw['w_o'] = p['w_o'][l].astype(BF16)
    lw['w_gate'] = p['w_ffn_gate'][l].astype(BF16)
    lw['w_up'] = p['w_ffn_up'][l].astype(BF16)
    lw['w_down'] = p['w_ffn_down'][l].astype(BF16)
    lw['ffn_conv_w'] = p['ffn_conv_w'][l]
    lw['ffn_conv_b'] = row(p['ffn_conv_b'][l])
    return lw


def _halos(buf, tq):
    b, _, c = buf.shape
    z = jnp.zeros((b, tq - 1, c), F32)
    h1 = jnp.concatenate([buf[:, 1:2], z], axis=1)
    h2 = jnp.concatenate([buf[:, 0:2], z[:, 1:]], axis=1)
    return h1.reshape(b * tq, c), h2.reshape(b * tq, c)


def _row_tile(rows, cap):
    tm = cap
    while rows % tm:
        tm //= 2
    return tm


def _per_seq_pages(xt, b, tq):
    c = xt.shape[0]
    x = xt.reshape(c, b, tq).transpose(1, 0, 2)
    return jnp.concatenate([x, jnp.zeros((b, c, PAGE_SIZE - tq), F32)], axis=-1)


def _prompt_layer(x2d, lw, tables, b, t):
    tm = _row_tile(t, 512)
    assert tm % MOBA_BLOCK == 0 and tm % SLC_TILE == 0
    tps = t // tm
    pw = dict(lw, mix=lw['mix_p'], mixb=lw['mixb_p'])
    (oab, qt, nsat, wint, mobat, gt, _, ztail, cmp_tm, slc_tm, win_tm, mk_tm, vt16, kmean) = _proj_call(
        x2d, pw, tables, None, tm=tm, tiles_per_seq=tps, n_seq=b, chunk=GMLP_CHUNK, seg=None)
    kcvc, kcvct = _compress_prompt_call(cmp_tm.reshape(b, t, LANES), lw)
    o_ct = _nsa_prompt_call(qt, gt, kcvc, kcvct, slc_tm.reshape(b, t, LANES), win_tm.reshape(b, t, LANES), vt16)
    nb = t // MOBA_BLOCK
    kmean3 = kmean[:, 0:tm // MOBA_BLOCK].reshape(b, nb, 256)
    nbp = -(-nb // SUBLANES) * SUBLANES
    if nbp != nb:
        kmean3 = jnp.concatenate([kmean3, jnp.zeros((b, nbp - nb, 256), F32)], axis=1)
    o_dt = _moba_prompt_call(qt, kmean3, mk_tm.reshape(b, t, 256), vt16)
    x_mid = _merge_call(x2d, oab, o_ct, o_dt, lw, tm=tm)
    y, ftail = _ffn_call(x_mid, lw, None, tm=tm, tiles_per_seq=tps, n_seq=b, seg=None)
    hd = HEAD_DIM
    wlen = min(NSA_WINDOW, t)
    state = (nsat.reshape(b, 4, hd, t).transpose(0, 3, 1, 2),
             mobat.reshape(b, 2, MOBA_HEADS, hd, t).transpose(0, 4, 1, 2, 3),
             wint[:, :, t - wlen:].reshape(b, 2, hd, wlen).transpose(0, 3, 1, 2),
             ztail[:, SUBLANES - 2:], ftail[:, SUBLANES - 2:])
    return y, state


def _sample_layer(x2d, lw, tables, b, tq, layer, nsa_cache5, moba_cache5, page_table, win_state5, conv_state,
                  ffn_state):
    rows = b * tq
    hd = HEAD_DIM
    pw = dict(lw, mix=lw['mix_s'], mixb=lw['mixb_s'])
    oab, qt, nsat, wint, mobat, _, gtm, ztail, vn = _proj_call(
        x2d, pw, tables, _halos(conv_state, tq), tm=rows, tiles_per_seq=1, n_seq=b, chunk=rows, seg=tq)
    nsat, wint, mobat = nsat[0], wint[0], mobat[0]
    kcvct = _compress_paged_call(nsa_cache5, layer, page_table, lw)
    q32 = qt[0:256].reshape(NSA_HEADS, hd, b, tq).transpose(2, 0, 3, 1).reshape(b, NSA_HEADS * tq, hd)
    g4 = gtm[:, 0:3 * NSA_HEADS].reshape(b, tq, NSA_HEADS, 3).transpose(0, 2, 1, 3).reshape(b, NSA_HEADS * tq, 3)
    g32 = jnp.concatenate([g4, jnp.zeros((b, NSA_HEADS * tq, LANES - 3), F32)], axis=-1)
    newt = _per_seq_pages(nsat[2 * hd:4 * hd], b, tq).reshape(b, 2, hd, PAGE_SIZE)
    wnewt = _per_seq_pages(wint, b, tq).reshape(b, 2, hd, PAGE_SIZE)
    o32 = _nsa_sample_call(nsa_cache5, layer, page_table, q32, g32, kcvct, newt, win_state5, wnewt)
    o_ct = o32.reshape(b, NSA_HEADS, tq, hd).transpose(1, 3, 0, 2).reshape(256, rows)
    qm = qt[256:512].reshape(MOBA_HEADS, hd, b, tq).transpose(2, 0, 3, 1)
    eye = jnp.eye(MOBA_HEADS, dtype=F32)
    qbd = (qm[:, :, :, None, :] * eye[None, :, None, :, None]).reshape(b, MOBA_HEADS * tq, 256)
    mnewt = _per_seq_pages(mobat, b, tq).reshape(b, 2, 256, PAGE_SIZE)
    o_dt = _moba_sample_call(moba_cache5, layer, page_table, qbd, mnewt).transpose(2, 0, 1).reshape(256, rows)
    x_mid = _merge_call(x2d, oab, o_ct, o_dt, lw, tm=rows)
    y, ftail = _ffn_call(x_mid, lw, _halos(ffn_state, tq), tm=rows, tiles_per_seq=1, n_seq=b, seg=tq)
    wnew = wint.reshape(2, hd, b, tq).transpose(2, 0, 1, 3)
    win_all = jnp.concatenate([win_state5[layer], wnew], axis=-1)
    win_new = win_all[..., win_all.shape[-1] - NSA_WINDOW:].transpose(0, 3, 1, 2)
    state = (nsat.T.reshape(b, tq, 4, hd), mobat.T.reshape(b, tq, 2, MOBA_HEADS, hd), win_new,
             ztail[:, tq - 2:tq], ftail[:, tq - 2:tq], vn.reshape(b, tq, 256))
    return y, state


def kernel(x_prompt, x_sample, cache_nsa_kv, cache_moba_kv, state_nsa_win, state_conv, state_ffn_conv, page_table, norm1, w_in, conv_w, conv_b, gmlp_ln_g, gmlp_ln_b, gmlp_ws, gmlp_bs, nsa_q_gain, nsa_k_gain, nsa_cmp_pos, nsa_cmp_w1, nsa_cmp_w2, moba_q_gain, moba_k_gain, w_branch, w_o, norm2, w_ffn_gate, w_ffn_up, ffn_conv_w, ffn_conv_b, w_ffn_down):
    params = dict(norm1=norm1, w_in=w_in, conv_w=conv_w, conv_b=conv_b, gmlp_ln_g=gmlp_ln_g, gmlp_ln_b=gmlp_ln_b,
                  gmlp_ws=gmlp_ws, gmlp_bs=gmlp_bs, nsa_q_gain=nsa_q_gain, nsa_k_gain=nsa_k_gain,
                  nsa_cmp_pos=nsa_cmp_pos, nsa_cmp_w1=nsa_cmp_w1, nsa_cmp_w2=nsa_cmp_w2, moba_q_gain=moba_q_gain,
                  moba_k_gain=moba_k_gain, w_branch=w_branch, w_o=w_o, norm2=norm2, w_ffn_gate=w_ffn_gate,
                  w_ffn_up=w_ffn_up, ffn_conv_w=ffn_conv_w, ffn_conv_b=ffn_conv_b, w_ffn_down=w_ffn_down)
    bp, tp, _ = x_prompt.shape
    bs, ts, _ = x_sample.shape
    depth = w_in.shape[0]
    n_pool = cache_nsa_kv.shape[1]
    past = page_table.shape[1] * PAGE_SIZE
    assert ts == SUBLANES and state_nsa_win.shape[2] == NSA_WINDOW
    nsa_cache5 = jnp.transpose(cache_nsa_kv, (0, 1, 3, 4, 2))
    moba_cache5 = jnp.transpose(cache_moba_kv, (0, 1, 3, 4, 5, 2)).reshape(depth, n_pool, 2, 256, PAGE_SIZE)
    win_state5 = jnp.transpose(state_nsa_win, (0, 1, 3, 4, 2))
    tab_p = _rope_tables_t(jnp.arange(tp, dtype=jnp.int32))
    tab_s = tuple(jnp.tile(a, (1, bs)) for a in _rope_tables_t(past + jnp.arange(ts, dtype=jnp.int32)))
    yp = x_prompt.reshape(bp * tp, D_MODEL)
    ys = x_sample.reshape(bs * ts, D_MODEL)
    sp, ss = [], []
    for l in range(depth):
        lw = _layer_weights(l, params, ts, bs)
        yp, st_p = _prompt_layer(yp, lw, tab_p, bp, tp)
        ys, st_s = _sample_layer(ys, lw, tab_s, bs, ts, l, nsa_cache5, moba_cache5, page_table, win_state5,
                                 state_conv[l], state_ffn_conv[l])
        sp.append(st_p)
        ss.append(st_s)
    stack = lambda lst, k: jnp.stack([s[k] for s in lst])
    return (yp.reshape(bp, tp, D_MODEL), ys.reshape(bs, ts, D_MODEL),
            stack(sp, 0), stack(ss, 0), stack(sp, 1), stack(ss, 1), stack(sp, 2), stack(ss, 2),
            stack(sp, 3), stack(ss, 3), stack(sp, 4), stack(ss, 4), stack(ss, 5))
```

```python
import functools

import jax
import jax.numpy as jnp
from jax import lax
from jax.experimental import pallas as pl
from jax.experimental.pallas import tpu as pltpu

F32 = jnp.float32
BF16 = jnp.bfloat16

D_MODEL = 1024
HEAD_DIM = 64
ROT_DIM = HEAD_DIM // 4
ROPE_THETA = 500000.0
PAGE_SIZE = 128
BRANCH_CH = 256
N_BRANCH = 4
GMLP_GROUPS = 4
GMLP_CHUNK = 128
NSA_HEADS = 4
NSA_CMP_LEN = 32
NSA_CMP_STRIDE = 16
NSA_CMP_HIDDEN = 256
NSA_SLC_BLOCK = 64
NSA_TOPN = 16
NSA_WINDOW = 512
MOBA_HEADS = 4
MOBA_BLOCK = 256
MOBA_TOPK = 3
D_FF = 2816
EPS = 1e-6
NEG = -1e30
BIG = 1e30

SLC_TILE = 512
LANES = 128
SUBLANES = 8
VMEM_LIMIT = 56 * 1024 * 1024

C_AH, C_AB, C_AC, C_U, C_V, C_G, C_END = 0, 256, 512, 768, 1024, 1280, 1408
R_Q, R_NSA, R_WIN, R_MK, R_MV, R_G, R_END = 0, 512, 768, 896, 1152, 1408, 1424
NORMED_GROUPS = (0, 1, 2, 3, 4, 5, 6, 7, 8, 10, 12, 14, 15, 16, 17)

_NT = (((1,), (1,)), ((), ()))


def _dot(a, b):
    return jnp.dot(a, b, preferred_element_type=F32)


def _dot_nt(a, b):
    return lax.dot_general(a, b, _NT, preferred_element_type=F32)


def _split(a):
    hi = a.astype(BF16)
    lo = (a - hi.astype(F32)).astype(BF16)
    return hi, lo


def _dot_3pass(a, b):
    ah, al = _split(a)
    bh, bl = _split(b)
    return _dot(ah, bh) + _dot(ah, bl) + _dot(al, bh)


def _iota(shape, dim):
    return lax.broadcasted_iota(jnp.int32, shape, dim)


def _lane_tile(a, n):
    return a if n == 1 else jnp.concatenate([a] * n, axis=1)


def _norm_rope_t(xh, gain, cos, sin):
    ss = jnp.sum(xh * xh, axis=0, keepdims=True)
    y = xh * lax.rsqrt(ss * (1.0 / HEAD_DIM) + EPS) * gain
    half = ROT_DIM // 2
    y0, y1 = y[0:half], y[half:ROT_DIM]
    return jnp.concatenate([y0 * cos - y1 * sin, y1 * cos + y0 * sin, y[ROT_DIM:]], axis=0)


def _shift_rows(z, prev1, prev2, seg):
    rows = _iota(z.shape, 0)
    z1 = pltpu.roll(z, 1, 0)
    z2 = pltpu.roll(z, 2, 0)
    if seg is None:
        z1 = jnp.where(rows == 0, prev1, z1)
        z2 = jnp.where(rows == 0, prev2, jnp.where(rows == 1, prev1, z2))
    else:
        t = rows % seg
        z1 = jnp.where(t == 0, prev1, z1)
        z2 = jnp.where(t <= 1, prev2, z2)
    return z1, z2


def _proj_kernel(*refs, tiles_per_seq, chunk, seg, subs):
    it = iter(refs)
    x_ref, n1_ref, wtm_ref, wtr_ref, cw_ref, cb_ref, lng_ref, lnb_ref, mix_ref, mixb_ref = (next(it) for _ in range(10))
    gain_ref, cos_ref, sin_ref = (next(it) for _ in range(3))
    if seg is not None:
        h1_ref, h2_ref = next(it), next(it)
    oab_ref, qt_ref, nsat_ref, wint_ref, mobat_ref, gt_ref, gtm_ref, ztail_ref = (next(it) for _ in range(8))
    if seg is not None:
        vn_ref = next(it)
    else:
        cmp_ref, slc_ref, win_ref, mk_ref, vt_ref, kmean_ref, zprev_ref = (next(it) for _ in range(7))

    tm = x_ref.shape[0]
    sm = tm // subs
    if seg is None:
        @pl.when(pl.program_id(0) % tiles_per_seq == 0)
        def _():
            zprev_ref[...] = jnp.zeros_like(zprev_ref)

        prev = zprev_ref[...]
        kmean_ref[...] = jnp.zeros_like(kmean_ref)

    for si in range(subs):
        r0 = si * sm
        rs = slice(r0, r0 + sm)
        x = x_ref[rs, :]
        xn = (x * lax.rsqrt(jnp.mean(x * x, axis=-1, keepdims=True) + EPS) * n1_ref[...]).astype(BF16)
        proj = _dot_nt(xn, wtm_ref[...])
        projt = _dot_nt(wtr_ref[...], xn)

        z = proj[:, C_AC:C_AC + 256] * proj[:, C_AH:C_AH + 256]
        if seg is None:
            z1, z2 = _shift_rows(z, prev[SUBLANES - 1:SUBLANES, :], prev[SUBLANES - 2:SUBLANES - 1, :], None)
            prev = z[sm - SUBLANES:, :]
        else:
            z1, z2 = _shift_rows(z, h1_ref[...], h2_ref[...], seg)
            ztail_ref[...] = z.reshape(ztail_ref.shape)
        ya = z2 * cw_ref[0:1, :] + cb_ref[...] + z1 * cw_ref[1:2, :] + z * cw_ref[2:3, :]
        oab_ref[rs, 0:256] = (proj[:, C_AB:C_AB + 256] * ya).astype(BF16)

        u = jax.nn.gelu(proj[:, C_U:C_U + 256])
        v = jax.nn.gelu(proj[:, C_V:C_V + 256])
        vc = v - jnp.mean(v, axis=-1, keepdims=True)
        vn = vc * lax.rsqrt(jnp.mean(vc * vc, axis=-1, keepdims=True) + EPS) * lng_ref[...] + lnb_ref[...]
        if seg is not None:
            vn_ref[...] = vn
        vnb = vn.astype(BF16)
        grp = _iota((chunk, 256), 1) // HEAD_DIM
        for c in range(sm // chunk):
            r = _dot(mix_ref[...], vnb[c * chunk:(c + 1) * chunk, :])
            s = r[0:chunk]
            for g in range(1, GMLP_GROUPS):
                s = jnp.where(grp == g, r[g * chunk:(g + 1) * chunk], s)
            s = s + mixb_ref[...]
            oab_ref[r0 + c * chunk:r0 + (c + 1) * chunk, 256:512] = (u[c * chunk:(c + 1) * chunk, :] * s).astype(BF16)

        gtm_ref[rs, :] = jax.nn.sigmoid(proj[:, C_G:C_END])
        gt_ref[:, rs] = jax.nn.sigmoid(projt[R_G:R_END, :])

        cos, sin = cos_ref[:, rs], sin_ref[:, rs]
        rep = sm // LANES
        groups = []
        for g in range(R_MV // HEAD_DIM):
            xh = projt[g * HEAD_DIM:(g + 1) * HEAD_DIM, :]
            if g in NORMED_GROUPS:
                xh = _norm_rope_t(xh, _lane_tile(gain_ref[g * HEAD_DIM:(g + 1) * HEAD_DIM, :], rep), cos, sin)
            groups.append(xh)
        qkv = jnp.concatenate(groups, axis=0)
        qt_ref[:, rs] = qkv[R_Q:R_NSA]
        nsat_ref[0, :, rs] = qkv[R_NSA:R_WIN]
        wint_ref[0, :, rs] = qkv[R_WIN:R_MK]
        mobat_ref[0, 0:256, rs] = qkv[R_MK:R_MV]
        vmt = projt[R_MV:R_G, :]
        mobat_ref[0, 256:512, rs] = vmt

        if seg is None:
            nsa_tm = qkv[R_NSA:R_WIN].T
            cmp_ref[rs, :] = nsa_tm[:, 0:LANES]
            lane = _iota((sm, LANES), 1) - HEAD_DIM
            blk_in_tile = ((r0 + _iota((sm, LANES), 0)) // NSA_SLC_BLOCK) % (SLC_TILE // NSA_SLC_BLOCK)
            slc_ref[rs, :] = jnp.where(lane < 0, nsa_tm[:, LANES:2 * LANES],
                                       jnp.where(lane == blk_in_tile, 1.0, 0.0)).astype(BF16)
            win_ref[rs, :] = qkv[R_WIN:R_MK].T.astype(BF16)
            mk_tm = qkv[R_MK:R_MV].T
            mk_ref[rs, :] = mk_tm.astype(BF16)
            vt_ref[0, 0:256, rs] = vmt.astype(BF16)
            vt_ref[0, 256:320, rs] = qkv[R_NSA + 192:R_NSA + 256].astype(BF16)
            vt_ref[0, 320:384, rs] = qkv[R_WIN + 64:R_WIN + 128].astype(BF16)
            for r in range(sm // MOBA_BLOCK):
                b0 = r0 // MOBA_BLOCK + r
                kmean_ref[0, b0:b0 + 1, :] = jnp.sum(mk_tm[r * MOBA_BLOCK:(r + 1) * MOBA_BLOCK, :], axis=0,
                                                     keepdims=True) * (1.0 / MOBA_BLOCK)

    if seg is None:
        zprev_ref[...] = prev
        ztail_ref[0] = prev


def _proj_call(x2d, lw, tables, halos, *, tm, tiles_per_seq, n_seq, chunk, seg):
    rows = x2d.shape[0]
    nt = rows // tm
    tps = tiles_per_seq
    n_out_seq = n_seq if seg is None else 1
    t_out = rows // n_out_seq
    row = lambda w: pl.BlockSpec((tm, w), lambda i: (i, 0))
    col = lambda r: pl.BlockSpec((r, tm), lambda i: (0, i))
    seq3 = lambda r: pl.BlockSpec((1, r, tm), lambda i: (i // tps, 0, i % tps))
    full = lambda a: pl.BlockSpec(a.shape, lambda i: (0,) * a.ndim)
    n_tab = tables[0].shape[1] // tm
    tab = pl.BlockSpec((ROT_DIM // 2, tm), lambda i: (0, i % n_tab))
    ws = [lw['norm1'], lw['w_tm'], lw['w_tr'], lw['conv_w'], lw['conv_b'], lw['ln_g'], lw['ln_b'], lw['mix'],
          lw['mixb'], lw['gain_t']]
    ins = [x2d] + ws + list(tables)
    in_specs = [row(D_MODEL)] + [full(a) for a in ws] + [tab] * 2
    sds = jax.ShapeDtypeStruct
    out_shape = [sds((rows, 512), BF16), sds((512, rows), F32), sds((n_out_seq, 256, t_out), F32),
                 sds((n_out_seq, 128, t_out), F32), sds((n_out_seq, 512, t_out), F32),
                 sds((2 * SUBLANES, rows), F32), sds((rows, LANES), F32), sds((n_seq, SUBLANES, 256), F32)]
    out_specs = [row(512), col(512), seq3(256), seq3(128), seq3(512), col(2 * SUBLANES), row(LANES)]
    scratch = []
    if seg is None:
        out_specs.append(pl.BlockSpec((1, SUBLANES, 256), lambda i: (i // tps, 0, 0)))
        out_shape += [sds((rows, LANES), F32), sds((rows, LANES), BF16), sds((rows, LANES), BF16),
                      sds((rows, 256), BF16), sds((n_seq, 384, t_out), BF16), sds((nt, SUBLANES, 256), F32)]
        out_specs += [row(LANES), row(LANES), row(LANES), row(256), seq3(384),
                      pl.BlockSpec((1, SUBLANES, 256), lambda i: (i, 0, 0))]
        scratch.append(pltpu.VMEM((SUBLANES, 256), F32))
    else:
        ins += list(halos)
        in_specs += [row(256), row(256)]
        out_specs.append(pl.BlockSpec((n_seq, SUBLANES, 256), lambda i: (0, 0, 0)))
        out_shape.append(sds((rows, 256), F32))
        out_specs.append(row(256))
    return pl.pallas_call(
        functools.partial(_proj_kernel, tiles_per_seq=tps, chunk=chunk, seg=seg,
                          subs=tm // MOBA_BLOCK if seg is None else 1),
        grid=(nt,), in_specs=in_specs, out_specs=out_specs, out_shape=out_shape, scratch_shapes=scratch,
        compiler_params=pltpu.CompilerParams(dimension_semantics=("arbitrary",), vmem_limit_bytes=VMEM_LIMIT),
        name="proj_prompt" if seg is None else "proj_sample",
    )(*ins)


def _compress_compute(src_ref, wa_ref, wb_ref, pt_ref, pb_ref, w2_ref, groups):
    acc_a = jnp.zeros((groups, 2 * NSA_CMP_HIDDEN), F32)
    acc_b = jnp.zeros((groups, 2 * NSA_CMP_HIDDEN), F32)
    for p in range(NSA_CMP_STRIDE // 2):
        xp = jnp.concatenate([src_ref[pl.ds(2 * p, groups, stride=NSA_CMP_STRIDE), :],
                              src_ref[pl.ds(2 * p + 1, groups, stride=NSA_CMP_STRIDE), :]], axis=1)
        acc_a = acc_a + _dot((xp + pt_ref[p:p + 1, :]).astype(BF16), wa_ref[p])
        acc_b = acc_b + _dot((xp + pb_ref[p:p + 1, :]).astype(BF16), wb_ref[p])
    hdn = jax.nn.gelu(acc_a + pltpu.roll(acc_b, groups - 1, 0))
    return _dot(hdn.astype(BF16), w2_ref[...])


def _compress_prompt_kernel(src_ref, wa_ref, wb_ref, pt_ref, pb_ref, w2_ref, out_ref, outt_ref, *, groups):
    out = _compress_compute(src_ref.at[0], wa_ref, wb_ref, pt_ref, pb_ref, w2_ref, groups)
    out_ref[0] = out
    outt_ref[0] = out.T


def _compress_paged_kernel(ptab_ref, *refs, groups, n_seq):
    del ptab_ref
    n_in = len(refs) - 7
    pages = refs[:n_in]
    wa_ref, wb_ref, pt_ref, pb_ref, w2_ref, outt_ref, stage_ref = refs[n_in:]
    n_pages = n_in // n_seq
    for s in range(n_seq):
        for k in range(n_pages):
            pg = pages[s * n_pages + k][0, 0].reshape(2 * HEAD_DIM, PAGE_SIZE)
            stage_ref[s, k * PAGE_SIZE:(k + 1) * PAGE_SIZE, :] = pg.T
    for s in range(n_seq):
        outt_ref[s] = _compress_compute(stage_ref.at[s], wa_ref, wb_ref, pt_ref, pb_ref, w2_ref, groups).T


def _compress_prompt_call(cmp_tm3, lw):
    b, t, _ = cmp_tm3.shape
    groups = t // NSA_CMP_STRIDE
    full = lambda a: pl.BlockSpec(a.shape, lambda i: (0,) * a.ndim)
    ws = [lw['cmp_wa'], lw['cmp_wb'], lw['cmp_pt'], lw['cmp_pb'], lw['cmp_w2']]
    return pl.pallas_call(
        functools.partial(_compress_prompt_kernel, groups=groups),
        grid=(b,),
        in_specs=[pl.BlockSpec((1, t, LANES), lambda i: (i, 0, 0))] + [full(a) for a in ws],
        out_specs=[pl.BlockSpec((1, groups, LANES), lambda i: (i, 0, 0)),
                   pl.BlockSpec((1, LANES, groups), lambda i: (i, 0, 0))],
        out_shape=[jax.ShapeDtypeStruct((b, groups, LANES), F32), jax.ShapeDtypeStruct((b, LANES, groups), F32)],
        compiler_params=pltpu.CompilerParams(dimension_semantics=("arbitrary",), vmem_limit_bytes=VMEM_LIMIT),
        name="compress_prompt",
    )(cmp_tm3, *ws)


def _compress_paged_call(cache5, layer, page_table, lw):
    b, n_pages = page_table.shape
    past = n_pages * PAGE_SIZE
    groups = past // NSA_CMP_STRIDE
    n_seq = 2 if b % 2 == 0 else 1
    full = lambda a: pl.BlockSpec(a.shape, lambda i, pt: (0,) * a.ndim)
    ws = [lw['cmp_wa'], lw['cmp_wb'], lw['cmp_pt'], lw['cmp_pb'], lw['cmp_w2']]
    page_specs = [pl.BlockSpec((1, 1, 2, HEAD_DIM, PAGE_SIZE), functools.partial(
        lambda i, pt, s, k: (layer, pt[i * n_seq + s, k], 0, 0, 0), s=s, k=k))
        for s in range(n_seq) for k in range(n_pages)]
    grid_spec = pltpu.PrefetchScalarGridSpec(
        num_scalar_prefetch=1, grid=(b // n_seq,),
        in_specs=page_specs + [full(a) for a in ws],
        out_specs=pl.BlockSpec((n_seq, LANES, groups), lambda i, pt: (i, 0, 0)),
        scratch_shapes=[pltpu.VMEM((n_seq, past, LANES), F32)])
    return pl.pallas_call(
        functools.partial(_compress_paged_kernel, groups=groups, n_seq=n_seq),
        grid_spec=grid_spec, out_shape=jax.ShapeDtypeStruct((b, LANES, groups), F32),
        compiler_params=pltpu.CompilerParams(dimension_semantics=("arbitrary",), vmem_limit_bytes=VMEM_LIMIT),
        name="compress_sample",
    )(page_table, *([cache5] * (n_seq * n_pages)), *ws)


def _topk_mask(score, n_cols, k):
    idx = _iota(score.shape, 1)
    rank = jnp.zeros(score.shape, jnp.int32)
    for j in range(n_cols):
        col = score[:, j:j + 1]
        rank = rank + jnp.where(col > score, 1, jnp.where(col == score, jnp.where(idx > j, 1, 0), 0))
    return rank < k


def _topk_mask_t(score, n_rows, k):
    n_tiles = score.shape[0] // SUBLANES
    tiles = [score[r * SUBLANES:(r + 1) * SUBLANES, :] for r in range(n_tiles)]
    idx = _iota(tiles[0].shape, 0)
    ranks = [jnp.zeros(tiles[0].shape, jnp.int32) for _ in range(n_tiles)]
    for j in range(n_rows):
        row = score[j:j + 1, :]
        for r in range(n_tiles):
            if r * SUBLANES > j:
                ranks[r] = jnp.where(row >= tiles[r], ranks[r] + 1, ranks[r])
            elif (r + 1) * SUBLANES - 1 <= j:
                ranks[r] = jnp.where(row > tiles[r], ranks[r] + 1, ranks[r])
            else:
                ranks[r] = ranks[r] + jnp.where(idx + r * SUBLANES > j, jnp.where(row >= tiles[r], 1, 0),
                                                jnp.where(row > tiles[r], 1, 0))
    return jnp.concatenate(ranks, axis=0) < k


def _softmax_axis(s, ok, axis):
    s = jnp.where(ok, s, NEG)
    m = jnp.max(s, axis=axis, keepdims=True)
    p = jnp.where(ok, jnp.exp(s - m), 0.0)
    l = jnp.sum(p, axis=axis, keepdims=True)
    return p / jnp.where(l > 0.0, l, 1.0)


def _forced_importance(imp, blk, cur):
    forced = (blk == 0) | (blk == cur) | (blk == cur - 1)
    imp = jnp.where(forced, BIG, imp)
    return jnp.where(blk > cur, NEG, imp)


def _nsa_prompt_kernel(qt_ref, gt_ref, kcvc_ref, kcvct_ref, slc_ref, win_ref, vt_ref, o_ref, sel_ref, s_ref, *, t, qb, tk):
    iq = pl.program_id(1)
    s0 = iq * qb
    nq = NSA_HEADS * qb
    qt = qt_ref[...] * (HEAD_DIM ** -0.5)
    qst = jnp.concatenate([qt[h * HEAD_DIM:(h + 1) * HEAD_DIM, :] for h in range(NSA_HEADS)], axis=1)
    qst = jnp.concatenate([qst, jnp.zeros_like(qst)], axis=0).astype(BF16)
    qpos_q = s0 + _iota((1, qb), 1)
    qpos = _lane_tile(qpos_q, NSA_HEADS)

    wlen = NSA_WINDOW + qb
    start = pl.multiple_of(jnp.maximum(s0 - NSA_WINDOW, 0), qb)
    sw = _dot(win_ref[0, pl.ds(start, wlen), :], qst)
    back = (qpos - start) - _iota(sw.shape, 0)
    sw = jnp.where((back & -NSA_WINDOW) == 0, sw, NEG)
    ew = jnp.exp(sw - jnp.max(sw, axis=0, keepdims=True))
    o_win = (_dot(vt_ref[0, HEAD_DIM:2 * HEAD_DIM, pl.ds(start, wlen)], ew.astype(BF16))
             / jnp.sum(ew, axis=0, keepdims=True))

    n_cmp = (t - NSA_CMP_LEN) // NSA_CMP_STRIDE + 1
    n_blk = t // NSA_SLC_BLOCK
    kcvc = kcvc_ref[0].astype(BF16)
    g = kcvc.shape[0]
    s = _dot(kcvc, qst)
    n_idx = _iota(s.shape, 0)
    ok = (n_idx * NSA_CMP_STRIDE + (NSA_CMP_LEN - 1) <= qpos) & (n_idx < n_cmp)
    s = jnp.where(ok, s, NEG)
    m = jnp.max(s, axis=0, keepdims=True)
    e = jnp.exp(s - m)
    p = e * jnp.where(m > 0.5 * NEG, 1.0 / jnp.sum(e, axis=0, keepdims=True), 0.0)
    o_cmp = _dot(kcvct_ref[0, HEAD_DIM:2 * HEAD_DIM, :].astype(BF16), p.astype(BF16))
    psum = p[:, 0:qb]
    for h in range(1, NSA_HEADS):
        psum = psum + p[:, h * qb:(h + 1) * qb]
    cj = _iota((n_blk, g), 0)
    cn = _iota((n_blk, g), 1)
    cover = ((cn * NSA_CMP_STRIDE <= cj * NSA_SLC_BLOCK + (NSA_SLC_BLOCK - 1))
             & (cn * NSA_CMP_STRIDE + (NSA_CMP_LEN - 1) >= cj * NSA_SLC_BLOCK) & (cn < n_cmp)).astype(BF16)
    p_hi, p_lo = _split(psum)
    imp = _dot(cover, p_hi) + _dot(cover, p_lo)
    imp = _forced_importance(imp, _iota(imp.shape, 0), qpos_q // NSA_SLC_BLOCK)
    cur = qpos_q // NSA_SLC_BLOCK
    picked = _topk_mask_t(imp, n_blk, NSA_TOPN) & (_iota(imp.shape, 0) <= cur)
    sel_ref[...] = _lane_tile(jnp.where(picked, 0.0, NEG), NSA_HEADS)

    bpt = tk // NSA_SLC_BLOCK
    q64 = qst[0:HEAD_DIM].astype(F32)
    zpad = jnp.zeros((LANES - HEAD_DIM - bpt, nq), F32)
    kd = s0 // tk

    def scores(kt):
        k0 = pl.multiple_of(kt * tk, tk)
        slab = sel_ref[pl.ds(pl.multiple_of(kt * bpt, bpt), bpt), :]
        w = jnp.concatenate([q64, slab, zpad], axis=0).astype(BF16)
        return _dot(slc_ref[0, pl.ds(k0, tk), :], w)

    def fold(x, op):
        return op(x.reshape(tk // SUBLANES, SUBLANES, nq), axis=0)

    def sweep1(kt, m8):
        sc = scores(kt)
        s_ref[kt] = sc
        return jnp.maximum(m8, fold(sc, jnp.max))

    m8 = lax.fori_loop(0, kd, sweep1, jnp.full((SUBLANES, nq), NEG, F32))
    sc = scores(kd)
    sc = jnp.where(kd * tk + _iota(sc.shape, 0) <= qpos, sc, NEG)
    s_ref[kd] = sc
    m = jnp.max(jnp.maximum(m8, fold(sc, jnp.max)), axis=0, keepdims=True)

    def sweep2(kt, carry):
        l8, acc = carry
        k0 = pl.multiple_of(kt * tk, tk)
        pp = jnp.exp(s_ref[kt] - m)
        return l8 + fold(pp, jnp.sum), acc + _dot(vt_ref[0, 0:HEAD_DIM, pl.ds(k0, tk)], pp.astype(BF16))

    l8, acc = lax.fori_loop(0, kd + 1, sweep2, (jnp.zeros((SUBLANES, nq), F32), jnp.zeros((HEAD_DIM, nq), F32)))
    o_slc = acc / jnp.sum(l8, axis=0, keepdims=True)

    gt = gt_ref[...]
    outs = []
    for h in range(NSA_HEADS):
        c = slice(h * qb, (h + 1) * qb)
        outs.append(gt[3 * h:3 * h + 1, :] * o_cmp[:, c] + gt[3 * h + 1:3 * h + 2, :] * o_slc[:, c]
                    + gt[3 * h + 2:3 * h + 3, :] * o_win[:, c])
    o_ref[...] = jnp.concatenate(outs, axis=0)


def _nsa_prompt_call(qt, gt, kcvc, kcvct, slc_tm3, win_tm3, vt3, *, qb=256, tk=SLC_TILE):
    b, t, _ = slc_tm3.shape
    nq = t // qb
    g = kcvc.shape[1]
    assert t % tk == 0 and tk % qb == 0 and t >= NSA_WINDOW + qb
    return pl.pallas_call(
        functools.partial(_nsa_prompt_kernel, t=t, qb=qb, tk=tk),
        grid=(b, nq),
        in_specs=[pl.BlockSpec((256, qb), lambda i, j: (0, i * nq + j)),
                  pl.BlockSpec((2 * SUBLANES, qb), lambda i, j: (0, i * nq + j)),
                  pl.BlockSpec((1, g, LANES), lambda i, j: (i, 0, 0)),
                  pl.BlockSpec((1, LANES, g), lambda i, j: (i, 0, 0)),
                  pl.BlockSpec((1, t, LANES), lambda i, j: (i, 0, 0)),
                  pl.BlockSpec((1, t, LANES), lambda i, j: (i, 0, 0)),
                  pl.BlockSpec((1, LANES, t), lambda i, j: (i, 2, 0))],
        out_specs=pl.BlockSpec((256, qb), lambda i, j: (0, i * nq + j)),
        out_shape=jax.ShapeDtypeStruct((256, b * t), F32),
        scratch_shapes=[pltpu.VMEM((t // NSA_SLC_BLOCK, NSA_HEADS * qb), F32),
                        pltpu.VMEM((t // tk, tk, NSA_HEADS * qb), F32)],
        compiler_params=pltpu.CompilerParams(dimension_semantics=("arbitrary", "arbitrary"),
                                             vmem_limit_bytes=VMEM_LIMIT),
        name="nsa_prompt",
    )(qt, gt, kcvc, kcvct, slc_tm3, win_tm3, vt3)


def _nsa_sample_kernel(ptab_ref, *refs, past, tq, n_seq):
    del ptab_ref
    n_in = len(refs) - 10
    n_pages = n_in // n_seq
    pages = refs[:n_in]
    q_ref, g_ref, kcvct_ref, new_ref, wst_ref, wnew_ref, o_ref, kt_ref, vt_ref, e_ref = refs[n_in:]

    @pl.when(pl.program_id(0) == 0)
    def _():
        e_ref[...] = (_iota(e_ref.shape, 0) == _iota(e_ref.shape, 1) // NSA_SLC_BLOCK).astype(BF16)

    for sq in range(n_seq):
        for k in range(n_pages):
            pg = pages[sq * n_pages + k][0, 0]
            kt_ref[sq, :, k * PAGE_SIZE:(k + 1) * PAGE_SIZE] = pg[0].astype(BF16)
            vt_ref[sq, :, k * PAGE_SIZE:(k + 1) * PAGE_SIZE] = pg[1].astype(BF16)

    for sq in range(n_seq):
        rows = NSA_HEADS * tq
        qs = (q_ref[sq] * (HEAD_DIM ** -0.5)).astype(BF16)
        qpos_q = past + _iota((tq, 1), 0)
        qpos = jnp.concatenate([qpos_q] * NSA_HEADS, axis=0)
        l_all = past + tq
        n_cmp = (l_all - NSA_CMP_LEN) // NSA_CMP_STRIDE + 1
        n_blk = -(-l_all // NSA_SLC_BLOCK)
        nb_pad = -(-n_blk // LANES) * LANES

        kcvct = kcvct_ref[sq].astype(BF16)
        g = kcvct.shape[1]
        s = _dot(qs, kcvct[0:HEAD_DIM])
        n_idx = _iota(s.shape, 1)
        ok = (n_idx * NSA_CMP_STRIDE + (NSA_CMP_LEN - 1) <= qpos) & (n_idx < n_cmp)
        p = _softmax_axis(s, ok, 1)
        o_cmp = _dot_nt(p.astype(BF16), kcvct[HEAD_DIM:2 * HEAD_DIM])
        psum = p[0:tq]
        for h in range(1, NSA_HEADS):
            psum = psum + p[h * tq:(h + 1) * tq]
        cn = _iota((g, nb_pad), 0)
        cj = _iota((g, nb_pad), 1)
        cover = ((cn * NSA_CMP_STRIDE <= cj * NSA_SLC_BLOCK + (NSA_SLC_BLOCK - 1))
                 & (cn * NSA_CMP_STRIDE + (NSA_CMP_LEN - 1) >= cj * NSA_SLC_BLOCK) & (cn < n_cmp)).astype(BF16)
        p_hi, p_lo = _split(psum)
        imp = _dot(p_hi, cover) + _dot(p_lo, cover)
        imp = _forced_importance(imp, _iota(imp.shape, 1), qpos_q // NSA_SLC_BLOCK)
        picked = _topk_mask(imp, n_blk, NSA_TOPN)

        nbe = e_ref.shape[0]
        bias = jnp.where(picked[:, 0:nbe], 0.0, NEG).astype(BF16)
        bias = jnp.concatenate([bias] * NSA_HEADS, axis=0)
        sc = _dot(qs, kt_ref[sq]) + _dot(bias, e_ref[...])
        new_ok = past + _iota((rows, PAGE_SIZE), 1) <= qpos
        sc_new = jnp.where(new_ok, _dot(qs, new_ref[sq, 0].astype(BF16)), NEG)
        m = jnp.maximum(jnp.max(sc, axis=-1, keepdims=True), jnp.max(sc_new, axis=-1, keepdims=True))
        pp = jnp.exp(sc - m)
        pp_new = jnp.exp(sc_new - m)
        l = jnp.sum(pp, axis=-1, keepdims=True) + jnp.sum(pp_new, axis=-1, keepdims=True)
        o_slc = (_dot_nt(pp.astype(BF16), vt_ref[sq])
                 + _dot_nt(pp_new.astype(BF16), new_ref[sq, 1].astype(BF16))) / l

        kw = jnp.concatenate([wst_ref[0, sq, 0], wnew_ref[sq, 0]], axis=1).astype(BF16)
        vw = jnp.concatenate([wst_ref[0, sq, 1], wnew_ref[sq, 1]], axis=1).astype(BF16)
        sw = _dot(qs, kw)
        kpos = past - NSA_WINDOW + _iota(sw.shape, 1)
        okw = (kpos <= qpos) & (kpos > qpos - NSA_WINDOW) & (kpos >= 0)
        o_win = _dot_nt(_softmax_axis(sw, okw, 1).astype(BF16), vw)

        gg = g_ref[sq]
        o_ref[sq] = gg[:, 0:1] * o_cmp + gg[:, 1:2] * o_slc + gg[:, 2:3] * o_win


def _nsa_sample_call(cache5, layer, page_table, q32, g32, kcvct, newt, wstate5, wnewt):
    b, n_pages = page_table.shape
    past = n_pages * PAGE_SIZE
    tq = q32.shape[1] // NSA_HEADS
    n_seq = 2 if b % 2 == 0 else 1
    nbe = -(-(past // NSA_SLC_BLOCK) // LANES) * LANES
    assert tq <= NSA_SLC_BLOCK and past >= NSA_WINDOW
    assert (past + tq - NSA_CMP_LEN) // NSA_CMP_STRIDE + 1 <= past // NSA_CMP_STRIDE - 1
    rows = NSA_HEADS * tq
    page_specs = [pl.BlockSpec((1, 1, 2, HEAD_DIM, PAGE_SIZE), functools.partial(
        lambda i, pt, s, k: (layer, pt[i * n_seq + s, k], 1, 0, 0), s=s, k=k))
        for s in range(n_seq) for k in range(n_pages)]
    per_b = lambda a: pl.BlockSpec((n_seq,) + a.shape[1:], lambda i, pt: (i,) + (0,) * (a.ndim - 1))
    grid_spec = pltpu.PrefetchScalarGridSpec(
        num_scalar_prefetch=1, grid=(b // n_seq,),
        in_specs=page_specs + [per_b(q32), per_b(g32), per_b(kcvct), per_b(newt),
                               pl.BlockSpec((1, n_seq, 2, HEAD_DIM, NSA_WINDOW), lambda i, pt: (layer, i, 0, 0, 0)),
                               per_b(wnewt)],
        out_specs=pl.BlockSpec((n_seq, rows, HEAD_DIM), lambda i, pt: (i, 0, 0)),
        scratch_shapes=[pltpu.VMEM((n_seq, HEAD_DIM, past), BF16), pltpu.VMEM((n_seq, HEAD_DIM, past), BF16),
                        pltpu.VMEM((nbe, past), BF16)])
    return pl.pallas_call(
        functools.partial(_nsa_sample_kernel, past=past, tq=tq, n_seq=n_seq),
        grid_spec=grid_spec, out_shape=jax.ShapeDtypeStruct((b, rows, HEAD_DIM), F32),
        compiler_params=pltpu.CompilerParams(dimension_semantics=("arbitrary",), vmem_limit_bytes=VMEM_LIMIT),
        name="nsa_sample",
    )(page_table, *([cache5] * (n_seq * n_pages)), q32, g32, kcvct, newt, wstate5, wnewt)


def _moba_prompt_kernel(qt_ref, kmean_ref, k_ref, vt_ref, o_ref, sel_ref, acc_ref, s_ref, *, nb):
    i = pl.program_id(1)
    r0 = pl.multiple_of(i * MOBA_BLOCK, MOBA_BLOCK)
    qb = MOBA_BLOCK
    nq = MOBA_HEADS * qb
    qt = qt_ref[...]
    rowgrp = _iota(qt.shape, 0) // HEAD_DIM
    kmean = kmean_ref[0]
    qpad = []
    for h in range(MOBA_HEADS):
        qh = jnp.where(rowgrp == h, qt, 0.0)
        gs = _dot_3pass(kmean, qh)
        n_idx = _iota(gs.shape, 0)
        gs = jnp.where(n_idx < i, gs, NEG)
        sel_ref[:, h * qb:(h + 1) * qb] = jnp.where(_topk_mask_t(gs, nb, MOBA_TOPK) & (n_idx < i), 0.0, NEG)
        qpad.append((qh * (HEAD_DIM ** -0.5)).astype(BF16))
    qcat = jnp.concatenate(qpad, axis=1)

    def block(n, width=1):
        k0 = pl.multiple_of(n * MOBA_BLOCK, width * MOBA_BLOCK)
        return k_ref[0, pl.ds(k0, width * MOBA_BLOCK), :], vt_ref[0, :, pl.ds(k0, width * MOBA_BLOCK)]

    def fold(x, op):
        return op(x.reshape(x.shape[0] // SUBLANES, SUBLANES, nq), axis=0)

    n_pairs = (i + 1) // 2

    def sweep1(n2, m8):
        bias = jnp.stack([sel_ref[pl.ds(2 * n2, 1), :], sel_ref[pl.ds(2 * n2 + 1, 1), :]])
        sc = (_dot(block(2 * n2, 2)[0], qcat).reshape(2, MOBA_BLOCK, nq) + bias).reshape(2 * MOBA_BLOCK, nq)
        s_ref[n2] = sc
        return jnp.maximum(m8, fold(sc, jnp.max))

    m8 = lax.fori_loop(0, n_pairs, sweep1, jnp.full((SUBLANES, nq), NEG, F32))
    own_k, own_v = block(i)
    causal = _iota((MOBA_BLOCK, nq), 0) <= _iota((MOBA_BLOCK, nq), 1) % qb
    own_s = jnp.where(causal, _dot(own_k, qcat), NEG)
    m = jnp.max(jnp.maximum(m8, fold(own_s, jnp.max)), axis=0, keepdims=True)

    acc_ref[...] = jnp.zeros_like(acc_ref)

    def accumulate(p, vb):
        pb = p.astype(BF16)
        for h in range(MOBA_HEADS):
            acc_ref[h] += _dot(vb[h * HEAD_DIM:(h + 1) * HEAD_DIM, :], pb[:, h * qb:(h + 1) * qb])
        return fold(p, jnp.sum)

    def sweep2(n2, l8):
        return l8 + accumulate(jnp.exp(s_ref[n2] - m), block(2 * n2, 2)[1])

    l8 = lax.fori_loop(0, n_pairs, sweep2, jnp.zeros((SUBLANES, nq), F32))
    l = jnp.sum(l8 + accumulate(jnp.exp(own_s - m), own_v), axis=0, keepdims=True)
    o_ref[...] = jnp.concatenate([acc_ref[h] / l[:, h * qb:(h + 1) * qb] for h in range(MOBA_HEADS)], axis=0)


def _moba_prompt_call(qt, kmean3, mk_tm3, vt3):
    b, t, _ = mk_tm3.shape
    nb = t // MOBA_BLOCK
    nbp = kmean3.shape[1]
    return pl.pallas_call(
        functools.partial(_moba_prompt_kernel, nb=nb),
        grid=(b, nb),
        in_specs=[pl.BlockSpec((256, MOBA_BLOCK), lambda i, j: (1, i * nb + j)),
                  pl.BlockSpec((1, nbp, 256), lambda i, j: (i, 0, 0)),
                  pl.BlockSpec((1, t, 256), lambda i, j: (i, 0, 0)),
                  pl.BlockSpec((1, 256, t), lambda i, j: (i, 0, 0))],
        out_specs=pl.BlockSpec((256, MOBA_BLOCK), lambda i, j: (0, i * nb + j)),
        out_shape=jax.ShapeDtypeStruct((256, b * t), F32),
        scratch_shapes=[pltpu.VMEM((nbp, MOBA_HEADS * MOBA_BLOCK), F32),
                        pltpu.VMEM((MOBA_HEADS, HEAD_DIM, MOBA_BLOCK), F32),
                        pltpu.VMEM((nb // 2, 2 * MOBA_BLOCK, MOBA_HEADS * MOBA_BLOCK), F32)],
        compiler_params=pltpu.CompilerParams(dimension_semantics=("arbitrary", "arbitrary"),
                                             vmem_limit_bytes=VMEM_LIMIT),
        name="moba_prompt",
    )(qt, kmean3, mk_tm3, vt3)


def _moba_sample_kernel(ptab_ref, *refs, past, tq, n_pages):
    del ptab_ref
    pages = refs[:n_pages]
    q_ref, new_ref, o_ref, s_ref, e_ref = refs[n_pages:]
    ppb = MOBA_BLOCK // PAGE_SIZE
    n_past = past // MOBA_BLOCK

    @pl.when(pl.program_id(0) == 0)
    def _():
        e_ref[...] = (_iota(e_ref.shape, 0) == _iota(e_ref.shape, 1) // MOBA_BLOCK).astype(BF16)

    q = q_ref[0]
    rows = q.shape[0]
    qb = (q * (HEAD_DIM ** -0.5)).astype(BF16)

    def block(n, which):
        return jnp.concatenate([pages[n * ppb + c][0, 0, which] for c in range(ppb)], axis=1)

    lane = _iota((256, LANES), 1)
    kmean = jnp.zeros((256, LANES), F32)
    for n in range(n_past):
        kb = block(n, 0)
        kmean = jnp.where(lane == n, jnp.sum(kb, axis=1, keepdims=True) * (1.0 / MOBA_BLOCK), kmean)
        s_ref[:, n * MOBA_BLOCK:(n + 1) * MOBA_BLOCK] = _dot(qb, kb.astype(BF16))
    gs = _dot_3pass(q, kmean)
    n_idx = _iota(gs.shape, 1)
    gs = jnp.where(n_idx < n_past, gs, NEG)
    bias = jnp.where(_topk_mask(gs, n_past, MOBA_TOPK) & (n_idx < n_past), 0.0, NEG).astype(BF16)
    s = s_ref[...] + _dot(bias, e_ref[...])
    own_ok = _iota((rows, PAGE_SIZE), 1) <= _iota((rows, PAGE_SIZE), 0) % tq
    s_own = jnp.where(own_ok, _dot(qb, new_ref[0, 0].astype(BF16)), NEG)
    m = jnp.maximum(jnp.max(s, axis=-1, keepdims=True), jnp.max(s_own, axis=-1, keepdims=True))
    p = jnp.exp(s - m)
    p_own = jnp.exp(s_own - m)
    l = jnp.sum(p, axis=-1, keepdims=True) + jnp.sum(p_own, axis=-1, keepdims=True)
    pb = p.astype(BF16)
    acc = _dot_nt(p_own.astype(BF16), new_ref[0, 1].astype(BF16))
    for n in range(n_past):
        acc = acc + _dot_nt(pb[:, n * MOBA_BLOCK:(n + 1) * MOBA_BLOCK], block(n, 1).astype(BF16))
    oh = acc / l
    grp = _iota((tq, 256), 1) // HEAD_DIM
    out = oh[0:tq]
    for h in range(1, MOBA_HEADS):
        out = jnp.where(grp == h, oh[h * tq:(h + 1) * tq], out)
    o_ref[0] = out


def _moba_sample_call(cache5, layer, page_table, qbd, newt):
    b, n_pages = page_table.shape
    past = n_pages * PAGE_SIZE
    rows = qbd.shape[1]
    tq = rows // MOBA_HEADS
    assert past % MOBA_BLOCK == 0 and tq <= PAGE_SIZE and past // MOBA_BLOCK <= LANES
    page_specs = [pl.BlockSpec((1, 1, 2, 256, PAGE_SIZE), functools.partial(
        lambda i, pt, k: (layer, pt[i, k], 0, 0, 0), k=k)) for k in range(n_pages)]
    per_b = lambda a: pl.BlockSpec((1,) + a.shape[1:], lambda i, pt: (i,) + (0,) * (a.ndim - 1))
    grid_spec = pltpu.PrefetchScalarGridSpec(
        num_scalar_prefetch=1, grid=(b,),
        in_specs=page_specs + [per_b(qbd), per_b(newt)],
        out_specs=pl.BlockSpec((1, tq, 256), lambda i, pt: (i, 0, 0)),
        scratch_shapes=[pltpu.VMEM((rows, past), F32), pltpu.VMEM((LANES, past), BF16)])
    return pl.pallas_call(
        functools.partial(_moba_sample_kernel, past=past, tq=tq, n_pages=n_pages),
        grid_spec=grid_spec, out_shape=jax.ShapeDtypeStruct((b, tq, 256), F32),
        compiler_params=pltpu.CompilerParams(dimension_semantics=("arbitrary",), vmem_limit_bytes=VMEM_LIMIT),
        name="moba_sample",
    )(page_table, *([cache5] * n_pages), qbd, newt)


def _merge_kernel(x_ref, oab_ref, oct_ref, odt_ref, n1_ref, wgt_ref, wbr_ref, wo_ref, y_ref, *, subs):
    sm = x_ref.shape[0] // subs
    for si in range(subs):
        rs = slice(si * sm, (si + 1) * sm)
        x = x_ref[rs, :]
        xn = (x * lax.rsqrt(jnp.mean(x * x, axis=-1, keepdims=True) + EPS) * n1_ref[...]).astype(BF16)
        branches = [oab_ref[rs, 0:256], oab_ref[rs, 256:512], oct_ref[:, rs].T.astype(BF16),
                    odt_ref[:, rs].T.astype(BF16)]
        merged = jnp.zeros(x.shape, F32)
        for n in range(N_BRANCH):
            g = jax.nn.sigmoid(_dot_nt(xn, wgt_ref[n * D_MODEL:(n + 1) * D_MODEL, :]))
            merged = merged + g * _dot(branches[n], wbr_ref[n])
        y_ref[rs, :] = x + _dot(merged.astype(BF16), wo_ref[...])


def _merge_call(x2d, oab, oct, odt, lw, *, tm):
    rows = x2d.shape[0]
    row = lambda w: pl.BlockSpec((tm, w), lambda i: (i, 0))
    col = lambda r: pl.BlockSpec((r, tm), lambda i: (0, i))
    full = lambda a: pl.BlockSpec(a.shape, lambda i: (0,) * a.ndim)
    ws = [lw['norm1'], lw['wg_t'], lw['w_branch'], lw['w_o']]
    return pl.pallas_call(
        functools.partial(_merge_kernel, subs=2 if tm % (2 * LANES) == 0 else 1), grid=(rows // tm,),
        in_specs=[row(D_MODEL), row(512), col(256), col(256)] + [full(a) for a in ws],
        out_specs=row(D_MODEL), out_shape=jax.ShapeDtypeStruct((rows, D_MODEL), F32),
        compiler_params=pltpu.CompilerParams(dimension_semantics=("arbitrary",), vmem_limit_bytes=VMEM_LIMIT),
        name="merge",
    )(x2d, oab, oct, odt, *ws)


def _ffn_kernel(*refs, tiles_per_seq, seg, subs):
    it = iter(refs)
    x_ref, n2_ref, wgate_ref, wup_ref, cw_ref, cb_ref, wdown_ref = (next(it) for _ in range(7))
    if seg is not None:
        h1_ref, h2_ref = next(it), next(it)
    y_ref, tail_ref = next(it), next(it)
    if seg is None:
        prev_ref = next(it)

    tm = x_ref.shape[0]
    sm = tm // subs
    if seg is None:
        @pl.when(pl.program_id(0) % tiles_per_seq == 0)
        def _():
            prev_ref[...] = jnp.zeros_like(prev_ref)

        prev = prev_ref[...]

    for si in range(subs):
        rs = slice(si * sm, (si + 1) * sm)
        x = x_ref[rs, :]
        h2 = (x * lax.rsqrt(jnp.mean(x * x, axis=-1, keepdims=True) + EPS) * n2_ref[...]).astype(BF16)
        gate = _dot(h2, wgate_ref[...])
        if seg is None:
            g1, g2 = _shift_rows(gate, prev[SUBLANES - 1:SUBLANES, :], prev[SUBLANES - 2:SUBLANES - 1, :], None)
            prev = gate[sm - SUBLANES:, :]
        else:
            g1, g2 = _shift_rows(gate, h1_ref[...], h2_ref[...], seg)
            tail_ref[...] = gate.reshape(tail_ref.shape)
        ac = g2 * cw_ref[0:1, :] + cb_ref[...] + g1 * cw_ref[1:2, :] + gate * cw_ref[2:3, :]
        act = jax.nn.silu(ac) * _dot(h2, wup_ref[...])
        y_ref[rs, :] = x + _dot(act.astype(BF16), wdown_ref[...])

    if seg is None:
        prev_ref[...] = prev
        tail_ref[0] = prev


def _ffn_call(x2d, lw, halos, *, tm, tiles_per_seq, n_seq, seg):
    rows = x2d.shape[0]
    row = lambda w: pl.BlockSpec((tm, w), lambda i: (i, 0))
    full = lambda a: pl.BlockSpec(a.shape, lambda i: (0,) * a.ndim)
    ws = [lw['norm2'], lw['w_gate'], lw['w_up'], lw['ffn_conv_w'], lw['ffn_conv_b'], lw['w_down']]
    ins = [x2d] + ws
    in_specs = [row(D_MODEL)] + [full(a) for a in ws]
    scratch = []
    if seg is None:
        tail_spec = pl.BlockSpec((1, SUBLANES, D_FF), lambda i: (i // tiles_per_seq, 0, 0))
        scratch.append(pltpu.VMEM((SUBLANES, D_FF), F32))
    else:
        ins += list(halos)
        in_specs += [row(D_FF), row(D_FF)]
        tail_spec = pl.BlockSpec((n_seq, SUBLANES, D_FF), lambda i: (0, 0, 0))
    return pl.pallas_call(
        functools.partial(_ffn_kernel, tiles_per_seq=tiles_per_seq, seg=seg, subs=2 if seg is None else 1),
        grid=(rows // tm,), in_specs=in_specs, out_specs=[row(D_MODEL), tail_spec],
        out_shape=[jax.ShapeDtypeStruct((rows, D_MODEL), F32), jax.ShapeDtypeStruct((n_seq, SUBLANES, D_FF), F32)],
        scratch_shapes=scratch,
        compiler_params=pltpu.CompilerParams(dimension_semantics=("arbitrary",), vmem_limit_bytes=VMEM_LIMIT),
        name="ffn_prompt" if seg is None else "ffn_sample",
    )(*ins)


def _rope_tables_t(pos):
    half = ROT_DIM // 2
    inv = jnp.power(jnp.float32(ROPE_THETA), -jnp.arange(half, dtype=F32) / half)
    ang = inv[:, None] * pos.astype(F32)[None, :]
    return jnp.cos(ang), jnp.sin(ang)


def _layer_weights(l, p, sample_tq, dec_batch):
    wt = jnp.transpose(p['w_in'], (2, 0, 1))[:, l, :]
    o_ck, o_cv, o_g = 1536, 1728, 1920
    o_dq = o_g + 3 * NSA_HEADS
    o_br = o_dq + 3 * 256
    hd = HEAD_DIM
    w_tm = jnp.concatenate([wt[0:1280], wt[o_g:o_g + 12], jnp.zeros((LANES - 12, D_MODEL), F32)], axis=0)
    w_tr = jnp.concatenate([
        wt[1280:1536], wt[o_dq:o_dq + 256],
        wt[o_ck:o_ck + hd], wt[o_cv:o_cv + hd], wt[o_ck + hd:o_ck + 2 * hd], wt[o_cv + hd:o_cv + 2 * hd],
        wt[o_ck + 2 * hd:o_ck + 3 * hd], wt[o_cv + 2 * hd:o_cv + 3 * hd],
        wt[o_dq + 256:o_dq + 768], wt[o_g:o_g + 12], jnp.zeros((2 * SUBLANES - 12, D_MODEL), F32)], axis=0)
    lw = {'w_tm': w_tm.astype(BF16), 'w_tr': w_tr.astype(BF16), 'wg_t': wt[o_br:].astype(BF16)}
    row = lambda a: a.reshape(1, -1)
    lw['norm1'] = row(p['norm1'][l])
    lw['norm2'] = row(p['norm2'][l])
    lw['conv_w'] = p['conv_w'][l]
    lw['conv_b'] = row(p['conv_b'][l])
    lw['ln_g'] = row(p['gmlp_ln_g'][l])
    lw['ln_b'] = row(p['gmlp_ln_b'][l])
    tril = jnp.tril(jnp.ones((GMLP_CHUNK, GMLP_CHUNK), bool))
    ws = jnp.where(tril[None], p['gmlp_ws'][l], 0.0)
    bs = p['gmlp_bs'][l]
    lw['mix_p'] = ws.reshape(GMLP_GROUPS * GMLP_CHUNK, GMLP_CHUNK).astype(BF16)
    lw['mixb_p'] = jnp.repeat(bs.T, HEAD_DIM, axis=1)
    eye = jnp.eye(dec_batch, dtype=F32)
    lw['mix_s'] = jnp.concatenate([jnp.kron(eye, ws[g, :sample_tq, :sample_tq]) for g in range(GMLP_GROUPS)],
                                  axis=0).astype(BF16)
    lw['mixb_s'] = jnp.tile(jnp.repeat(bs.T[:sample_tq], HEAD_DIM, axis=1), (dec_batch, 1))
    kg = p['nsa_k_gain'][l]
    one = jnp.ones((hd,), F32)
    gains = jnp.concatenate([jnp.tile(p['nsa_q_gain'][l], NSA_HEADS), jnp.tile(p['moba_q_gain'][l], MOBA_HEADS),
                             kg[0], one, kg[1], one, kg[2], one, jnp.tile(p['moba_k_gain'][l], MOBA_HEADS)])
    lw['gain_t'] = jnp.broadcast_to(gains[:, None], (R_MV, LANES))
    w1c = p['nsa_cmp_w1'][l].reshape(2, NSA_CMP_LEN, HEAD_DIM, NSA_CMP_HIDDEN)
    pos = p['nsa_cmp_pos'][l]
    zero = jnp.zeros((HEAD_DIM, NSA_CMP_HIDDEN), F32)

    def pair_w(r_off):
        mats = []
        for pp in range(NSA_CMP_STRIDE // 2):
            blocks = []
            for r in (2 * pp, 2 * pp + 1):
                blocks.append(jnp.concatenate([w1c[0, r + r_off], zero], axis=1))
                blocks.append(jnp.concatenate([zero, w1c[1, r + r_off]], axis=1))
            mats.append(jnp.concatenate(blocks, axis=0))
        return jnp.stack(mats).astype(BF16)

    def pair_pos(r_off):
        return jnp.stack([jnp.concatenate([pos[0, 2 * pp + r_off], pos[1, 2 * pp + r_off],
                                           pos[0, 2 * pp + 1 + r_off], pos[1, 2 * pp + 1 + r_off]])
                          for pp in range(NSA_CMP_STRIDE // 2)])

    lw['cmp_wa'], lw['cmp_wb'] = pair_w(0), pair_w(NSA_CMP_STRIDE)
    lw['cmp_pt'], lw['cmp_pb'] = pair_pos(0), pair_pos(NSA_CMP_STRIDE)
    w2 = p['nsa_cmp_w2'][l]
    z2 = jnp.zeros((NSA_CMP_HIDDEN, HEAD_DIM), F32)
    lw['cmp_w2'] = jnp.concatenate([jnp.concatenate([w2[0], z2], axis=1),
                                    jnp.concatenate([z2, w2[1]], axis=1)], axis=0).astype(BF16)
    lw['w_branch'] = p['w_branch'][l].astype(BF16)
    lw['w_o'] = p['w_o'][l].astype(BF16)
    lw['w_gate'] = p['w_ffn_gate'][l].astype(BF16)
    lw['w_up'] = p['w_ffn_up'][l].astype(BF16)
    lw['w_down'] = p['w_ffn_down'][l].astype(BF16)
    lw['ffn_conv_w'] = p['ffn_conv_w'][l]
    lw['ffn_conv_b'] = row(p['ffn_conv_b'][l])
    return lw


def _halos(buf, tq):
    b, _, c = buf.shape
    z = jnp.zeros((b, tq - 1, c), F32)
    h1 = jnp.concatenate([buf[:, 1:2], z], axis=1)
    h2 = jnp.concatenate([buf[:, 0:2], z[:, 1:]], axis=1)
    return h1.reshape(b * tq, c), h2.reshape(b * tq, c)


def _row_tile(rows, cap):
    tm = cap
    while rows % tm:
        tm //= 2
    return tm


def _per_seq_pages(xt, b, tq):
    c = xt.shape[0]
    x = xt.reshape(c, b, tq).transpose(1, 0, 2)
    return jnp.concatenate([x, jnp.zeros((b, c, PAGE_SIZE - tq), F32)], axis=-1)


def _prompt_layer(x2d, lw, tables, b, t):
    tm = _row_tile(t, 512)
    assert tm % MOBA_BLOCK == 0 and tm % SLC_TILE == 0
    tps = t // tm
    pw = dict(lw, mix=lw['mix_p'], mixb=lw['mixb_p'])
    (oab, qt, nsat, wint, mobat, gt, _, ztail, cmp_tm, slc_tm, win_tm, mk_tm, vt16, kmean) = _proj_call(
        x2d, pw, tables, None, tm=tm, tiles_per_seq=tps, n_seq=b, chunk=GMLP_CHUNK, seg=None)
    kcvc, kcvct = _compress_prompt_call(cmp_tm.reshape(b, t, LANES), lw)
    o_ct = _nsa_prompt_call(qt, gt, kcvc, kcvct, slc_tm.reshape(b, t, LANES), win_tm.reshape(b, t, LANES), vt16)
    nb = t // MOBA_BLOCK
    kmean3 = kmean[:, 0:tm // MOBA_BLOCK].reshape(b, nb, 256)
    nbp = -(-nb // SUBLANES) * SUBLANES
    if nbp != nb:
        kmean3 = jnp.concatenate([kmean3, jnp.zeros((b, nbp - nb, 256), F32)], axis=1)
    o_dt = _moba_prompt_call(qt, kmean3, mk_tm.reshape(b, t, 256), vt16)
    x_mid = _merge_call(x2d, oab, o_ct, o_dt, lw, tm=tm)
    y, ftail = _ffn_call(x_mid, lw, None, tm=tm, tiles_per_seq=tps, n_seq=b, seg=None)
    hd = HEAD_DIM
    wlen = min(NSA_WINDOW, t)
    state = (nsat.reshape(b, 4, hd, t).transpose(0, 3, 1, 2),
             mobat.reshape(b, 2, MOBA_HEADS, hd, t).transpose(0, 4, 1, 2, 3),
             wint[:, :, t - wlen:].reshape(b, 2, hd, wlen).transpose(0, 3, 1, 2),
             ztail[:, SUBLANES - 2:], ftail[:, SUBLANES - 2:])
    return y, state


def _sample_layer(x2d, lw, tables, b, tq, layer, nsa_cache5, moba_cache5, page_table, win_state5, conv_state,
                  ffn_state):
    rows = b * tq
    hd = HEAD_DIM
    pw = dict(lw, mix=lw['mix_s'], mixb=lw['mixb_s'])
    oab, qt, nsat, wint, mobat, _, gtm, ztail, vn = _proj_call(
        x2d, pw, tables, _halos(conv_state, tq), tm=rows, tiles_per_seq=1, n_seq=b, chunk=rows, seg=tq)
    nsat, wint, mobat = nsat[0], wint[0], mobat[0]
    kcvct = _compress_paged_call(nsa_cache5, layer, page_table, lw)
    q32 = qt[0:256].reshape(NSA_HEADS, hd, b, tq).transpose(2, 0, 3, 1).reshape(b, NSA_HEADS * tq, hd)
    g4 = gtm[:, 0:3 * NSA_HEADS].reshape(b, tq, NSA_HEADS, 3).transpose(0, 2, 1, 3).reshape(b, NSA_HEADS * tq, 3)
    g32 = jnp.concatenate([g4, jnp.zeros((b, NSA_HEADS * tq, LANES - 3), F32)], axis=-1)
    newt = _per_seq_pages(nsat[2 * hd:4 * hd], b, tq).reshape(b, 2, hd, PAGE_SIZE)
    wnewt = _per_seq_pages(wint, b, tq).reshape(b, 2, hd, PAGE_SIZE)
    o32 = _nsa_sample_call(nsa_cache5, layer, page_table, q32, g32, kcvct, newt, win_state5, wnewt)
    o_ct = o32.reshape(b, NSA_HEADS, tq, hd).transpose(1, 3, 0, 2).reshape(256, rows)
    qm = qt[256:512].reshape(MOBA_HEADS, hd, b, tq).transpose(2, 0, 3, 1)
    eye = jnp.eye(MOBA_HEADS, dtype=F32)
    qbd = (qm[:, :, :, None, :] * eye[None, :, None, :, None]).reshape(b, MOBA_HEADS * tq, 256)
    mnewt = _per_seq_pages(mobat, b, tq).reshape(b, 2, 256, PAGE_SIZE)
    o_dt = _moba_sample_call(moba_cache5, layer, page_table, qbd, mnewt).transpose(2, 0, 1).reshape(256, rows)
    x_mid = _merge_call(x2d, oab, o_ct, o_dt, lw, tm=rows)
    y, ftail = _ffn_call(x_mid, lw, _halos(ffn_state, tq), tm=rows, tiles_per_seq=1, n_seq=b, seg=tq)
    wnew = wint.reshape(2, hd, b, tq).transpose(2, 0, 1, 3)
    win_all = jnp.concatenate([win_state5[layer], wnew], axis=-1)
    win_new = win_all[..., win_all.shape[-1] - NSA_WINDOW:].transpose(0, 3, 1, 2)
    state = (nsat.T.reshape(b, tq, 4, hd), mobat.T.reshape(b, tq, 2, MOBA_HEADS, hd), win_new,
             ztail[:, tq - 2:tq], ftail[:, tq - 2:tq], vn.reshape(b, tq, 256))
    return y, state


def kernel(x_prompt, x_sample, cache_nsa_kv, cache_moba_kv, state_nsa_win, state_conv, state_ffn_conv, page_table, norm1, w_in, conv_w, conv_b, gmlp_ln_g, gmlp_ln_b, gmlp_ws, gmlp_bs, nsa_q_gain, nsa_k_gain, nsa_cmp_pos, nsa_cmp_w1, nsa_cmp_w2, moba_q_gain, moba_k_gain, w_branch, w_o, norm2, w_ffn_gate, w_ffn_up, ffn_conv_w, ffn_conv_b, w_ffn_down):
    params = dict(norm1=norm1, w_in=w_in, conv_w=conv_w, conv_b=conv_b, gmlp_ln_g=gmlp_ln_g, gmlp_ln_b=gmlp_ln_b,
                  gmlp_ws=gmlp_ws, gmlp_bs=gmlp_bs, nsa_q_gain=nsa_q_gain, nsa_k_gain=nsa_k_gain,
                  nsa_cmp_pos=nsa_cmp_pos, nsa_cmp_w1=nsa_cmp_w1, nsa_cmp_w2=nsa_cmp_w2, moba_q_gain=moba_q_gain,
                  moba_k_gain=moba_k_gain, w_branch=w_branch, w_o=w_o, norm2=norm2, w_ffn_gate=w_ffn_gate,
                  w_ffn_up=w_ffn_up, ffn_conv_w=ffn_conv_w, ffn_conv_b=ffn_conv_b, w_ffn_down=w_ffn_down)
    bp, tp, _ = x_prompt.shape
    bs, ts, _ = x_sample.shape
    depth = w_in.shape[0]
    n_pool = cache_nsa_kv.shape[1]
    past = page_table.shape[1] * PAGE_SIZE
    assert ts == SUBLANES and state_nsa_win.shape[2] == NSA_WINDOW
    nsa_cache5 = jnp.transpose(cache_nsa_kv, (0, 1, 3, 4, 2))
    moba_cache5 = jnp.transpose(cache_moba_kv, (0, 1, 3, 4, 5, 2)).reshape(depth, n_pool, 2, 256, PAGE_SIZE)
    win_state5 = jnp.transpose(state_nsa_win, (0, 1, 3, 4, 2))
    tab_p = _rope_tables_t(jnp.arange(tp, dtype=jnp.int32))
    tab_s = tuple(jnp.tile(a, (1, bs)) for a in _rope_tables_t(past + jnp.arange(ts, dtype=jnp.int32)))
    yp = x_prompt.reshape(bp * tp, D_MODEL)
    ys = x_sample.reshape(bs * ts, D_MODEL)
    sp, ss = [], []
    for l in range(depth):
        lw = _layer_weights(l, params, ts, bs)
        yp, st_p = _prompt_layer(yp, lw, tab_p, bp, tp)
        ys, st_s = _sample_layer(ys, lw, tab_s, bs, ts, l, nsa_cache5, moba_cache5, page_table, win_state5,
                                 state_conv[l], state_ffn_conv[l])
        sp.append(st_p)
        ss.append(st_s)
    stack = lambda lst, k: jnp.stack([s[k] for s in lst])
    return (yp.reshape(bp, tp, D_MODEL), ys.reshape(bs, ts, D_MODEL),
            stack(sp, 0), stack(ss, 0), stack(sp, 1), stack(ss, 1), stack(sp, 2), stack(ss, 2),
            stack(sp, 3), stack(ss, 3), stack(sp, 4), stack(ss, 4), stack(ss, 5))
```

```python
import functools

import jax
import jax.numpy as jnp
from jax import lax
from jax.experimental import pallas as pl
from jax.experimental.pallas import tpu as pltpu

F32 = jnp.float32
BF16 = jnp.bfloat16

D_MODEL = 1024
HEAD_DIM = 64
ROT_DIM = HEAD_DIM // 4
ROPE_THETA = 500000.0
PAGE_SIZE = 128
BRANCH_CH = 256
N_BRANCH = 4
GMLP_GROUPS = 4
GMLP_CHUNK = 128
NSA_HEADS = 4
NSA_CMP_LEN = 32
NSA_CMP_STRIDE = 16
NSA_CMP_HIDDEN = 256
NSA_SLC_BLOCK = 64
NSA_TOPN = 16
NSA_WINDOW = 512
MOBA_HEADS = 4
MOBA_BLOCK = 256
MOBA_TOPK = 3
D_FF = 2816
EPS = 1e-6
NEG = -1e30
BIG = 1e30
LOG2E = 1.4426950408889634

SLC_TILE = 512
LANES = 128
SUBLANES = 8
VMEM_LIMIT = 56 * 1024 * 1024

C_AH, C_AB, C_AC, C_U, C_V, C_G, C_END = 0, 256, 512, 768, 1024, 1280, 1408
R_Q, R_NSA, R_WIN, R_MK, R_MV, R_G, R_END = 0, 512, 768, 896, 1152, 1408, 1424
NORMED_GROUPS = (0, 1, 2, 3, 4, 5, 6, 7, 8, 10, 12, 14, 15, 16, 17)

_NT = (((1,), (1,)), ((), ()))


def _dot(a, b):
    return jnp.dot(a, b, preferred_element_type=F32)


def _dot_nt(a, b):
    return lax.dot_general(a, b, _NT, preferred_element_type=F32)


def _split(a):
    hi = a.astype(BF16)
    lo = (a - hi.astype(F32)).astype(BF16)
    return hi, lo


def _dot_3pass(a, b):
    ah, al = _split(a)
    bh, bl = _split(b)
    return _dot(ah, bh) + _dot(ah, bl) + _dot(al, bh)


def _iota(shape, dim):
    return lax.broadcasted_iota(jnp.int32, shape, dim)


def _lane_tile(a, n):
    return a if n == 1 else jnp.concatenate([a] * n, axis=1)


def _norm_rope_t(xh, gain, cos, sin):
    ss = jnp.sum(xh * xh, axis=0, keepdims=True)
    y = xh * lax.rsqrt(ss * (1.0 / HEAD_DIM) + EPS) * gain
    half = ROT_DIM // 2
    y0, y1 = y[0:half], y[half:ROT_DIM]
    return jnp.concatenate([y0 * cos - y1 * sin, y1 * cos + y0 * sin, y[ROT_DIM:]], axis=0)


def _shift_rows(z, prev1, prev2, seg):
    rows = _iota(z.shape, 0)
    z1 = pltpu.roll(z, 1, 0)
    z2 = pltpu.roll(z, 2, 0)
    if seg is None:
        z1 = jnp.where(rows == 0, prev1, z1)
        z2 = jnp.where(rows == 0, prev2, jnp.where(rows == 1, prev1, z2))
    else:
        t = rows % seg
        z1 = jnp.where(t == 0, prev1, z1)
        z2 = jnp.where(t <= 1, prev2, z2)
    return z1, z2


def _proj_kernel(*refs, tiles_per_seq, chunk, seg, subs):
    it = iter(refs)
    x_ref, n1_ref, wtm_ref, wtr_ref, cw_ref, cb_ref, lng_ref, lnb_ref, mix_ref, mixb_ref = (next(it) for _ in range(10))
    gain_ref, cos_ref, sin_ref = (next(it) for _ in range(3))
    if seg is not None:
        h1_ref, h2_ref = next(it), next(it)
    oab_ref, qt_ref, nsat_ref, wint_ref, mobat_ref, gt_ref, gtm_ref, ztail_ref = (next(it) for _ in range(8))
    if seg is not None:
        vn_ref = next(it)
    else:
        cmp_ref, slc_ref, win_ref, mk_ref, vt_ref, kmean_ref, zprev_ref = (next(it) for _ in range(7))

    tm = x_ref.shape[0]
    sm = tm // subs
    if seg is None:
        @pl.when(pl.program_id(0) % tiles_per_seq == 0)
        def _():
            zprev_ref[...] = jnp.zeros_like(zprev_ref)

        prev = zprev_ref[...]
        kmean_ref[...] = jnp.zeros_like(kmean_ref)

    for si in range(subs):
        r0 = si * sm
        rs = slice(r0, r0 + sm)
        x = x_ref[rs, :]
        xn = (x * lax.rsqrt(jnp.mean(x * x, axis=-1, keepdims=True) + EPS) * n1_ref[...]).astype(BF16)
        proj = _dot_nt(xn, wtm_ref[...])
        projt = _dot_nt(wtr_ref[...], xn)

        z = proj[:, C_AC:C_AC + 256] * proj[:, C_AH:C_AH + 256]
        if seg is None:
            z1, z2 = _shift_rows(z, prev[SUBLANES - 1:SUBLANES, :], prev[SUBLANES - 2:SUBLANES - 1, :], None)
            prev = z[sm - SUBLANES:, :]
        else:
            z1, z2 = _shift_rows(z, h1_ref[...], h2_ref[...], seg)
            ztail_ref[...] = z.reshape(ztail_ref.shape)
        ya = z2 * cw_ref[0:1, :] + cb_ref[...] + z1 * cw_ref[1:2, :] + z * cw_ref[2:3, :]
        oab_ref[rs, 0:256] = (proj[:, C_AB:C_AB + 256] * ya).astype(BF16)

        u = jax.nn.gelu(proj[:, C_U:C_U + 256])
        v = jax.nn.gelu(proj[:, C_V:C_V + 256])
        vc = v - jnp.mean(v, axis=-1, keepdims=True)
        vn = vc * lax.rsqrt(jnp.mean(vc * vc, axis=-1, keepdims=True) + EPS) * lng_ref[...] + lnb_ref[...]
        if seg is not None:
            vn_ref[...] = vn
        vnb = vn.astype(BF16)
        grp = _iota((chunk, 256), 1) // HEAD_DIM
        for c in range(sm // chunk):
            r = _dot(mix_ref[...], vnb[c * chunk:(c + 1) * chunk, :])
            s = r[0:chunk]
            for g in range(1, GMLP_GROUPS):
                s = jnp.where(grp == g, r[g * chunk:(g + 1) * chunk], s)
            s = s + mixb_ref[...]
            oab_ref[r0 + c * chunk:r0 + (c + 1) * chunk, 256:512] = (u[c * chunk:(c + 1) * chunk, :] * s).astype(BF16)

        gtm_ref[rs, :] = jax.nn.sigmoid(proj[:, C_G:C_END])
        gt_ref[:, rs] = jax.nn.sigmoid(projt[R_G:R_END, :])

        cos, sin = cos_ref[:, rs], sin_ref[:, rs]
        rep = sm // LANES
        groups = []
        for g in range(R_MV // HEAD_DIM):
            xh = projt[g * HEAD_DIM:(g + 1) * HEAD_DIM, :]
            if g in NORMED_GROUPS:
                xh = _norm_rope_t(xh, _lane_tile(gain_ref[g * HEAD_DIM:(g + 1) * HEAD_DIM, :], rep), cos, sin)
            groups.append(xh)
        qkv = jnp.concatenate(groups, axis=0)
        qt_ref[:, rs] = qkv[R_Q:R_NSA]
        nsat_ref[0, :, rs] = qkv[R_NSA:R_WIN]
        wint_ref[0, :, rs] = qkv[R_WIN:R_MK]
        mobat_ref[0, 0:256, rs] = qkv[R_MK:R_MV]
        vmt = projt[R_MV:R_G, :]
        mobat_ref[0, 256:512, rs] = vmt

        if seg is None:
            nsa_tm = qkv[R_NSA:R_WIN].T
            cmp_ref[rs, :] = nsa_tm[:, 0:LANES]
            lane = _iota((sm, LANES), 1) - HEAD_DIM
            blk_in_tile = ((r0 + _iota((sm, LANES), 0)) // NSA_SLC_BLOCK) % (SLC_TILE // NSA_SLC_BLOCK)
            slc_ref[rs, :] = jnp.where(lane < 0, nsa_tm[:, LANES:2 * LANES],
                                       jnp.where(lane == blk_in_tile, 1.0, 0.0)).astype(BF16)
            win_ref[rs, :] = qkv[R_WIN:R_MK].T.astype(BF16)
            mk_tm = qkv[R_MK:R_MV].T
            mk_ref[rs, :] = mk_tm.astype(BF16)
            vt_ref[0, 0:256, rs] = vmt.astype(BF16)
            vt_ref[0, 256:320, rs] = qkv[R_NSA + 192:R_NSA + 256].astype(BF16)
            vt_ref[0, 320:384, rs] = qkv[R_WIN + 64:R_WIN + 128].astype(BF16)
            for r in range(sm // MOBA_BLOCK):
                b0 = r0 // MOBA_BLOCK + r
                kmean_ref[0, b0:b0 + 1, :] = jnp.sum(mk_tm[r * MOBA_BLOCK:(r + 1) * MOBA_BLOCK, :], axis=0,
                                                     keepdims=True) * (1.0 / MOBA_BLOCK)

    if seg is None:
        zprev_ref[...] = prev
        ztail_ref[0] = prev


def _proj_call(x2d, lw, tables, halos, *, tm, tiles_per_seq, n_seq, chunk, seg):
    rows = x2d.shape[0]
    nt = rows // tm
    tps = tiles_per_seq
    n_out_seq = n_seq if seg is None else 1
    t_out = rows // n_out_seq
    row = lambda w: pl.BlockSpec((tm, w), lambda i: (i, 0))
    col = lambda r: pl.BlockSpec((r, tm), lambda i: (0, i))
    seq3 = lambda r: pl.BlockSpec((1, r, tm), lambda i: (i // tps, 0, i % tps))
    full = lambda a: pl.BlockSpec(a.shape, lambda i: (0,) * a.ndim)
    n_tab = tables[0].shape[1] // tm
    tab = pl.BlockSpec((ROT_DIM // 2, tm), lambda i: (0, i % n_tab))
    ws = [lw['norm1'], lw['w_tm'], lw['w_tr'], lw['conv_w'], lw['conv_b'], lw['ln_g'], lw['ln_b'], lw['mix'],
          lw['mixb'], lw['gain_t']]
    ins = [x2d] + ws + list(tables)
    in_specs = [row(D_MODEL)] + [full(a) for a in ws] + [tab] * 2
    sds = jax.ShapeDtypeStruct
    out_shape = [sds((rows, 512), BF16), sds((512, rows), F32), sds((n_out_seq, 256, t_out), F32),
                 sds((n_out_seq, 128, t_out), F32), sds((n_out_seq, 512, t_out), F32),
                 sds((2 * SUBLANES, rows), F32), sds((rows, LANES), F32), sds((n_seq, SUBLANES, 256), F32)]
    out_specs = [row(512), col(512), seq3(256), seq3(128), seq3(512), col(2 * SUBLANES), row(LANES)]
    scratch = []
    if seg is None:
        out_specs.append(pl.BlockSpec((1, SUBLANES, 256), lambda i: (i // tps, 0, 0)))
        out_shape += [sds((rows, LANES), F32), sds((rows, LANES), BF16), sds((rows, LANES), BF16),
                      sds((rows, 256), BF16), sds((n_seq, 384, t_out), BF16), sds((nt, SUBLANES, 256), F32)]
        out_specs += [row(LANES), row(LANES), row(LANES), row(256), seq3(384),
                      pl.BlockSpec((1, SUBLANES, 256), lambda i: (i, 0, 0))]
        scratch.append(pltpu.VMEM((SUBLANES, 256), F32))
    else:
        ins += list(halos)
        in_specs += [row(256), row(256)]
        out_specs.append(pl.BlockSpec((n_seq, SUBLANES, 256), lambda i: (0, 0, 0)))
        out_shape.append(sds((rows, 256), F32))
        out_specs.append(row(256))
    return pl.pallas_call(
        functools.partial(_proj_kernel, tiles_per_seq=tps, chunk=chunk, seg=seg,
                          subs=tm // MOBA_BLOCK if seg is None else 1),
        grid=(nt,), in_specs=in_specs, out_specs=out_specs, out_shape=out_shape, scratch_shapes=scratch,
        compiler_params=pltpu.CompilerParams(dimension_semantics=("arbitrary",), vmem_limit_bytes=VMEM_LIMIT),
        name="proj_prompt" if seg is None else "proj_sample",
    )(*ins)


def _compress_compute(src_ref, wa_ref, wb_ref, pt_ref, pb_ref, w2_ref, groups):
    acc_a = jnp.zeros((groups, 2 * NSA_CMP_HIDDEN), F32)
    acc_b = jnp.zeros((groups, 2 * NSA_CMP_HIDDEN), F32)
    for p in range(NSA_CMP_STRIDE // 2):
        xp = jnp.concatenate([src_ref[pl.ds(2 * p, groups, stride=NSA_CMP_STRIDE), :],
                              src_ref[pl.ds(2 * p + 1, groups, stride=NSA_CMP_STRIDE), :]], axis=1)
        acc_a = acc_a + _dot((xp + pt_ref[p:p + 1, :]).astype(BF16), wa_ref[p])
        acc_b = acc_b + _dot((xp + pb_ref[p:p + 1, :]).astype(BF16), wb_ref[p])
    hdn = jax.nn.gelu(acc_a + pltpu.roll(acc_b, groups - 1, 0))
    return _dot(hdn.astype(BF16), w2_ref[...])


def _compress_prompt_kernel(src_ref, wa_ref, wb_ref, pt_ref, pb_ref, w2_ref, out_ref, outt_ref, *, groups):
    out = _compress_compute(src_ref.at[0], wa_ref, wb_ref, pt_ref, pb_ref, w2_ref, groups)
    out_ref[0] = out
    outt_ref[0] = out.T


def _compress_paged_kernel(ptab_ref, *refs, groups, n_seq):
    del ptab_ref
    n_in = len(refs) - 7
    pages = refs[:n_in]
    wa_ref, wb_ref, pt_ref, pb_ref, w2_ref, outt_ref, stage_ref = refs[n_in:]
    n_pages = n_in // n_seq
    for s in range(n_seq):
        for k in range(n_pages):
            pg = pages[s * n_pages + k][0, 0].reshape(2 * HEAD_DIM, PAGE_SIZE)
            stage_ref[s, k * PAGE_SIZE:(k + 1) * PAGE_SIZE, :] = pg.T
    for s in range(n_seq):
        outt_ref[s] = _compress_compute(stage_ref.at[s], wa_ref, wb_ref, pt_ref, pb_ref, w2_ref, groups).T


def _compress_prompt_call(cmp_tm3, lw):
    b, t, _ = cmp_tm3.shape
    groups = t // NSA_CMP_STRIDE
    full = lambda a: pl.BlockSpec(a.shape, lambda i: (0,) * a.ndim)
    ws = [lw['cmp_wa'], lw['cmp_wb'], lw['cmp_pt'], lw['cmp_pb'], lw['cmp_w2']]
    return pl.pallas_call(
        functools.partial(_compress_prompt_kernel, groups=groups),
        grid=(b,),
        in_specs=[pl.BlockSpec((1, t, LANES), lambda i: (i, 0, 0))] + [full(a) for a in ws],
        out_specs=[pl.BlockSpec((1, groups, LANES), lambda i: (i, 0, 0)),
                   pl.BlockSpec((1, LANES, groups), lambda i: (i, 0, 0))],
        out_shape=[jax.ShapeDtypeStruct((b, groups, LANES), F32), jax.ShapeDtypeStruct((b, LANES, groups), F32)],
        compiler_params=pltpu.CompilerParams(dimension_semantics=("arbitrary",), vmem_limit_bytes=VMEM_LIMIT),
        name="compress_prompt",
    )(cmp_tm3, *ws)


def _compress_paged_call(cache5, layer, page_table, lw):
    b, n_pages = page_table.shape
    past = n_pages * PAGE_SIZE
    groups = past // NSA_CMP_STRIDE
    n_seq = 2 if b % 2 == 0 else 1
    full = lambda a: pl.BlockSpec(a.shape, lambda i, pt: (0,) * a.ndim)
    ws = [lw['cmp_wa'], lw['cmp_wb'], lw['cmp_pt'], lw['cmp_pb'], lw['cmp_w2']]
    page_specs = [pl.BlockSpec((1, 1, 2, HEAD_DIM, PAGE_SIZE), functools.partial(
        lambda i, pt, s, k: (layer, pt[i * n_seq + s, k], 0, 0, 0), s=s, k=k))
        for s in range(n_seq) for k in range(n_pages)]
    grid_spec = pltpu.PrefetchScalarGridSpec(
        num_scalar_prefetch=1, grid=(b // n_seq,),
        in_specs=page_specs + [full(a) for a in ws],
        out_specs=pl.BlockSpec((n_seq, LANES, groups), lambda i, pt: (i, 0, 0)),
        scratch_shapes=[pltpu.VMEM((n_seq, past, LANES), F32)])
    return pl.pallas_call(
        functools.partial(_compress_paged_kernel, groups=groups, n_seq=n_seq),
        grid_spec=grid_spec, out_shape=jax.ShapeDtypeStruct((b, LANES, groups), F32),
        compiler_params=pltpu.CompilerParams(dimension_semantics=("arbitrary",), vmem_limit_bytes=VMEM_LIMIT),
        name="compress_sample",
    )(page_table, *([cache5] * (n_seq * n_pages)), *ws)


def _topk_mask(score, n_cols, k):
    idx = _iota(score.shape, 1)
    rank = jnp.zeros(score.shape, jnp.int32)
    for j in range(n_cols):
        col = score[:, j:j + 1]
        rank = rank + jnp.where(col > score, 1, jnp.where(col == score, jnp.where(idx > j, 1, 0), 0))
    return rank < k


def _topk_mask_t(score, n_rows, k):
    n_tiles = score.shape[0] // SUBLANES
    tiles = [score[r * SUBLANES:(r + 1) * SUBLANES, :] for r in range(n_tiles)]
    idx = _iota(tiles[0].shape, 0)
    ranks = [jnp.zeros(tiles[0].shape, jnp.int32) for _ in range(n_tiles)]
    for j in range(n_rows):
        row = score[j:j + 1, :]
        for r in range(n_tiles):
            if r * SUBLANES > j:
                ranks[r] = jnp.where(row >= tiles[r], ranks[r] + 1, ranks[r])
            elif (r + 1) * SUBLANES - 1 <= j:
                ranks[r] = jnp.where(row > tiles[r], ranks[r] + 1, ranks[r])
            else:
                ranks[r] = ranks[r] + jnp.where(idx + r * SUBLANES > j, jnp.where(row >= tiles[r], 1, 0),
                                                jnp.where(row > tiles[r], 1, 0))
    return jnp.concatenate(ranks, axis=0) < k


def _softmax_axis(s, ok, axis):
    s = jnp.where(ok, s, NEG)
    m = jnp.max(s, axis=axis, keepdims=True)
    p = jnp.where(ok, jnp.exp(s - m), 0.0)
    l = jnp.sum(p, axis=axis, keepdims=True)
    return p / jnp.where(l > 0.0, l, 1.0)


def _forced_importance(imp, blk, cur):
    forced = (blk == 0) | (blk == cur) | (blk == cur - 1)
    imp = jnp.where(forced, BIG, imp)
    return jnp.where(blk > cur, NEG, imp)


def _nsa_prompt_kernel(qt_ref, gt_ref, kcvc_ref, kcvct_ref, slc_ref, win_ref, vt_ref, o_ref, sel_ref, s_ref, *, t, qb, tk):
    iq = pl.program_id(1)
    s0 = iq * qb
    nq = NSA_HEADS * qb
    qt = qt_ref[...] * (HEAD_DIM ** -0.5 * LOG2E)
    qst = jnp.concatenate([qt[h * HEAD_DIM:(h + 1) * HEAD_DIM, :] for h in range(NSA_HEADS)], axis=1)
    qst = jnp.concatenate([qst, jnp.zeros_like(qst)], axis=0).astype(BF16)
    qpos_q = s0 + _iota((1, qb), 1)
    qpos = _lane_tile(qpos_q, NSA_HEADS)

    wlen = NSA_WINDOW + qb
    start = pl.multiple_of(jnp.maximum(s0 - NSA_WINDOW, 0), qb)
    sw = _dot(win_ref[0, pl.ds(start, wlen), :], qst)
    back = (qpos - start) - _iota(sw.shape, 0)
    sw = jnp.where((back & -NSA_WINDOW) == 0, sw, NEG)
    ew = jnp.exp2(sw - jnp.max(sw, axis=0, keepdims=True))
    o_win = (_dot(vt_ref[0, HEAD_DIM:2 * HEAD_DIM, pl.ds(start, wlen)], ew.astype(BF16))
             / jnp.sum(ew, axis=0, keepdims=True))

    n_cmp = (t - NSA_CMP_LEN) // NSA_CMP_STRIDE + 1
    n_blk = t // NSA_SLC_BLOCK
    kcvc = kcvc_ref[0].astype(BF16)
    g = kcvc.shape[0]
    s = _dot(kcvc, qst)
    n_idx = _iota(s.shape, 0)
    ok = (n_idx * NSA_CMP_STRIDE + (NSA_CMP_LEN - 1) <= qpos) & (n_idx < n_cmp)
    s = jnp.where(ok, s, NEG)
    m = jnp.max(s, axis=0, keepdims=True)
    e = jnp.exp2(s - m)
    p = e * jnp.where(m > 0.5 * NEG, 1.0 / jnp.sum(e, axis=0, keepdims=True), 0.0)
    o_cmp = _dot(kcvct_ref[0, HEAD_DIM:2 * HEAD_DIM, :].astype(BF16), p.astype(BF16))
    psum = p[:, 0:qb]
    for h in range(1, NSA_HEADS):
        psum = psum + p[:, h * qb:(h + 1) * qb]
    cj = _iota((n_blk, g), 0)
    cn = _iota((n_blk, g), 1)
    cover = ((cn * NSA_CMP_STRIDE <= cj * NSA_SLC_BLOCK + (NSA_SLC_BLOCK - 1))
             & (cn * NSA_CMP_STRIDE + (NSA_CMP_LEN - 1) >= cj * NSA_SLC_BLOCK) & (cn < n_cmp)).astype(BF16)
    p_hi, p_lo = _split(psum)
    imp = _dot(cover, p_hi) + _dot(cover, p_lo)
    imp = _forced_importance(imp, _iota(imp.shape, 0), qpos_q // NSA_SLC_BLOCK)
    cur = qpos_q // NSA_SLC_BLOCK
    picked = _topk_mask_t(imp, n_blk, NSA_TOPN) & (_iota(imp.shape, 0) <= cur)
    sel_ref[...] = _lane_tile(jnp.where(picked, 0.0, NEG), NSA_HEADS)

    bpt = tk // NSA_SLC_BLOCK
    q64 = qst[0:HEAD_DIM].astype(F32)
    zpad = jnp.zeros((LANES - HEAD_DIM - bpt, nq), F32)
    kd = s0 // tk

    def scores(kt):
        k0 = pl.multiple_of(kt * tk, tk)
        slab = sel_ref[pl.ds(pl.multiple_of(kt * bpt, bpt), bpt), :]
        w = jnp.concatenate([q64, slab, zpad], axis=0).astype(BF16)
        return _dot(slc_ref[0, pl.ds(k0, tk), :], w)

    def fold(x, op):
        return op(x.reshape(tk // SUBLANES, SUBLANES, nq), axis=0)

    def sweep1(kt, m8):
        sc = scores(kt)
        s_ref[kt] = sc
        return jnp.maximum(m8, fold(sc, jnp.max))

    m8 = lax.fori_loop(0, kd, sweep1, jnp.full((SUBLANES, nq), NEG, F32))
    sc = scores(kd)
    sc = jnp.where(kd * tk + _iota(sc.shape, 0) <= qpos, sc, NEG)
    s_ref[kd] = sc
    m = jnp.max(jnp.maximum(m8, fold(sc, jnp.max)), axis=0, keepdims=True)

    def sweep2(kt, carry):
        l8, acc = carry
        k0 = pl.multiple_of(kt * tk, tk)
        pp = jnp.exp2(s_ref[kt] - m)
        return l8 + fold(pp, jnp.sum), acc + _dot(vt_ref[0, 0:HEAD_DIM, pl.ds(k0, tk)], pp.astype(BF16))

    l8, acc = lax.fori_loop(0, kd + 1, sweep2, (jnp.zeros((SUBLANES, nq), F32), jnp.zeros((HEAD_DIM, nq), F32)))
    o_slc = acc / jnp.sum(l8, axis=0, keepdims=True)

    gt = gt_ref[...]
    outs = []
    for h in range(NSA_HEADS):
        c = slice(h * qb, (h + 1) * qb)
        outs.append(gt[3 * h:3 * h + 1, :] * o_cmp[:, c] + gt[3 * h + 1:3 * h + 2, :] * o_slc[:, c]
                    + gt[3 * h + 2:3 * h + 3, :] * o_win[:, c])
    o_ref[...] = jnp.concatenate(outs, axis=0)


def _nsa_prompt_call(qt, gt, kcvc, kcvct, slc_tm3, win_tm3, vt3, *, qb=256, tk=SLC_TILE):
    b, t, _ = slc_tm3.shape
    nq = t // qb
    g = kcvc.shape[1]
    assert t % tk == 0 and tk % qb == 0 and t >= NSA_WINDOW + qb
    return pl.pallas_call(
        functools.partial(_nsa_prompt_kernel, t=t, qb=qb, tk=tk),
        grid=(b, nq),
        in_specs=[pl.BlockSpec((256, qb), lambda i, j: (0, i * nq + j)),
                  pl.BlockSpec((2 * SUBLANES, qb), lambda i, j: (0, i * nq + j)),
                  pl.BlockSpec((1, g, LANES), lambda i, j: (i, 0, 0)),
                  pl.BlockSpec((1, LANES, g), lambda i, j: (i, 0, 0)),
                  pl.BlockSpec((1, t, LANES), lambda i, j: (i, 0, 0)),
                  pl.BlockSpec((1, t, LANES), lambda i, j: (i, 0, 0)),
                  pl.BlockSpec((1, LANES, t), lambda i, j: (i, 2, 0))],
        out_specs=pl.BlockSpec((256, qb), lambda i, j: (0, i * nq + j)),
        out_shape=jax.ShapeDtypeStruct((256, b * t), F32),
        scratch_shapes=[pltpu.VMEM((t // NSA_SLC_BLOCK, NSA_HEADS * qb), F32),
                        pltpu.VMEM((t // tk, tk, NSA_HEADS * qb), F32)],
        compiler_params=pltpu.CompilerParams(dimension_semantics=("arbitrary", "arbitrary"),
                                             vmem_limit_bytes=VMEM_LIMIT),
        name="nsa_prompt",
    )(qt, gt, kcvc, kcvct, slc_tm3, win_tm3, vt3)


def _nsa_sample_kernel(ptab_ref, *refs, past, tq, n_seq):
    del ptab_ref
    n_in = len(refs) - 10
    n_pages = n_in // n_seq
    pages = refs[:n_in]
    q_ref, g_ref, kcvct_ref, new_ref, wst_ref, wnew_ref, o_ref, kt_ref, vt_ref, e_ref = refs[n_in:]

    @pl.when(pl.program_id(0) == 0)
    def _():
        e_ref[...] = (_iota(e_ref.shape, 0) == _iota(e_ref.shape, 1) // NSA_SLC_BLOCK).astype(BF16)

    for sq in range(n_seq):
        for k in range(n_pages):
            pg = pages[sq * n_pages + k][0, 0]
            kt_ref[sq, :, k * PAGE_SIZE:(k + 1) * PAGE_SIZE] = pg[0].astype(BF16)
            vt_ref[sq, :, k * PAGE_SIZE:(k + 1) * PAGE_SIZE] = pg[1].astype(BF16)

    for sq in range(n_seq):
        rows = NSA_HEADS * tq
        qs = (q_ref[sq] * (HEAD_DIM ** -0.5)).astype(BF16)
        qpos_q = past + _iota((tq, 1), 0)
        qpos = jnp.concatenate([qpos_q] * NSA_HEADS, axis=0)
        l_all = past + tq
        n_cmp = (l_all - NSA_CMP_LEN) // NSA_CMP_STRIDE + 1
        n_blk = -(-l_all // NSA_SLC_BLOCK)
        nb_pad = -(-n_blk // LANES) * LANES

        kcvct = kcvct_ref[sq].astype(BF16)
        g = kcvct.shape[1]
        s = _dot(qs, kcvct[0:HEAD_DIM])
        n_idx = _iota(s.shape, 1)
        ok = (n_idx * NSA_CMP_STRIDE + (NSA_CMP_LEN - 1) <= qpos) & (n_idx < n_cmp)
        p = _softmax_axis(s, ok, 1)
        o_cmp = _dot_nt(p.astype(BF16), kcvct[HEAD_DIM:2 * HEAD_DIM])
        psum = p[0:tq]
        for h in range(1, NSA_HEADS):
            psum = psum + p[h * tq:(h + 1) * tq]
        cn = _iota((g, nb_pad), 0)
        cj = _iota((g, nb_pad), 1)
        cover = ((cn * NSA_CMP_STRIDE <= cj * NSA_SLC_BLOCK + (NSA_SLC_BLOCK - 1))
                 & (cn * NSA_CMP_STRIDE + (NSA_CMP_LEN - 1) >= cj * NSA_SLC_BLOCK) & (cn < n_cmp)).astype(BF16)
        p_hi, p_lo = _split(psum)
        imp = _dot(p_hi, cover) + _dot(p_lo, cover)
        imp = _forced_importance(imp, _iota(imp.shape, 1), qpos_q // NSA_SLC_BLOCK)
        picked = _topk_mask(imp, n_blk, NSA_TOPN)

        nbe = e_ref.shape[0]
        bias = jnp.where(picked[:, 0:nbe], 0.0, NEG).astype(BF16)
        bias = jnp.concatenate([bias] * NSA_HEADS, axis=0)
        sc = _dot(qs, kt_ref[sq]) + _dot(bias, e_ref[...])
        new_ok = past + _iota((rows, PAGE_SIZE), 1) <= qpos
        sc_new = jnp.where(new_ok, _dot(qs, new_ref[sq, 0].astype(BF16)), NEG)
        m = jnp.maximum(jnp.max(sc, axis=-1, keepdims=True), jnp.max(sc_new, axis=-1, keepdims=True))
        pp = jnp.exp(sc - m)
        pp_new = jnp.exp(sc_new - m)
        l = jnp.sum(pp, axis=-1, keepdims=True) + jnp.sum(pp_new, axis=-1, keepdims=True)
        o_slc = (_dot_nt(pp.astype(BF16), vt_ref[sq])
                 + _dot_nt(pp_new.astype(BF16), new_ref[sq, 1].astype(BF16))) / l

        kw = jnp.concatenate([wst_ref[0, sq, 0], wnew_ref[sq, 0]], axis=1).astype(BF16)
        vw = jnp.concatenate([wst_ref[0, sq, 1], wnew_ref[sq, 1]], axis=1).astype(BF16)
        sw = _dot(qs, kw)
        kpos = past - NSA_WINDOW + _iota(sw.shape, 1)
        okw = (kpos <= qpos) & (kpos > qpos - NSA_WINDOW) & (kpos >= 0)
        o_win = _dot_nt(_softmax_axis(sw, okw, 1).astype(BF16), vw)

        gg = g_ref[sq]
        o_ref[sq] = gg[:, 0:1] * o_cmp + gg[:, 1:2] * o_slc + gg[:, 2:3] * o_win


def _nsa_sample_call(cache5, layer, page_table, q32, g32, kcvct, newt, wstate5, wnewt):
    b, n_pages = page_table.shape
    past = n_pages * PAGE_SIZE
    tq = q32.shape[1] // NSA_HEADS
    n_seq = 2 if b % 2 == 0 else 1
    nbe = -(-(past // NSA_SLC_BLOCK) // LANES) * LANES
    assert tq <= NSA_SLC_BLOCK and past >= NSA_WINDOW
    assert (past + tq - NSA_CMP_LEN) // NSA_CMP_STRIDE + 1 <= past // NSA_CMP_STRIDE - 1
    rows = NSA_HEADS * tq
    page_specs = [pl.BlockSpec((1, 1, 2, HEAD_DIM, PAGE_SIZE), functools.partial(
        lambda i, pt, s, k: (layer, pt[i * n_seq + s, k], 1, 0, 0), s=s, k=k))
        for s in range(n_seq) for k in range(n_pages)]
    per_b = lambda a: pl.BlockSpec((n_seq,) + a.shape[1:], lambda i, pt: (i,) + (0,) * (a.ndim - 1))
    grid_spec = pltpu.PrefetchScalarGridSpec(
        num_scalar_prefetch=1, grid=(b // n_seq,),
        in_specs=page_specs + [per_b(q32), per_b(g32), per_b(kcvct), per_b(newt),
                               pl.BlockSpec((1, n_seq, 2, HEAD_DIM, NSA_WINDOW), lambda i, pt: (layer, i, 0, 0, 0)),
                               per_b(wnewt)],
        out_specs=pl.BlockSpec((n_seq, rows, HEAD_DIM), lambda i, pt: (i, 0, 0)),
        scratch_shapes=[pltpu.VMEM((n_seq, HEAD_DIM, past), BF16), pltpu.VMEM((n_seq, HEAD_DIM, past), BF16),
                        pltpu.VMEM((nbe, past), BF16)])
    return pl.pallas_call(
        functools.partial(_nsa_sample_kernel, past=past, tq=tq, n_seq=n_seq),
        grid_spec=grid_spec, out_shape=jax.ShapeDtypeStruct((b, rows, HEAD_DIM), F32),
        compiler_params=pltpu.CompilerParams(dimension_semantics=("arbitrary",), vmem_limit_bytes=VMEM_LIMIT),
        name="nsa_sample",
    )(page_table, *([cache5] * (n_seq * n_pages)), q32, g32, kcvct, newt, wstate5, wnewt)


def _moba_prompt_kernel(qt_ref, kmean_ref, k_ref, vt_ref, o_ref, sel_ref, acc_ref, s_ref, *, nb):
    i = pl.program_id(1)
    r0 = pl.multiple_of(i * MOBA_BLOCK, MOBA_BLOCK)
    qb = MOBA_BLOCK
    nq = MOBA_HEADS * qb
    qt = qt_ref[...]
    rowgrp = _iota(qt.shape, 0) // HEAD_DIM
    kmean = kmean_ref[0]
    qpad = []
    for h in range(MOBA_HEADS):
        qh = jnp.where(rowgrp == h, qt, 0.0)
        gs = _dot_3pass(kmean, qh)
        n_idx = _iota(gs.shape, 0)
        gs = jnp.where(n_idx < i, gs, NEG)
        sel_ref[:, h * qb:(h + 1) * qb] = jnp.where(_topk_mask_t(gs, nb, MOBA_TOPK) & (n_idx < i), 0.0, NEG)
        qpad.append((qh * (HEAD_DIM ** -0.5 * LOG2E)).astype(BF16))
    qcat = jnp.concatenate(qpad, axis=1)

    def block(n, width=1):
        k0 = pl.multiple_of(n * MOBA_BLOCK, width * MOBA_BLOCK)
        return k_ref[0, pl.ds(k0, width * MOBA_BLOCK), :], vt_ref[0, :, pl.ds(k0, width * MOBA_BLOCK)]

    def fold(x, op):
        return op(x.reshape(x.shape[0] // SUBLANES, SUBLANES, nq), axis=0)

    n_pairs = (i + 1) // 2

    def sweep1(n2, m8):
        bias = jnp.stack([sel_ref[pl.ds(2 * n2, 1), :], sel_ref[pl.ds(2 * n2 + 1, 1), :]])
        sc = (_dot(block(2 * n2, 2)[0], qcat).reshape(2, MOBA_BLOCK, nq) + bias).reshape(2 * MOBA_BLOCK, nq)
        s_ref[n2] = sc
        return jnp.maximum(m8, fold(sc, jnp.max))

    m8 = lax.fori_loop(0, n_pairs, sweep1, jnp.full((SUBLANES, nq), NEG, F32))
    own_k, own_v = block(i)
    causal = _iota((MOBA_BLOCK, nq), 0) <= _iota((MOBA_BLOCK, nq), 1) % qb
    own_s = jnp.where(causal, _dot(own_k, qcat), NEG)
    m = jnp.max(jnp.maximum(m8, fold(own_s, jnp.max)), axis=0, keepdims=True)

    acc_ref[...] = jnp.zeros_like(acc_ref)

    def accumulate(p, vb):
        pb = p.astype(BF16)
        for h in range(MOBA_HEADS):
            acc_ref[h] += _dot(vb[h * HEAD_DIM:(h + 1) * HEAD_DIM, :], pb[:, h * qb:(h + 1) * qb])
        return fold(p, jnp.sum)

    def sweep2(n2, l8):
        return l8 + accumulate(jnp.exp2(s_ref[n2] - m), block(2 * n2, 2)[1])

    l8 = lax.fori_loop(0, n_pairs, sweep2, jnp.zeros((SUBLANES, nq), F32))
    l = jnp.sum(l8 + accumulate(jnp.exp2(own_s - m), own_v), axis=0, keepdims=True)
    o_ref[...] = jnp.concatenate([acc_ref[h] / l[:, h * qb:(h + 1) * qb] for h in range(MOBA_HEADS)], axis=0)


def _moba_prompt_call(qt, kmean3, mk_tm3, vt3):
    b, t, _ = mk_tm3.shape
    nb = t // MOBA_BLOCK
    nbp = kmean3.shape[1]
    return pl.pallas_call(
        functools.partial(_moba_prompt_kernel, nb=nb),
        grid=(b, nb),
        in_specs=[pl.BlockSpec((256, MOBA_BLOCK), lambda i, j: (1, i * nb + j)),
                  pl.BlockSpec((1, nbp, 256), lambda i, j: (i, 0, 0)),
                  pl.BlockSpec((1, t, 256), lambda i, j: (i, 0, 0)),
                  pl.BlockSpec((1, 256, t), lambda i, j: (i, 0, 0))],
        out_specs=pl.BlockSpec((256, MOBA_BLOCK), lambda i, j: (0, i * nb + j)),
        out_shape=jax.ShapeDtypeStruct((256, b * t), F32),
        scratch_shapes=[pltpu.VMEM((nbp, MOBA_HEADS * MOBA_BLOCK), F32),
                        pltpu.VMEM((MOBA_HEADS, HEAD_DIM, MOBA_BLOCK), F32),
                        pltpu.VMEM((nb // 2, 2 * MOBA_BLOCK, MOBA_HEADS * MOBA_BLOCK), F32)],
        compiler_params=pltpu.CompilerParams(dimension_semantics=("arbitrary", "arbitrary"),
                                             vmem_limit_bytes=VMEM_LIMIT),
        name="moba_prompt",
    )(qt, kmean3, mk_tm3, vt3)


def _moba_sample_kernel(ptab_ref, *refs, past, tq, n_pages):
    del ptab_ref
    pages = refs[:n_pages]
    q_ref, new_ref, o_ref, s_ref, e_ref = refs[n_pages:]
    ppb = MOBA_BLOCK // PAGE_SIZE
    n_past = past // MOBA_BLOCK

    @pl.when(pl.program_id(0) == 0)
    def _():
        e_ref[...] = (_iota(e_ref.shape, 0) == _iota(e_ref.shape, 1) // MOBA_BLOCK).astype(BF16)

    q = q_ref[0]
    rows = q.shape[0]
    qb = (q * (HEAD_DIM ** -0.5)).astype(BF16)

    def block(n, which):
        return jnp.concatenate([pages[n * ppb + c][0, 0, which] for c in range(ppb)], axis=1)

    lane = _iota((256, LANES), 1)
    kmean = jnp.zeros((256, LANES), F32)
    for n in range(n_past):
        kb = block(n, 0)
        kmean = jnp.where(lane == n, jnp.sum(kb, axis=1, keepdims=True) * (1.0 / MOBA_BLOCK), kmean)
        s_ref[:, n * MOBA_BLOCK:(n + 1) * MOBA_BLOCK] = _dot(qb, kb.astype(BF16))
    gs = _dot_3pass(q, kmean)
    n_idx = _iota(gs.shape, 1)
    gs = jnp.where(n_idx < n_past, gs, NEG)
    bias = jnp.where(_topk_mask(gs, n_past, MOBA_TOPK) & (n_idx < n_past), 0.0, NEG).astype(BF16)
    s = s_ref[...] + _dot(bias, e_ref[...])
    own_ok = _iota((rows, PAGE_SIZE), 1) <= _iota((rows, PAGE_SIZE), 0) % tq
    s_own = jnp.where(own_ok, _dot(qb, new_ref[0, 0].astype(BF16)), NEG)
    m = jnp.maximum(jnp.max(s, axis=-1, keepdims=True), jnp.max(s_own, axis=-1, keepdims=True))
    p = jnp.exp(s - m)
    p_own = jnp.exp(s_own - m)
    l = jnp.sum(p, axis=-1, keepdims=True) + jnp.sum(p_own, axis=-1, keepdims=True)
    pb = p.astype(BF16)
    acc = _dot_nt(p_own.astype(BF16), new_ref[0, 1].astype(BF16))
    for n in range(n_past):
        acc = acc + _dot_nt(pb[:, n * MOBA_BLOCK:(n + 1) * MOBA_BLOCK], block(n, 1).astype(BF16))
    oh = acc / l
    grp = _iota((tq, 256), 1) // HEAD_DIM
    out = oh[0:tq]
    for h in range(1, MOBA_HEADS):
        out = jnp.where(grp == h, oh[h * tq:(h + 1) * tq], out)
    o_ref[0] = out


def _moba_sample_call(cache5, layer, page_table, qbd, newt):
    b, n_pages = page_table.shape
    past = n_pages * PAGE_SIZE
    rows = qbd.shape[1]
    tq = rows // MOBA_HEADS
    assert past % MOBA_BLOCK == 0 and tq <= PAGE_SIZE and past // MOBA_BLOCK <= LANES
    page_specs = [pl.BlockSpec((1, 1, 2, 256, PAGE_SIZE), functools.partial(
        lambda i, pt, k: (layer, pt[i, k], 0, 0, 0), k=k)) for k in range(n_pages)]
    per_b = lambda a: pl.BlockSpec((1,) + a.shape[1:], lambda i, pt: (i,) + (0,) * (a.ndim - 1))
    grid_spec = pltpu.PrefetchScalarGridSpec(
        num_scalar_prefetch=1, grid=(b,),
        in_specs=page_specs + [per_b(qbd), per_b(newt)],
        out_specs=pl.BlockSpec((1, tq, 256), lambda i, pt: (i, 0, 0)),
        scratch_shapes=[pltpu.VMEM((rows, past), F32), pltpu.VMEM((LANES, past), BF16)])
    return pl.pallas_call(
        functools.partial(_moba_sample_kernel, past=past, tq=tq, n_pages=n_pages),
        grid_spec=grid_spec, out_shape=jax.ShapeDtypeStruct((b, tq, 256), F32),
        compiler_params=pltpu.CompilerParams(dimension_semantics=("arbitrary",), vmem_limit_bytes=VMEM_LIMIT),
        name="moba_sample",
    )(page_table, *([cache5] * n_pages), qbd, newt)


def _merge_kernel(x_ref, oab_ref, oct_ref, odt_ref, n1_ref, wgt_ref, wbr_ref, wo_ref, y_ref, *, subs):
    sm = x_ref.shape[0] // subs
    for si in range(subs):
        rs = slice(si * sm, (si + 1) * sm)
        x = x_ref[rs, :]
        xn = (x * lax.rsqrt(jnp.mean(x * x, axis=-1, keepdims=True) + EPS) * n1_ref[...]).astype(BF16)
        branches = [oab_ref[rs, 0:256], oab_ref[rs, 256:512], oct_ref[:, rs].T.astype(BF16),
                    odt_ref[:, rs].T.astype(BF16)]
        merged = jnp.zeros(x.shape, F32)
        for n in range(N_BRANCH):
            g = jax.nn.sigmoid(_dot_nt(xn, wgt_ref[n * D_MODEL:(n + 1) * D_MODEL, :]))
            merged = merged + g * _dot(branches[n], wbr_ref[n])
        y_ref[rs, :] = x + _dot(merged.astype(BF16), wo_ref[...])


def _merge_call(x2d, oab, oct, odt, lw, *, tm):
    rows = x2d.shape[0]
    row = lambda w: pl.BlockSpec((tm, w), lambda i: (i, 0))
    col = lambda r: pl.BlockSpec((r, tm), lambda i: (0, i))
    full = lambda a: pl.BlockSpec(a.shape, lambda i: (0,) * a.ndim)
    ws = [lw['norm1'], lw['wg_t'], lw['w_branch'], lw['w_o']]
    return pl.pallas_call(
        functools.partial(_merge_kernel, subs=2 if tm % (2 * LANES) == 0 else 1), grid=(rows // tm,),
        in_specs=[row(D_MODEL), row(512), col(256), col(256)] + [full(a) for a in ws],
        out_specs=row(D_MODEL), out_shape=jax.ShapeDtypeStruct((rows, D_MODEL), F32),
        compiler_params=pltpu.CompilerParams(dimension_semantics=("arbitrary",), vmem_limit_bytes=VMEM_LIMIT),
        name="merge",
    )(x2d, oab, oct, odt, *ws)


def _ffn_kernel(*refs, tiles_per_seq, seg, subs):
    it = iter(refs)
    x_ref, n2_ref, wgate_ref, wup_ref, cw_ref, cb_ref, wdown_ref = (next(it) for _ in range(7))
    if seg is not None:
        h1_ref, h2_ref = next(it), next(it)
    y_ref, tail_ref = next(it), next(it)
    if seg is None:
        prev_ref = next(it)

    tm = x_ref.shape[0]
    sm = tm // subs
    if seg is None:
        @pl.when(pl.program_id(0) % tiles_per_seq == 0)
        def _():
            prev_ref[...] = jnp.zeros_like(prev_ref)

        prev = prev_ref[...]

    for si in range(subs):
        rs = slice(si * sm, (si + 1) * sm)
        x = x_ref[rs, :]
        h2 = (x * lax.rsqrt(jnp.mean(x * x, axis=-1, keepdims=True) + EPS) * n2_ref[...]).astype(BF16)
        gate = _dot(h2, wgate_ref[...])
        if seg is None:
            g1, g2 = _shift_rows(gate, prev[SUBLANES - 1:SUBLANES, :], prev[SUBLANES - 2:SUBLANES - 1, :], None)
            prev = gate[sm - SUBLANES:, :]
        else:
            g1, g2 = _shift_rows(gate, h1_ref[...], h2_ref[...], seg)
            tail_ref[...] = gate.reshape(tail_ref.shape)
        ac = g2 * cw_ref[0:1, :] + cb_ref[...] + g1 * cw_ref[1:2, :] + gate * cw_ref[2:3, :]
        act = jax.nn.silu(ac) * _dot(h2, wup_ref[...])
        y_ref[rs, :] = x + _dot(act.astype(BF16), wdown_ref[...])

    if seg is None:
        prev_ref[...] = prev
        tail_ref[0] = prev


def _ffn_call(x2d, lw, halos, *, tm, tiles_per_seq, n_seq, seg):
    rows = x2d.shape[0]
    row = lambda w: pl.BlockSpec((tm, w), lambda i: (i, 0))
    full = lambda a: pl.BlockSpec(a.shape, lambda i: (0,) * a.ndim)
    ws = [lw['norm2'], lw['w_gate'], lw['w_up'], lw['ffn_conv_w'], lw['ffn_conv_b'], lw['w_down']]
    ins = [x2d] + ws
    in_specs = [row(D_MODEL)] + [full(a) for a in ws]
    scratch = []
    if seg is None:
        tail_spec = pl.BlockSpec((1, SUBLANES, D_FF), lambda i: (i // tiles_per_seq, 0, 0))
        scratch.append(pltpu.VMEM((SUBLANES, D_FF), F32))
    else:
        ins += list(halos)
        in_specs += [row(D_FF), row(D_FF)]
        tail_spec = pl.BlockSpec((n_seq, SUBLANES, D_FF), lambda i: (0, 0, 0))
    return pl.pallas_call(
        functools.partial(_ffn_kernel, tiles_per_seq=tiles_per_seq, seg=seg, subs=2 if seg is None else 1),
        grid=(rows // tm,), in_specs=in_specs, out_specs=[row(D_MODEL), tail_spec],
        out_shape=[jax.ShapeDtypeStruct((rows, D_MODEL), F32), jax.ShapeDtypeStruct((n_seq, SUBLANES, D_FF), F32)],
        scratch_shapes=scratch,
        compiler_params=pltpu.CompilerParams(dimension_semantics=("arbitrary",), vmem_limit_bytes=VMEM_LIMIT),
        name="ffn_prompt" if seg is None else "ffn_sample",
    )(*ins)


def _rope_tables_t(pos):
    half = ROT_DIM // 2
    inv = jnp.power(jnp.float32(ROPE_THETA), -jnp.arange(half, dtype=F32) / half)
    ang = inv[:, None] * pos.astype(F32)[None, :]
    return jnp.cos(ang), jnp.sin(ang)


def _layer_weights(l, p, sample_tq, dec_batch):
    wt = jnp.transpose(p['w_in'], (2, 0, 1))[:, l, :]
    o_ck, o_cv, o_g = 1536, 1728, 1920
    o_dq = o_g + 3 * NSA_HEADS
    o_br = o_dq + 3 * 256
    hd = HEAD_DIM
    w_tm = jnp.concatenate([wt[0:1280], wt[o_g:o_g + 12], jnp.zeros((LANES - 12, D_MODEL), F32)], axis=0)
    w_tr = jnp.concatenate([
        wt[1280:1536], wt[o_dq:o_dq + 256],
        wt[o_ck:o_ck + hd], wt[o_cv:o_cv + hd], wt[o_ck + hd:o_ck + 2 * hd], wt[o_cv + hd:o_cv + 2 * hd],
        wt[o_ck + 2 * hd:o_ck + 3 * hd], wt[o_cv + 2 * hd:o_cv + 3 * hd],
        wt[o_dq + 256:o_dq + 768], wt[o_g:o_g + 12], jnp.zeros((2 * SUBLANES - 12, D_MODEL), F32)], axis=0)
    lw = {'w_tm': w_tm.astype(BF16), 'w_tr': w_tr.astype(BF16), 'wg_t': wt[o_br:].astype(BF16)}
    row = lambda a: a.reshape(1, -1)
    lw['norm1'] = row(p['norm1'][l])
    lw['norm2'] = row(p['norm2'][l])
    lw['conv_w'] = p['conv_w'][l]
    lw['conv_b'] = row(p['conv_b'][l])
    lw['ln_g'] = row(p['gmlp_ln_g'][l])
    lw['ln_b'] = row(p['gmlp_ln_b'][l])
    tril = jnp.tril(jnp.ones((GMLP_CHUNK, GMLP_CHUNK), bool))
    ws = jnp.where(tril[None], p['gmlp_ws'][l], 0.0)
    bs = p['gmlp_bs'][l]
    lw['mix_p'] = ws.reshape(GMLP_GROUPS * GMLP_CHUNK, GMLP_CHUNK).astype(BF16)
    lw['mixb_p'] = jnp.repeat(bs.T, HEAD_DIM, axis=1)
    eye = jnp.eye(dec_batch, dtype=F32)
    lw['mix_s'] = jnp.concatenate([jnp.kron(eye, ws[g, :sample_tq, :sample_tq]) for g in range(GMLP_GROUPS)],
                                  axis=0).astype(BF16)
    lw['mixb_s'] = jnp.tile(jnp.repeat(bs.T[:sample_tq], HEAD_DIM, axis=1), (dec_batch, 1))
    kg = p['nsa_k_gain'][l]
    one = jnp.ones((hd,), F32)
    gains = jnp.concatenate([jnp.tile(p['nsa_q_gain'][l], NSA_HEADS), jnp.tile(p['moba_q_gain'][l], MOBA_HEADS),
                             kg[0], one, kg[1], one, kg[2], one, jnp.tile(p['moba_k_gain'][l], MOBA_HEADS)])
    lw['gain_t'] = jnp.broadcast_to(gains[:, None], (R_MV, LANES))
    w1c = p['nsa_cmp_w1'][l].reshape(2, NSA_CMP_LEN, HEAD_DIM, NSA_CMP_HIDDEN)
    pos = p['nsa_cmp_pos'][l]
    zero = jnp.zeros((HEAD_DIM, NSA_CMP_HIDDEN), F32)

    def pair_w(r_off):
        mats = []
        for pp in range(NSA_CMP_STRIDE // 2):
            blocks = []
            for r in (2 * pp, 2 * pp + 1):
                blocks.append(jnp.concatenate([w1c[0, r + r_off], zero], axis=1))
                blocks.append(jnp.concatenate([zero, w1c[1, r + r_off]], axis=1))
            mats.append(jnp.concatenate(blocks, axis=0))
        return jnp.stack(mats).astype(BF16)

    def pair_pos(r_off):
        return jnp.stack([jnp.concatenate([pos[0, 2 * pp + r_off], pos[1, 2 * pp + r_off],
                                           pos[0, 2 * pp + 1 + r_off], pos[1, 2 * pp + 1 + r_off]])
                          for pp in range(NSA_CMP_STRIDE // 2)])

    lw['cmp_wa'], lw['cmp_wb'] = pair_w(0), pair_w(NSA_CMP_STRIDE)
    lw['cmp_pt'], lw['cmp_pb'] = pair_pos(0), pair_pos(NSA_CMP_STRIDE)
    w2 = p['nsa_cmp_w2'][l]
    z2 = jnp.zeros((NSA_CMP_HIDDEN, HEAD_DIM), F32)
    lw['cmp_w2'] = jnp.concatenate([jnp.concatenate([w2[0], z2], axis=1),
                                    jnp.concatenate([z2, w2[1]], axis=1)], axis=0).astype(BF16)
    lw['w_branch'] = p['w_branch'][l].astype(BF16)
    lw['w_o'] = p['w_o'][l].astype(BF16)
    lw['w_gate'] = p['w_ffn_gate'][l].astype(BF16)
    lw['w_up'] = p['w_ffn_up'][l].astype(BF16)
    lw['w_down'] = p['w_ffn_down'][l].astype(BF16)
    lw['ffn_conv_w'] = p['ffn_conv_w'][l]
    lw['ffn_conv_b'] = row(p['ffn_conv_b'][l])
    return lw


def _halos(buf, tq):
    b, _, c = buf.shape
    z = jnp.zeros((b, tq - 1, c), F32)
    h1 = jnp.concatenate([buf[:, 1:2], z], axis=1)
    h2 = jnp.concatenate([buf[:, 0:2], z[:, 1:]], axis=1)
    return h1.reshape(b * tq, c), h2.reshape(b * tq, c)


def _row_tile(rows, cap):
    tm = cap
    while rows % tm:
        tm //= 2
    return tm


def _per_seq_pages(xt, b, tq):
    c = xt.shape[0]
    x = xt.reshape(c, b, tq).transpose(1, 0, 2)
    return jnp.concatenate([x, jnp.zeros((b, c, PAGE_SIZE - tq), F32)], axis=-1)


def _prompt_layer(x2d, lw, tables, b, t):
    tm = _row_tile(t, 512)
    assert tm % MOBA_BLOCK == 0 and tm % SLC_TILE == 0
    tps = t // tm
    pw = dict(lw, mix=lw['mix_p'], mixb=lw['mixb_p'])
    (oab, qt, nsat, wint, mobat, gt, _, ztail, cmp_tm, slc_tm, win_tm, mk_tm, vt16, kmean) = _proj_call(
        x2d, pw, tables, None, tm=tm, tiles_per_seq=tps, n_seq=b, chunk=GMLP_CHUNK, seg=None)
    kcvc, kcvct = _compress_prompt_call(cmp_tm.reshape(b, t, LANES), lw)
    o_ct = _nsa_prompt_call(qt, gt, kcvc, kcvct, slc_tm.reshape(b, t, LANES), win_tm.reshape(b, t, LANES), vt16)
    nb = t // MOBA_BLOCK
    kmean3 = kmean[:, 0:tm // MOBA_BLOCK].reshape(b, nb, 256)
    nbp = -(-nb // SUBLANES) * SUBLANES
    if nbp != nb:
        kmean3 = jnp.concatenate([kmean3, jnp.zeros((b, nbp - nb, 256), F32)], axis=1)
    o_dt = _moba_prompt_call(qt, kmean3, mk_tm.reshape(b, t, 256), vt16)
    x_mid = _merge_call(x2d, oab, o_ct, o_dt, lw, tm=tm)
    y, ftail = _ffn_call(x_mid, lw, None, tm=tm, tiles_per_seq=tps, n_seq=b, seg=None)
    hd = HEAD_DIM
    wlen = min(NSA_WINDOW, t)
    state = (nsat.reshape(b, 4, hd, t).transpose(0, 3, 1, 2),
             mobat.reshape(b, 2, MOBA_HEADS, hd, t).transpose(0, 4, 1, 2, 3),
             wint[:, :, t - wlen:].reshape(b, 2, hd, wlen).transpose(0, 3, 1, 2),
             ztail[:, SUBLANES - 2:], ftail[:, SUBLANES - 2:])
    return y, state


def _sample_layer(x2d, lw, tables, b, tq, layer, nsa_cache5, moba_cache5, page_table, win_state5, conv_state,
                  ffn_state):
    rows = b * tq
    hd = HEAD_DIM
    pw = dict(lw, mix=lw['mix_s'], mixb=lw['mixb_s'])
    oab, qt, nsat, wint, mobat, _, gtm, ztail, vn = _proj_call(
        x2d, pw, tables, _halos(conv_state, tq), tm=rows, tiles_per_seq=1, n_seq=b, chunk=rows, seg=tq)
    nsat, wint, mobat = nsat[0], wint[0], mobat[0]
    kcvct = _compress_paged_call(nsa_cache5, layer, page_table, lw)
    q32 = qt[0:256].reshape(NSA_HEADS, hd, b, tq).transpose(2, 0, 3, 1).reshape(b, NSA_HEADS * tq, hd)
    g4 = gtm[:, 0:3 * NSA_HEADS].reshape(b, tq, NSA_HEADS, 3).transpose(0, 2, 1, 3).reshape(b, NSA_HEADS * tq, 3)
    g32 = jnp.concatenate([g4, jnp.zeros((b, NSA_HEADS * tq, LANES - 3), F32)], axis=-1)
    newt = _per_seq_pages(nsat[2 * hd:4 * hd], b, tq).reshape(b, 2, hd, PAGE_SIZE)
    wnewt = _per_seq_pages(wint, b, tq).reshape(b, 2, hd, PAGE_SIZE)
    o32 = _nsa_sample_call(nsa_cache5, layer, page_table, q32, g32, kcvct, newt, win_state5, wnewt)
    o_ct = o32.reshape(b, NSA_HEADS, tq, hd).transpose(1, 3, 0, 2).reshape(256, rows)
    qm = qt[256:512].reshape(MOBA_HEADS, hd, b, tq).transpose(2, 0, 3, 1)
    eye = jnp.eye(MOBA_HEADS, dtype=F32)
    qbd = (qm[:, :, :, None, :] * eye[None, :, None, :, None]).reshape(b, MOBA_HEADS * tq, 256)
    mnewt = _per_seq_pages(mobat, b, tq).reshape(b, 2, 256, PAGE_SIZE)
    o_dt = _moba_sample_call(moba_cache5, layer, page_table, qbd, mnewt).transpose(2, 0, 1).reshape(256, rows)
    x_mid = _merge_call(x2d, oab, o_ct, o_dt, lw, tm=rows)
    y, ftail = _ffn_call(x_mid, lw, _halos(ffn_state, tq), tm=rows, tiles_per_seq=1, n_seq=b, seg=tq)
    wnew = wint.reshape(2, hd, b, tq).transpose(2, 0, 1, 3)
    win_all = jnp.concatenate([win_state5[layer], wnew], axis=-1)
    win_new = win_all[..., win_all.shape[-1] - NSA_WINDOW:].transpose(0, 3, 1, 2)
    state = (nsat.T.reshape(b, tq, 4, hd), mobat.T.reshape(b, tq, 2, MOBA_HEADS, hd), win_new,
             ztail[:, tq - 2:tq], ftail[:, tq - 2:tq], vn.reshape(b, tq, 256))
    return y, state


def kernel(x_prompt, x_sample, cache_nsa_kv, cache_moba_kv, state_nsa_win, state_conv, state_ffn_conv, page_table, norm1, w_in, conv_w, conv_b, gmlp_ln_g, gmlp_ln_b, gmlp_ws, gmlp_bs, nsa_q_gain, nsa_k_gain, nsa_cmp_pos, nsa_cmp_w1, nsa_cmp_w2, moba_q_gain, moba_k_gain, w_branch, w_o, norm2, w_ffn_gate, w_ffn_up, ffn_conv_w, ffn_conv_b, w_ffn_down):
    params = dict(norm1=norm1, w_in=w_in, conv_w=conv_w, conv_b=conv_b, gmlp_ln_g=gmlp_ln_g, gmlp_ln_b=gmlp_ln_b,
                  gmlp_ws=gmlp_ws, gmlp_bs=gmlp_bs, nsa_q_gain=nsa_q_gain, nsa_k_gain=nsa_k_gain,
                  nsa_cmp_pos=nsa_cmp_pos, nsa_cmp_w1=nsa_cmp_w1, nsa_cmp_w2=nsa_cmp_w2, moba_q_gain=moba_q_gain,
                  moba_k_gain=moba_k_gain, w_branch=w_branch, w_o=w_o, norm2=norm2, w_ffn_gate=w_ffn_gate,
                  w_ffn_up=w_ffn_up, ffn_conv_w=ffn_conv_w, ffn_conv_b=ffn_conv_b, w_ffn_down=w_ffn_down)
    bp, tp, _ = x_prompt.shape
    bs, ts, _ = x_sample.shape
    depth = w_in.shape[0]
    n_pool = cache_nsa_kv.shape[1]
    past = page_table.shape[1] * PAGE_SIZE
    assert ts == SUBLANES and state_nsa_win.shape[2] == NSA_WINDOW
    nsa_cache5 = jnp.transpose(cache_nsa_kv, (0, 1, 3, 4, 2))
    moba_cache5 = jnp.transpose(cache_moba_kv, (0, 1, 3, 4, 5, 2)).reshape(depth, n_pool, 2, 256, PAGE_SIZE)
    win_state5 = jnp.transpose(state_nsa_win, (0, 1, 3, 4, 2))
    tab_p = _rope_tables_t(jnp.arange(tp, dtype=jnp.int32))
    tab_s = tuple(jnp.tile(a, (1, bs)) for a in _rope_tables_t(past + jnp.arange(ts, dtype=jnp.int32)))
    yp = x_prompt.reshape(bp * tp, D_MODEL)
    ys = x_sample.reshape(bs * ts, D_MODEL)
    sp, ss = [], []
    for l in range(depth):
        lw = _layer_weights(l, params, ts, bs)
        yp, st_p = _prompt_layer(yp, lw, tab_p, bp, tp)
        ys, st_s = _sample_layer(ys, lw, tab_s, bs, ts, l, nsa_cache5, moba_cache5, page_table, win_state5,
                                 state_conv[l], state_ffn_conv[l])
        sp.append(st_p)
        ss.append(st_s)
    stack = lambda lst, k: jnp.stack([s[k] for s in lst])
    return (yp.reshape(bp, tp, D_MODEL), ys.reshape(bs, ts, D_MODEL),
            stack(sp, 0), stack(ss, 0), stack(sp, 1), stack(ss, 1), stack(sp, 2), stack(ss, 2),
            stack(sp, 3), stack(ss, 3), stack(sp, 4), stack(ss, 4), stack(ss, 5))
```
